```python
import math
import jax, jax.numpy as jnp
from jax import lax
import numpy as np

D_MODEL = 1024
BATCH = 8
SEQ = 2048
DEPTH = 2
DEC_BATCH = 128
DEC_SEQ = 1
PAST_LEN = 16384
PAGE_SIZE = 128

N_META = 16
D_INNER = 2 * D_MODEL
SSM_HEAD_DIM = 64
SSM_HEADS = D_INNER // SSM_HEAD_DIM
SSM_GROUPS = 4
SSM_HPG = SSM_HEADS // SSM_GROUPS
SSM_STATE = 128
CONV_K = 4
CONV_DIM = D_INNER + 2 * SSM_GROUPS * SSM_STATE
SSD_CHUNK = 128
RWKV_DIM = D_MODEL
RWKV_HEAD_DIM = 64
RWKV_HEADS = RWKV_DIM // RWKV_HEAD_DIM
DECAY_LORA = 64
AAA_LORA = 64
GATE_LORA = 128
RWKV_COLS = 3 * RWKV_DIM + DECAY_LORA + AAA_LORA + GATE_LORA
IN_SPLITS = (D_INNER, D_INNER + CONV_DIM, D_INNER + CONV_DIM + SSM_HEADS,
             D_INNER + CONV_DIM + SSM_HEADS + RWKV_COLS)
IN_COLS = IN_SPLITS[-1] + 2 * D_MODEL
RWKV_SPLITS = (RWKV_DIM, 2 * RWKV_DIM, 3 * RWKV_DIM, 3 * RWKV_DIM + DECAY_LORA,
               3 * RWKV_DIM + DECAY_LORA + AAA_LORA)
D_FF = -(-8 * D_MODEL // (3 * 256)) * 256
ALPHA = (2 * DEPTH) ** 0.25
BETA = (8 * DEPTH) ** -0.25
LN_EPS = 1e-5
RMS_EPS = 1e-5
GN_EPS = 64e-5
F32 = jnp.float32

kernel_name = 'hybrid_ssd_rwkv7_gated_decoder_step'


def layer_norm(x, g, b):
    xf = x.astype(F32)
    mu = jnp.mean(xf, axis=-1, keepdims=True)
    var = jnp.mean(jnp.square(xf - mu), axis=-1, keepdims=True)
    return ((xf - mu) * lax.rsqrt(var + LN_EPS)).astype(x.dtype) * g + b


def causal_conv(prefix, u, w, b):
    L = u.shape[1]
    full = jnp.concatenate([prefix.astype(u.dtype), u], axis=1)
    y = b + sum(full[:, k:k + L] * w[k] for k in range(CONV_K))
    return y, full[:, L:]


def ssd_segment(x, dt, a, bm, cm, h0):
    bsz, L = x.shape[:2]
    q = min(SSD_CHUNK, L)
    nc = -(-L // q)
    pad = nc * q - L
    if pad:
        padf = lambda t: jnp.pad(t, [(0, 0), (0, pad)] + [(0, 0)] * (t.ndim - 2))
        x, dt, bm, cm = padf(x), padf(dt), padf(bm), padf(cm)
    ch = lambda t: t.reshape((bsz, nc, q) + t.shape[2:])
    x, dt, bm, cm = ch(x), ch(dt), ch(bm), ch(cm)
    acum = jnp.cumsum((dt * a).astype(F32), axis=2)
    causal = jnp.tril(jnp.ones((q, q), dtype=bool))
    diff = acum[:, :, :, None] - acum[:, :, None, :]
    lmat = jnp.exp(jnp.where(causal[:, :, None, None], diff, -jnp.inf))
    xdt = x * dt[..., None]
    cb = jnp.einsum('bcign,bcjgn->bcijg', cm, bm)
    y_diag = jnp.einsum('bcijg,bcijgr,bcjgrp->bcigrp', cb, lmat, xdt)
    decay_to_end = jnp.exp(acum[:, :, -1:] - acum)
    chunk_states = jnp.einsum('bcjgn,bcjgr,bcjgrp->bcgrpn', bm, decay_to_end, xdt)
    chunk_decay = jnp.exp(acum[:, :, -1])

    def step(h, inp):
        s, d = inp
        return h * d[..., None, None] + s, h

    h_final, h_starts = lax.scan(step, h0.astype(F32),
                                 (jnp.moveaxis(chunk_states, 1, 0), jnp.moveaxis(chunk_decay, 1, 0)))
    h_starts = jnp.moveaxis(h_starts, 0, 1)
    y_off = jnp.einsum('bcign,bcgrpn,bcigr->bcigrp', cm, h_starts, jnp.exp(acum))
    y = (y_diag + y_off).reshape((bsz, nc * q) + x.shape[3:])[:, :L]
    return y, h_final


def mamba_branch(u_z, u_xbc, u_dt, h0, conv_prefix, segments, lw):
    bsz, L = u_z.shape[:2]
    xbc, conv_tail = causal_conv(conv_prefix, u_xbc, lw['conv_w'], lw['conv_b'])
    xbc = jax.nn.silu(xbc)
    xs, bm, cm = jnp.split(xbc, [D_INNER, D_INNER + SSM_GROUPS * SSM_STATE], axis=-1)
    xs = xs.reshape(bsz, L, SSM_GROUPS, SSM_HPG, SSM_HEAD_DIM)
    bm = bm.reshape(bsz, L, SSM_GROUPS, SSM_STATE)
    cm = cm.reshape(bsz, L, SSM_GROUPS, SSM_STATE)
    dt = jax.nn.softplus(u_dt.astype(F32) + lw['dt_bias']).reshape(bsz, L, SSM_GROUPS, SSM_HPG)
    a = -jnp.exp(lw['a_log'].astype(F32)).reshape(SSM_GROUPS, SSM_HPG)
    h = h0.reshape(bsz, SSM_GROUPS, SSM_HPG, SSM_HEAD_DIM, SSM_STATE)
    ys = []
    start = 0
    for seg in segments:
        sl = slice(start, start + seg)
        y_seg, h = ssd_segment(xs[:, sl], dt[:, sl], a, bm[:, sl], cm[:, sl], h)
        ys.append(y_seg)
        start += seg
    y = jnp.concatenate(ys, axis=1) if len(ys) > 1 else ys[0]
    y = y + xs * lw['d_skip'].reshape(SSM_GROUPS, SSM_HPG, 1)
    y = y.reshape(bsz, L, D_INNER) * jax.nn.silu(u_z)
    yg = y.reshape(bsz, L, SSM_GROUPS, D_INNER // SSM_GROUPS).astype(F32)
    yg = yg * lax.rsqrt(jnp.mean(jnp.square(yg), axis=-1, keepdims=True) + RMS_EPS)
    y = yg.reshape(bsz, L, D_INNER).astype(u_z.dtype) * lw['ssm_norm_w']
    return y, h.reshape(bsz, SSM_HEADS, SSM_HEAD_DIM, SSM_STATE), conv_tail


def rwkv_branch(u_r, shift_prev, s0, lw):
    bsz, L = u_r.shape[:2]
    prev = jnp.concatenate([shift_prev[:, None].astype(u_r.dtype), u_r[:, :-1]], axis=1)
    shifted = u_r + (prev - u_r) * lw['shift_mu']
    r, k, v, w_lo, a_lo, g_lo = jnp.split(shifted, RWKV_SPLITS, axis=-1)
    w_log = -jax.nn.softplus(-(lw['w0'] + jnp.tanh(w_lo) @ lw['w_lora_up'])) - 0.5
    decay = jnp.exp(-jnp.exp(w_log.astype(F32)))
    a = jax.nn.sigmoid(lw['a0'] + a_lo @ lw['a_lora_up'])
    g = jax.nn.sigmoid(g_lo) @ lw['g_lora_up']
    hd = lambda t: t.reshape(bsz, L, RWKV_HEADS, RWKV_HEAD_DIM).astype(F32)
    kk = hd(k * lw['k_k'])
    kk = kk * lax.rsqrt(jnp.maximum(jnp.sum(kk * kk, axis=-1, keepdims=True), 1e-24))
    k_mod = k * (1 + (a - 1) * lw['k_a'])
    r_h, k_h, v_h, w_h, a_h = hd(r), hd(k_mod), hd(v), hd(decay), hd(a)

    def step(S, inp):
        r_t, w_t, k_t, v_t, kk_t, a_t = inp
        sa = jnp.einsum('bhij,bhj->bhi', S, -kk_t)
        S = (S * w_t[:, :, None, :] + sa[..., None] * (kk_t * a_t)[:, :, None, :]
             + v_t[..., None] * k_t[:, :, None, :])
        return S, jnp.einsum('bhij,bhj->bhi', S, r_t)

    tm = lambda t: jnp.moveaxis(t, 1, 0)
    s_fin, o = lax.scan(step, s0.astype(F32), (tm(r_h), tm(w_h), tm(k_h), tm(v_h), tm(kk), tm(a_h)))
    o = jnp.moveaxis(o, 0, 1)
    mu = jnp.mean(o, axis=-1, keepdims=True)
    var = jnp.mean(jnp.square(o - mu), axis=-1, keepdims=True)
    on = ((o - mu) * lax.rsqrt(var + GN_EPS)).reshape(bsz, L, RWKV_DIM) * lw['lnx_g'] + lw['lnx_b']
    bonus = jnp.sum(r_h * k_h * lw['r_k'], axis=-1, keepdims=True) * v_h
    out = (on + bonus.reshape(bsz, L, RWKV_DIM)) * g
    return out, s_fin, u_r[:, -1]


def decoder_layer(x, h0, conv_prefix, s0, shift_prev, segments, lw):
    u = jnp.einsum('bld,de->ble', x, lw['w_in'])
    u_z, u_xbc, u_dt, u_r, u_gate = jnp.split(u, IN_SPLITS, axis=-1)
    y_ssm, h_new, conv_new = mamba_branch(u_z, u_xbc, u_dt, h0, conv_prefix, segments, lw)
    y_rwkv, s_new, shift_new = rwkv_branch(u_r, shift_prev, s0, lw)
    gate_ssm, gate_rwkv = jnp.split(jax.nn.sigmoid(u_gate), 2, axis=-1)
    merged = gate_ssm * (y_ssm @ lw['p_ssm']) + gate_rwkv * (y_rwkv @ lw['p_rwkv'])
    x = layer_norm(ALPHA * x + merged @ lw['w_out'], lw['ln1_g'], lw['ln1_b'])
    hg, hu = jnp.split(x @ lw['w_ffn_in'], 2, axis=-1)
    x = layer_norm(ALPHA * x + (jax.nn.silu(hg) * hu) @ lw['w_ffn_out'], lw['ln2_g'], lw['ln2_b'])
    return x, h_new, conv_new, s_new, shift_new


def run_trunk(x, ssm0, conv0, wkv0, shift0, segments, weights):
    ssm_l, conv_l, wkv_l, shift_l = [], [], [], []
    for l in range(DEPTH):
        lw = {name: arr[l] for name, arr in weights.items()}
        x, h, c, s, sh = decoder_layer(x, ssm0[l], conv0[l], wkv0[l], shift0[l], segments, lw)
        ssm_l.append(h)
        conv_l.append(c)
        wkv_l.append(s)
        shift_l.append(sh)
    return x, jnp.stack(ssm_l), jnp.stack(conv_l), jnp.stack(wkv_l), jnp.stack(shift_l)


def setup_inputs(seed: int = 0) -> dict:
    key = jax.random.key(seed)
    ks = iter(jax.random.split(key, 48))
    nrm = lambda shape, scale: scale * jax.random.normal(next(ks), shape, F32)
    unif = lambda shape, lo, hi: jax.random.uniform(next(ks), shape, F32, lo, hi)
    dt_init = jnp.exp(unif((DEPTH, SSM_HEADS), math.log(1e-3), math.log(1e-1)))
    return {
        'x_prompt': nrm((BATCH, SEQ, D_MODEL), 1.0),
        'x_sample': nrm((DEC_BATCH, DEC_SEQ, D_MODEL), 1.0),
        'state_ssm': nrm((DEPTH, DEC_BATCH, SSM_HEADS, SSM_HEAD_DIM, SSM_STATE), 0.5),
        'state_conv': nrm((DEPTH, DEC_BATCH, CONV_K - 1, CONV_DIM), 1.0),
        'state_wkv': nrm((DEPTH, DEC_BATCH, RWKV_HEADS, RWKV_HEAD_DIM, RWKV_HEAD_DIM), 0.3),
        'state_shift': nrm((DEPTH, DEC_BATCH, RWKV_COLS), 1.0),
        'meta_tokens': nrm((N_META, D_MODEL), 1.0),
        'w_in': nrm((DEPTH, D_MODEL, IN_COLS), D_MODEL ** -0.5),
        'conv_w': nrm((DEPTH, CONV_K, CONV_DIM), CONV_K ** -0.5),
        'conv_b': nrm((DEPTH, CONV_DIM), 0.01),
        'dt_bias': dt_init + jnp.log(-jnp.expm1(-dt_init)),
        'a_log': jnp.log(unif((DEPTH, SSM_HEADS), 1.0, 16.0)),
        'd_skip': 1.0 + nrm((DEPTH, SSM_HEADS), 0.1),
        'ssm_norm_w': 1.0 + nrm((DEPTH, D_INNER), 0.02),
        'p_ssm': nrm((DEPTH, D_INNER, D_MODEL), D_INNER ** -0.5),
        'shift_mu': unif((DEPTH, RWKV_COLS), 0.0, 1.0),
        'w0': unif((DEPTH, RWKV_DIM), -6.0, 0.0),
        'w_lora_up': nrm((DEPTH, DECAY_LORA, RWKV_DIM), 0.5 * DECAY_LORA ** -0.5),
        'a0': nrm((DEPTH, RWKV_DIM), 0.1),
        'a_lora_up': nrm((DEPTH, AAA_LORA, RWKV_DIM), AAA_LORA ** -0.5),
        'g_lora_up': nrm((DEPTH, GATE_LORA, RWKV_DIM), GATE_LORA ** -0.5),
        'k_k': 0.85 + nrm((DEPTH, RWKV_DIM), 0.05),
        'k_a': 1.0 + nrm((DEPTH, RWKV_DIM), 0.05),
        'r_k': nrm((DEPTH, RWKV_HEADS, RWKV_HEAD_DIM), 0.1),
        'lnx_g': 1.0 + nrm((DEPTH, RWKV_DIM), 0.02),
        'lnx_b': nrm((DEPTH, RWKV_DIM), 0.01),
        'p_rwkv': nrm((DEPTH, RWKV_DIM, D_MODEL), RWKV_DIM ** -0.5),
        'w_out': nrm((DEPTH, D_MODEL, D_MODEL), BETA * D_MODEL ** -0.5),
        'ln1_g': 1.0 + nrm((DEPTH, D_MODEL), 0.02),
        'ln1_b': nrm((DEPTH, D_MODEL), 0.01),
        'w_ffn_in': nrm((DEPTH, D_MODEL, 2 * D_FF), D_MODEL ** -0.5),
        'w_ffn_out': nrm((DEPTH, D_FF, D_MODEL), BETA * D_FF ** -0.5),
        'ln2_g': 1.0 + nrm((DEPTH, D_MODEL), 0.02),
        'ln2_b': nrm((DEPTH, D_MODEL), 0.01),
    }


def reference(x_prompt, x_sample, state_ssm, state_conv, state_wkv, state_shift, meta_tokens,
              w_in, conv_w, conv_b, dt_bias, a_log, d_skip, ssm_norm_w, p_ssm, shift_mu,
              w0, w_lora_up, a0, a_lora_up, g_lora_up, k_k, k_a, r_k, lnx_g, lnx_b, p_rwkv,
              w_out, ln1_g, ln1_b, w_ffn_in, w_ffn_out, ln2_g, ln2_b):
    weights = {
        'w_in': w_in, 'conv_w': conv_w, 'conv_b': conv_b, 'dt_bias': dt_bias, 'a_log': a_log,
        'd_skip': d_skip, 'ssm_norm_w': ssm_norm_w, 'p_ssm': p_ssm, 'shift_mu': shift_mu,
        'w0': w0, 'w_lora_up': w_lora_up, 'a0': a0, 'a_lora_up': a_lora_up, 'g_lora_up': g_lora_up,
        'k_k': k_k, 'k_a': k_a, 'r_k': r_k, 'lnx_g': lnx_g, 'lnx_b': lnx_b, 'p_rwkv': p_rwkv,
        'w_out': w_out, 'ln1_g': ln1_g, 'ln1_b': ln1_b, 'w_ffn_in': w_ffn_in,
        'w_ffn_out': w_ffn_out, 'ln2_g': ln2_g, 'ln2_b': ln2_b,
    }
    bsz, seq = x_prompt.shape[:2]
    meta = jnp.broadcast_to(meta_tokens.astype(x_prompt.dtype)[None], (bsz, N_META, D_MODEL))
    xp = jnp.concatenate([meta, x_prompt], axis=1)
    ssm0 = jnp.zeros((DEPTH, bsz, SSM_HEADS, SSM_HEAD_DIM, SSM_STATE), F32)
    conv0 = jnp.zeros((DEPTH, bsz, CONV_K - 1, CONV_DIM), x_prompt.dtype)
    wkv0 = jnp.zeros((DEPTH, bsz, RWKV_HEADS, RWKV_HEAD_DIM, RWKV_HEAD_DIM), F32)
    shift0 = jnp.zeros((DEPTH, bsz, RWKV_COLS), x_prompt.dtype)
    yp, ssm_p, conv_p, wkv_p, shift_p = run_trunk(xp, ssm0, conv0, wkv0, shift0, (N_META, seq), weights)
    y_prompt = yp[:, N_META:]
    y_sample, ssm_s, conv_s, wkv_s, shift_s = run_trunk(
        x_sample, state_ssm, state_conv, state_wkv, state_shift, (x_sample.shape[1],), weights)
    return (y_prompt, y_sample, ssm_p, conv_p, wkv_p, shift_p, ssm_s, conv_s, wkv_s, shift_s)
```

```python
import functools
import math

import jax
import jax.numpy as jnp
from jax import lax
from jax.experimental import pallas as pl
from jax.experimental.pallas import tpu as pltpu

F32 = jnp.float32
BF16 = jnp.bfloat16

LANES = 128
N_META = 16
HEAD_DIM = 64
SSM_STATE = 128
SSM_GROUPS = 4
CONV_K = 4
SSD_CHUNK = 128
LN_EPS = 1e-5
RMS_EPS = 1e-5
GN_EPS = 64e-5
VMEM_LIMIT_BYTES = 56 * 1024 * 1024


def _cparams(*sem):
    return pltpu.CompilerParams(dimension_semantics=sem, vmem_limit_bytes=VMEM_LIMIT_BYTES)


def _pick_tile(n, cap):
    if n <= cap:
        return n
    best = LANES
    for t in range(LANES, cap + 1, LANES):
        if n % t == 0:
            best = t
    return best


def _silu(x):
    return x * jax.nn.sigmoid(x)


def _softplus(x):
    return jnp.maximum(x, 0.0) + jnp.log1p(jnp.exp(-jnp.abs(x)))


def _layer_norm(y, g, b):
    mu = jnp.mean(y, axis=-1, keepdims=True)
    d = y - mu
    var = jnp.mean(d * d, axis=-1, keepdims=True)
    return d * lax.rsqrt(var + LN_EPS) * g + b


def _mm_kernel(x_ref, w_ref, o_ref):
    o_ref[...] = jnp.dot(x_ref[...], w_ref[...], preferred_element_type=F32).astype(o_ref.dtype)


def _matmul(x, w, out_dtype=F32):
    m, k = x.shape
    n = w.shape[1]
    tm = _pick_tile(m, 1024)
    tn = _pick_tile(n, 1024)
    return pl.pallas_call(
        _mm_kernel,
        grid=(n // tn, m // tm),
        in_specs=[pl.BlockSpec((tm, k), lambda j, i: (i, 0)),
                  pl.BlockSpec((k, tn), lambda j, i: (0, j))],
        out_specs=pl.BlockSpec((tm, tn), lambda j, i: (i, j)),
        out_shape=jax.ShapeDtypeStruct((m, n), out_dtype),
        compiler_params=_cparams("parallel", "parallel"),
        name="matmul",
    )(x, w)


def _merge_kernel(alpha, ys_ref, yr_ref, g_ref, ug_ref, x_ref, ps_ref, pr_ref, wo_ref, lg_ref, lb_ref,
                  o_ref, ob_ref):
    d = o_ref.shape[1]
    a = jnp.dot(ys_ref[...], ps_ref[...], preferred_element_type=F32)
    yr = (yr_ref[...] * g_ref[...]).astype(BF16)
    b = jnp.dot(yr, pr_ref[...], preferred_element_type=F32)
    gates = jax.nn.sigmoid(ug_ref[...])
    merged = gates[:, :d] * a + gates[:, d:] * b
    y = alpha * x_ref[...] + jnp.dot(merged.astype(BF16), wo_ref[...], preferred_element_type=F32)
    out = _layer_norm(y, lg_ref[...], lb_ref[...])
    o_ref[...] = out
    ob_ref[...] = out.astype(BF16)


def _merge_out_ln(alpha, y_ssm, y_rwkv, g_rwkv, u_gate, x, p_ssm, p_rwkv, w_out, ln_g, ln_b):
    m, d = x.shape
    tm = _pick_tile(m, 512)
    row = lambda c: pl.BlockSpec((tm, c), lambda i: (i, 0))
    full = lambda a: pl.BlockSpec(a.shape, lambda i: (0, 0))
    return pl.pallas_call(
        functools.partial(_merge_kernel, alpha),
        grid=(m // tm,),
        in_specs=[row(y_ssm.shape[1]), row(d), row(d), row(2 * d), row(d),
                  full(p_ssm), full(p_rwkv), full(w_out), full(ln_g), full(ln_b)],
        out_specs=[row(d), row(d)],
        out_shape=[jax.ShapeDtypeStruct((m, d), F32), jax.ShapeDtypeStruct((m, d), BF16)],
        compiler_params=_cparams("parallel"),
        name="merge_out_ln",
    )(y_ssm, y_rwkv, g_rwkv, u_gate, x, p_ssm, p_rwkv, w_out, ln_g, ln_b)


def _swiglu_kernel(x_ref, wg_ref, wu_ref, o_ref):
    x = x_ref[...]
    hg = jnp.dot(x, wg_ref[...], preferred_element_type=F32)
    hu = jnp.dot(x, wu_ref[...], preferred_element_type=F32)
    o_ref[...] = (_silu(hg) * hu).astype(o_ref.dtype)


def _ffn_in(x, w_ffn_in):
    m, k = x.shape
    dff = w_ffn_in.shape[1] // 2
    tm = _pick_tile(m, 1024)
    tn = _pick_tile(dff, 1408)
    nj = dff // tn
    return pl.pallas_call(
        _swiglu_kernel,
        grid=(nj, m // tm),
        in_specs=[pl.BlockSpec((tm, k), lambda j, i: (i, 0)),
                  pl.BlockSpec((k, tn), lambda j, i: (0, j)),
                  pl.BlockSpec((k, tn), lambda j, i: (0, j + nj))],
        out_specs=pl.BlockSpec((tm, tn), lambda j, i: (i, j)),
        out_shape=jax.ShapeDtypeStruct((m, dff), BF16),
        compiler_params=_cparams("parallel", "parallel"),
        name="ffn_in_swiglu",
    )(x, w_ffn_in, w_ffn_in)


def _ffn_out_kernel(alpha, h_ref, w_ref, x_ref, lg_ref, lb_ref, o_ref, ob_ref):
    y = alpha * x_ref[...] + jnp.dot(h_ref[...], w_ref[...], preferred_element_type=F32)
    out = _layer_norm(y, lg_ref[...], lb_ref[...])
    o_ref[...] = out
    ob_ref[...] = out.astype(BF16)


def _ffn_out_ln(alpha, h, w_ffn_out, x, ln_g, ln_b):
    m, d = x.shape
    k = h.shape[1]
    tm = _pick_tile(m, 512)
    row = lambda c: pl.BlockSpec((tm, c), lambda i: (i, 0))
    full = lambda a: pl.BlockSpec(a.shape, lambda i: (0, 0))
    return pl.pallas_call(
        functools.partial(_ffn_out_kernel, alpha),
        grid=(m // tm,),
        in_specs=[row(k), full(w_ffn_out), row(d), full(ln_g), full(ln_b)],
        out_specs=[row(d), row(d)],
        out_shape=[jax.ShapeDtypeStruct((m, d), F32), jax.ShapeDtypeStruct((m, d), BF16)],
        compiler_params=_cparams("parallel"),
        name="ffn_out_ln",
    )(h, w_ffn_out, x, ln_g, ln_b)


def _ssd_chunk_kernel(l_real, nchunks,
                      uz_ref, uxbc_ref, udt_ref, h0_ref, cpre_ref, convw_ref, convb_ref, dtb_ref, alog_ref,
                      dskip_ref, normw_ref,
                      y_ref, hout_ref, ctail_ref,
                      ht_scr, cbuf_scr, xbc_scr, y_scr):
    q = SSD_CHUNK
    d_inner = uz_ref.shape[1]
    gw = d_inner // SSM_GROUPS
    c = pl.program_id(1)

    @pl.when(c == 0)
    def _():
        for kb in range(d_inner // LANES):
            ht_scr[:, kb * LANES:(kb + 1) * LANES] = h0_ref[0, kb * LANES:(kb + 1) * LANES, :].T
        cbuf_scr[0:8, :] = jnp.zeros((8, cbuf_scr.shape[1]), F32)
        cbuf_scr[8 - (CONV_K - 1):8, :] = cpre_ref[0]

    cbuf_scr[8:8 + q, :] = uxbc_ref[...]
    conv_dim = cbuf_scr.shape[1]
    for s in range(conv_dim // 512):
        cols = slice(s * 512, (s + 1) * 512)
        acc = convb_ref[:, cols] + cbuf_scr[5:5 + q, cols] * convw_ref[0:1, cols]
        for k in range(1, CONV_K):
            acc = acc + cbuf_scr[5 + k:5 + k + q, cols] * convw_ref[k:k + 1, cols]
        xbc_scr[:, cols] = _silu(acc)
    tail = cbuf_scr[l_real + 5:l_real + 8, :]
    cbuf_scr[5:8, :] = tail

    rows = lax.broadcasted_iota(jnp.int32, (q, LANES), 0)
    dt = _softplus(udt_ref[...] + dtb_ref[...])
    if l_real < q:
        dt = jnp.where(rows < l_real, dt, 0.0)
    da = dt * (-jnp.exp(alog_ref[...]))
    acum = da
    s = 1
    while s < q:
        acum = acum + jnp.where(rows >= s, pltpu.roll(acum, s, axis=0), 0.0)
        s *= 2
    acum_t = acum.T
    dt_t = dt.T
    a_last = acum[q - 1:q, :]
    st_t = dt_t * jnp.exp(acum_t[:, q - 1:q] - acum_t)
    ii = lax.broadcasted_iota(jnp.int32, (q, q), 0)
    jj = lax.broadcasted_iota(jnp.int32, (q, q), 1)
    causal = ii >= jj
    low = lax.broadcasted_iota(jnp.int32, (q, LANES), 1) < HEAD_DIM

    for g in range(SSM_GROUPS):
        bm = xbc_scr[:, d_inner + g * SSM_STATE:d_inner + (g + 1) * SSM_STATE]
        cm = xbc_scr[:, d_inner + (SSM_GROUPS + g) * SSM_STATE:d_inner + (SSM_GROUPS + g + 1) * SSM_STATE]
        cmb = cm.astype(BF16)
        cb = lax.dot_general(cmb, bm.astype(BF16), (((1,), (1,)), ((), ())), preferred_element_type=F32)
        bm_t = bm.T
        for pr in range(gw // LANES):
            lanes = slice(g * gw + pr * LANES, g * gw + (pr + 1) * LANES)
            hd0 = (g * gw + pr * LANES) // HEAD_DIM
            x_pair = xbc_scr[:, lanes]
            xb = x_pair.astype(BF16)
            yo = jnp.dot(cmb, ht_scr[:, lanes].astype(BF16), preferred_element_type=F32)
            yd, st, ea, cd = [], [], [], []
            for hd in (hd0, hd0 + 1):
                a_col = jnp.broadcast_to(acum[:, hd:hd + 1], (q, q))
                lmat = jnp.exp(jnp.where(causal, a_col - acum_t[hd:hd + 1, :], -jnp.inf))
                wd = (cb * lmat * dt_t[hd:hd + 1, :]).astype(BF16)
                yd.append(jnp.dot(wd, xb, preferred_element_type=F32))
                ea.append(jnp.exp(a_col))
                st.append(jnp.dot((bm_t * st_t[hd:hd + 1, :]).astype(BF16), xb, preferred_element_type=F32))
                cd.append(jnp.broadcast_to(jnp.exp(a_last[:, hd:hd + 1]), (SSM_STATE, LANES)))
            y_pair = jnp.where(low, yd[0] + yo * ea[0], yd[1] + yo * ea[1]) + x_pair * dskip_ref[:, lanes]
            y_scr[:, lanes] = y_pair
            ht_scr[:, lanes] = ht_scr[:, lanes] * jnp.where(low, cd[0], cd[1]) + jnp.where(low, st[0], st[1])

    for g in range(SSM_GROUPS):
        cols = slice(g * gw, (g + 1) * gw)
        yg = y_scr[:, cols] * _silu(uz_ref[:, cols])
        ms = jnp.mean(yg * yg, axis=-1, keepdims=True)
        y_ref[:, cols] = (yg * lax.rsqrt(ms + RMS_EPS) * normw_ref[:, cols]).astype(y_ref.dtype)

    @pl.when(c == nchunks - 1)
    def _():
        for kb in range(d_inner // LANES):
            hout_ref[0, kb * LANES:(kb + 1) * LANES, :] = ht_scr[:, kb * LANES:(kb + 1) * LANES].T
        ctail_ref[0] = cbuf_scr[5:8, :]


def _ssd_chunked(u_z, u_xbc, u_dt, h0, conv_pre, lw, bsz, l_real):
    m, d_inner = u_z.shape
    conv_dim = u_xbc.shape[1]
    q = SSD_CHUNK
    nchunks = m // bsz // q
    row = lambda cdim: pl.BlockSpec((q, cdim), lambda b, c: (b * nchunks + c, 0))
    full = lambda a: pl.BlockSpec(a.shape, lambda b, c: (0, 0))
    per_b = lambda a: pl.BlockSpec((1,) + a.shape[1:], lambda b, c: (b, 0, 0))
    hshape = jax.ShapeDtypeStruct(h0.shape, F32)
    cshape = jax.ShapeDtypeStruct(conv_pre.shape, F32)
    return pl.pallas_call(
        functools.partial(_ssd_chunk_kernel, min(l_real, q), nchunks),
        grid=(bsz, nchunks),
        in_specs=[row(d_inner), row(conv_dim), row(LANES), per_b(h0), per_b(conv_pre),
                  full(lw['conv_w']), full(lw['conv_b']), full(lw['dt_bias']), full(lw['a_log']),
                  full(lw['d_skip']), full(lw['ssm_norm_w'])],
        out_specs=[row(d_inner), per_b(h0), per_b(conv_pre)],
        out_shape=[jax.ShapeDtypeStruct((m, d_inner), BF16), hshape, cshape],
        scratch_shapes=[pltpu.VMEM((SSM_STATE, d_inner), F32),
                        pltpu.VMEM((q + 8, conv_dim), F32),
                        pltpu.VMEM((q, conv_dim), F32),
                        pltpu.VMEM((q, d_inner), F32)],
        compiler_params=_cparams("parallel", "arbitrary"),
        name="ssd_chunk",
    )(u_z, u_xbc, u_dt, h0, conv_pre, lw['conv_w'], lw['conv_b'], lw['dt_bias'], lw['a_log'],
      lw['d_skip'], lw['ssm_norm_w'])


def _ssd_step_kernel(uz_ref, uxbc_ref, udt_ref, h0_ref, cpre_ref, convw_ref, convb_ref, dtb_ref, alog_ref,
                     dskip_ref, normw_ref, y_ref, hout_ref, ctail_ref):
    d_inner = uz_ref.shape[2]
    gw = d_inner // SSM_GROUPS
    u = uxbc_ref[0]
    pre = cpre_ref[0]
    acc = convb_ref[...] + u * convw_ref[CONV_K - 1:CONV_K, :]
    for k in range(CONV_K - 1):
        acc = acc + pre[k:k + 1, :] * convw_ref[k:k + 1, :]
    xbc = _silu(acc)
    ctail_ref[0, 0:CONV_K - 2, :] = pre[1:, :]
    ctail_ref[0, CONV_K - 2:CONV_K - 1, :] = u
    dt = _softplus(udt_ref[0] + dtb_ref[...])
    dec = jnp.exp(dt * (-jnp.exp(alog_ref[...])))
    nh = d_inner // HEAD_DIM
    hh = lax.broadcasted_iota(jnp.int32, (d_inner, LANES), 0) // HEAD_DIM
    ll = lax.broadcasted_iota(jnp.int32, (d_inner, LANES), 1)
    sel = hh == ll
    dt_col = jnp.sum(jnp.where(sel, dt, 0.0), axis=1, keepdims=True)
    dec_col = jnp.sum(jnp.where(sel, dec, 0.0), axis=1, keepdims=True)
    del nh
    xs_row = xbc[:, :d_inner]
    xs_cols = []
    for kb in range(d_inner // LANES):
        blk = jnp.broadcast_to(xs_row[:, kb * LANES:(kb + 1) * LANES], (LANES, LANES))
        r_i = lax.broadcasted_iota(jnp.int32, (LANES, LANES), 0)
        c_i = lax.broadcasted_iota(jnp.int32, (LANES, LANES), 1)
        xs_cols.append(jnp.sum(jnp.where(r_i == c_i, blk, 0.0), axis=1, keepdims=True))
    x_col = jnp.concatenate(xs_cols, axis=0)
    xdt_col = x_col * dt_col
    y_cols = []
    for g in range(SSM_GROUPS):
        rws = slice(g * gw, (g + 1) * gw)
        bm = xbc[:, d_inner + g * SSM_STATE:d_inner + (g + 1) * SSM_STATE]
        cm = xbc[:, d_inner + (SSM_GROUPS + g) * SSM_STATE:d_inner + (SSM_GROUPS + g + 1) * SSM_STATE]
        hn = h0_ref[0, rws, :] * dec_col[rws, :] + xdt_col[rws, :] * bm
        hout_ref[0, rws, :] = hn
        y_cols.append(jnp.sum(hn * cm, axis=1, keepdims=True))
    y_col = jnp.concatenate(y_cols, axis=0)
    y_parts = []
    for kb in range(d_inner // LANES):
        blk = jnp.broadcast_to(y_col[kb * LANES:(kb + 1) * LANES, :], (LANES, LANES))
        r_i = lax.broadcasted_iota(jnp.int32, (LANES, LANES), 0)
        c_i = lax.broadcasted_iota(jnp.int32, (LANES, LANES), 1)
        y_parts.append(jnp.sum(jnp.where(r_i == c_i, blk, 0.0), axis=0, keepdims=True))
    y_row = jnp.concatenate(y_parts, axis=1) + xs_row * dskip_ref[...]
    y_row = y_row * _silu(uz_ref[0])
    outs = []
    for g in range(SSM_GROUPS):
        yg = y_row[:, g * gw:(g + 1) * gw]
        ms = jnp.mean(yg * yg, axis=-1, keepdims=True)
        outs.append(yg * lax.rsqrt(ms + RMS_EPS))
    y_ref[0] = (jnp.concatenate(outs, axis=1) * normw_ref[...]).astype(y_ref.dtype)


def _ssd_step(u_z, u_xbc, u_dt, h0, conv_pre, lw):
    bsz, d_inner = u_z.shape
    conv_dim = u_xbc.shape[1]
    r3 = lambda a: a.reshape(bsz, 1, a.shape[1])
    per_b = lambda a: pl.BlockSpec((1,) + a.shape[1:], lambda b: (b, 0, 0))
    full = lambda a: pl.BlockSpec(a.shape, lambda b: (0, 0))
    uz3, ux3, ud3 = r3(u_z), r3(u_xbc), r3(u_dt)
    y, h, ct = pl.pallas_call(
        _ssd_step_kernel,
        grid=(bsz,),
        in_specs=[per_b(uz3), per_b(ux3), per_b(ud3), per_b(h0), per_b(conv_pre),
                  full(lw['conv_w']), full(lw['conv_b']), full(lw['dt_bias']), full(lw['a_log']),
                  full(lw['d_skip']), full(lw['ssm_norm_w'])],
        out_specs=[per_b(uz3), per_b(h0), per_b(conv_pre)],
        out_shape=[jax.ShapeDtypeStruct((bsz, 1, d_inner), BF16), jax.ShapeDtypeStruct(h0.shape, F32),
                   jax.ShapeDtypeStruct(conv_pre.shape, F32)],
        compiler_params=_cparams("parallel"),
        name="ssd_step",
    )(uz3, ux3, ud3, h0, conv_pre, lw['conv_w'], lw['conv_b'], lw['dt_bias'], lw['a_log'],
      lw['d_skip'], lw['ssm_norm_w'])
    del conv_dim
    return y.reshape(bsz, d_inner), h, ct


def _rwkv_prep_kernel(seq_mode, l_real, ur_ref, prev_ref, mu_ref, w0_ref, a0_ref, lora_ref, gup_ref,
                      r_ref, w_ref, k_ref, v_ref, a_ref, g_ref, last_ref, carry_scr):
    t, cols = ur_ref.shape
    d = r_ref.shape[1]
    u = ur_ref[...]
    if seq_mode:
        c = pl.program_id(1)

        @pl.when(c == 0)
        def _():
            carry_scr[...] = jnp.broadcast_to(prev_ref[0], carry_scr.shape)

        rows = lax.broadcasted_iota(jnp.int32, (t, cols), 0)
        prev = jnp.where(rows == 0, jnp.broadcast_to(carry_scr[0:1, :], (t, cols)), pltpu.roll(u, 1, axis=0))
        lrow = (l_real - 1) % t
        carry_scr[...] = jnp.broadcast_to(u[lrow:lrow + 1, :], carry_scr.shape)
        last_ref[0] = u[lrow:lrow + 1, :]
    else:
        prev = prev_ref[...]
        last_ref[...] = u
    sh = u + (prev - u) * mu_ref[...]
    r_ref[...] = sh[:, 0:d]
    k_ref[...] = sh[:, d:2 * d]
    v_ref[...] = sh[:, 2 * d:3 * d]
    la = sh[:, 3 * d:3 * d + LANES]
    lane = lax.broadcasted_iota(jnp.int32, la.shape, 1)
    la = jnp.where(lane < LANES // 2, jnp.tanh(la), la).astype(BF16)
    pre = jnp.dot(la, lora_ref[...], preferred_element_type=F32)
    w_log = -_softplus(-(w0_ref[...] + pre[:, :d])) - 0.5
    w_ref[...] = jnp.exp(-jnp.exp(w_log))
    a_ref[...] = jax.nn.sigmoid(a0_ref[...] + pre[:, d:])
    gl = jax.nn.sigmoid(sh[:, 3 * d + LANES:]).astype(BF16)
    g_ref[...] = jnp.dot(gl, gup_ref[...], preferred_element_type=F32)


def _rwkv_prep(u_r, shift_prev, lw, bsz, seq_mode, l_real=None):
    m, cols = u_r.shape
    d = lw['w0'].shape[1]
    full = lambda a: pl.BlockSpec(a.shape, lambda *_: (0,) * a.ndim)
    outs = [jax.ShapeDtypeStruct((m, d), F32)] * 6
    args = (lw['shift_mu'], lw['w0'], lw['a0'], lw['lora_up'], lw['g_lora_up'])
    if seq_mode:
        seq = m // bsz
        t = _pick_tile(seq, 256)
        nblk = seq // t
        l_real = seq if l_real is None else l_real
        row = lambda cdim: pl.BlockSpec((t, cdim), lambda b, c: (b * nblk + c, 0))
        prev3 = shift_prev.reshape(bsz, 1, cols)
        per_b = pl.BlockSpec((1, 1, cols), lambda b, c: (b, 0, 0))
        res = pl.pallas_call(
            functools.partial(_rwkv_prep_kernel, True, l_real),
            grid=(bsz, nblk),
            in_specs=[row(cols), per_b] + [full(a) for a in args],
            out_specs=[row(d)] * 6 + [per_b],
            out_shape=outs + [jax.ShapeDtypeStruct((bsz, 1, cols), F32)],
            scratch_shapes=[pltpu.VMEM((8, cols), F32)],
            compiler_params=_cparams("parallel", "arbitrary"),
            name="rwkv_prep_seq",
        )(u_r, prev3, *args)
        return res[:6], res[6].reshape(bsz, cols)
    t = _pick_tile(m, 256)
    row = lambda cdim: pl.BlockSpec((t, cdim), lambda i: (i, 0))
    res = pl.pallas_call(
        functools.partial(_rwkv_prep_kernel, False, None),
        grid=(m // t,),
        in_specs=[row(cols), row(cols)] + [full(a) for a in args],
        out_specs=[row(d)] * 6 + [row(cols)],
        out_shape=outs + [jax.ShapeDtypeStruct((m, cols), F32)],
        scratch_shapes=[pltpu.VMEM((8, cols), F32)],
        compiler_params=_cparams("parallel"),
        name="rwkv_prep_step",
    )(u_r, shift_prev, *args)
    return res[:6], res[6]


def _rwkv_scan_kernel(tb, r_ref, w_ref, k_ref, v_ref, a_ref, s0_ref, kk_ref, ka_ref, rk_ref, lg_ref, lb_ref,
                      o_ref, sout_ref, s_scr):
    hd = HEAD_DIM
    blk = pl.program_id(1)

    @pl.when(blk == 0)
    def _():
        s_scr[...] = s0_ref[...]

    def step(t, carry):
        r = r_ref[t]
        w = w_ref[t]
        k = k_ref[t]
        v = v_ref[t]
        a = a_ref[t]
        kk = k * kk_ref[...]
        kk = kk * lax.rsqrt(jnp.maximum(jnp.sum(kk * kk, axis=0, keepdims=True), 1e-24))
        b = kk * a
        km = k * (1.0 + (a - 1.0) * ka_ref[...])
        sa = s_scr[0] * kk[0:1, :]
        for j in range(1, hd):
            sa = sa + s_scr[j] * kk[j:j + 1, :]
        sa = -sa
        o = None
        for j in range(hd):
            sj = s_scr[j] * w[j:j + 1, :] + sa * b[j:j + 1, :] + v * km[j:j + 1, :]
            s_scr[j] = sj
            o = sj * r[j:j + 1, :] if o is None else o + sj * r[j:j + 1, :]
        mu = jnp.mean(o, axis=0, keepdims=True)
        dlt = o - mu
        var = jnp.mean(dlt * dlt, axis=0, keepdims=True)
        on = dlt * lax.rsqrt(var + GN_EPS) * lg_ref[...] + lb_ref[...]
        bonus = jnp.sum(r * km * rk_ref[...], axis=0, keepdims=True) * v
        o_ref[t] = on + bonus
        return carry

    lax.fori_loop(0, tb, step, 0)

    @pl.when(blk == pl.num_programs(1) - 1)
    def _():
        sout_ref[...] = s_scr[...]


def _rwkv_scan(r, w, k, v, a, s0, lw_c):
    t_total, hd, c = r.shape
    tb = 8 if t_total % 8 == 0 else t_total
    nb = t_total // tb
    ncl = c // LANES
    seq = pl.BlockSpec((tb, hd, LANES), lambda cl, i: (i, 0, cl))
    st = pl.BlockSpec((hd, hd, LANES), lambda cl, i: (0, 0, cl))
    par = pl.BlockSpec((hd, LANES), lambda cl, i: (0, cl))
    return pl.pallas_call(
        functools.partial(_rwkv_scan_kernel, tb),
        grid=(ncl, nb),
        in_specs=[seq] * 5 + [st] + [par] * 5,
        out_specs=[seq, st],
        out_shape=[jax.ShapeDtypeStruct((t_total, hd, c), F32), jax.ShapeDtypeStruct((hd, hd, c), F32)],
        scratch_shapes=[pltpu.VMEM((hd, hd, LANES), F32)],
        compiler_params=_cparams("parallel", "arbitrary"),
        name="rwkv_scan",
    )(r, w, k, v, a, s0, lw_c['k_k'], lw_c['k_a'], lw_c['r_k'], lw_c['lnx_g'], lw_c['lnx_b'])


def _to_chain(x, bsz, heads):
    t = x.shape[0] // bsz
    return x.reshape(bsz, t, heads, HEAD_DIM).transpose(1, 3, 0, 2).reshape(t, HEAD_DIM, bsz * heads)


def _from_chain(x, bsz, heads):
    t = x.shape[0]
    return x.reshape(t, HEAD_DIM, bsz, heads).transpose(2, 0, 3, 1).reshape(bsz * t, heads * HEAD_DIM)


def _chain_param(p, bsz, heads):
    return jnp.tile(p.reshape(heads, HEAD_DIM).T, (1, bsz))


def _pad_lanes(x, c_pad):
    c = x.shape[-1]
    if c == c_pad:
        return x
    return jnp.pad(x, [(0, 0)] * (x.ndim - 1) + [(0, c_pad - c)])


def _rwkv_core(rwkva, s0, lw, bsz):
    d = rwkva[0].shape[1]
    heads = d // HEAD_DIM
    c = bsz * heads
    c_pad = -(-c // LANES) * LANES
    chain = [_pad_lanes(_to_chain(x, bsz, heads), c_pad) for x in rwkva]
    s0c = _pad_lanes(s0.transpose(3, 2, 0, 1).reshape(HEAD_DIM, HEAD_DIM, c), c_pad)
    lw_c = {n: _pad_lanes(_chain_param(lw[n], bsz, heads), c_pad) for n in ('k_k', 'k_a', 'r_k', 'lnx_g', 'lnx_b')}
    o, s = _rwkv_scan(*chain, s0c, lw_c)
    o = _from_chain(o[..., :c], bsz, heads)
    s = s[..., :c].reshape(HEAD_DIM, HEAD_DIM, bsz, heads).transpose(2, 3, 1, 0)
    return o, s


def _prep_layer_weights(w, l):
    d = w['w_in'].shape[1]
    d_inner = w['p_ssm'].shape[1]
    heads_ssm = w['dt_bias'].shape[1]
    conv_dim = w['conv_w'].shape[2]
    rcols = w['shift_mu'].shape[1]
    dl = w['w_lora_up'].shape[1]
    s0, s1, s2 = d_inner, d_inner + conv_dim, d_inner + conv_dim + heads_ssm
    s3 = s2 + rcols
    w_in = w['w_in'][l]
    pad_h = lambda v, fill: jnp.pad(v, (0, LANES - heads_ssm), constant_values=fill).reshape(1, LANES)
    row = lambda v: v.reshape(1, -1)
    lora = jnp.zeros((LANES, 2 * d), F32)
    lora = lora.at[:dl, :d].set(w['w_lora_up'][l]).at[dl:dl + w['a_lora_up'].shape[1], d:].set(w['a_lora_up'][l])
    return {
        'w_z': w_in[:, :s0].astype(BF16),
        'w_xbc': w_in[:, s0:s1].astype(BF16),
        'w_dt': jnp.pad(w_in[:, s1:s2], ((0, 0), (0, LANES - heads_ssm))).astype(BF16),
        'w_r': w_in[:, s2:s3].astype(BF16),
        'w_gate': w_in[:, s3:].astype(BF16),
        'conv_w': w['conv_w'][l], 'conv_b': row(w['conv_b'][l]),
        'dt_bias': pad_h(w['dt_bias'][l], 0.0), 'a_log': pad_h(w['a_log'][l], 0.0),
        'd_skip': row(jnp.repeat(w['d_skip'][l], HEAD_DIM)),
        'ssm_norm_w': row(w['ssm_norm_w'][l]),
        'p_ssm': w['p_ssm'][l].astype(BF16),
        'shift_mu': row(w['shift_mu'][l]), 'w0': row(w['w0'][l]), 'a0': row(w['a0'][l]),
        'lora_up': lora.astype(BF16), 'g_lora_up': w['g_lora_up'][l].astype(BF16),
        'k_k': w['k_k'][l], 'k_a': w['k_a'][l], 'r_k': w['r_k'][l].reshape(-1),
        'lnx_g': w['lnx_g'][l], 'lnx_b': w['lnx_b'][l],
        'p_rwkv': w['p_rwkv'][l].astype(BF16), 'w_out': w['w_out'][l].astype(BF16),
        'ln1_g': row(w['ln1_g'][l]), 'ln1_b': row(w['ln1_b'][l]),
        'w_ffn_in': w['w_ffn_in'][l].astype(BF16), 'w_ffn_out': w['w_ffn_out'][l].astype(BF16),
        'ln2_g': row(w['ln2_g'][l]), 'ln2_b': row(w['ln2_b'][l]),
    }


def _layer_tail(alpha, x, xb, y_ssm, o_rwkv, g_rwkv, u_gate, lw):
    del xb
    x1, x1b = _merge_out_ln(alpha, y_ssm, o_rwkv, g_rwkv, u_gate, x, lw['p_ssm'], lw['p_rwkv'], lw['w_out'],
                            lw['ln1_g'], lw['ln1_b'])
    hmid = _ffn_in(x1b, lw['w_ffn_in'])
    return _ffn_out_ln(alpha, hmid, lw['w_ffn_out'], x1, lw['ln2_g'], lw['ln2_b'])


def _in_proj(xb, lw):
    return (_matmul(xb, lw['w_z']), _matmul(xb, lw['w_xbc']), _matmul(xb, lw['w_dt']),
            _matmul(xb, lw['w_r']), _matmul(xb, lw['w_gate']))


def _seq_layer(alpha, x, xb, ssm0, conv0, wkv0, shift0, lw, bsz, l_real):
    l_pad = x.shape[0] // bsz
    u_z, u_xbc, u_dt, u_r, u_gate = _in_proj(xb, lw)
    h0 = ssm0.reshape(bsz, -1, SSM_STATE)
    y_ssm, h_new, conv_new = _ssd_chunked(u_z, u_xbc, u_dt, h0, conv0, lw, bsz, l_real)
    (r, w, k, v, a, g), shift_new = _rwkv_prep(u_r, shift0, lw, bsz, True, l_real)
    if l_real < l_pad:
        cut = lambda t: t.reshape(bsz, l_pad, -1)[:, :l_real].reshape(bsz * l_real, -1)
        o, s_new = _rwkv_core([cut(t) for t in (r, w, k, v, a)], wkv0, lw, bsz)
        o = jnp.pad(o.reshape(bsz, l_real, -1), ((0, 0), (0, l_pad - l_real), (0, 0))).reshape(bsz * l_pad, -1)
    else:
        o, s_new = _rwkv_core([r, w, k, v, a], wkv0, lw, bsz)
    x2, x2b = _layer_tail(alpha, x, xb, y_ssm, o, g, u_gate, lw)
    return x2, x2b, h_new.reshape(ssm0.shape), conv_new, s_new, shift_new


def _step_layer(alpha, x, xb, ssm0, conv0, wkv0, shift0, lw):
    bsz = x.shape[0]
    u_z, u_xbc, u_dt, u_r, u_gate = _in_proj(xb, lw)
    h0 = ssm0.reshape(bsz, -1, SSM_STATE)
    y_ssm, h_new, conv_new = _ssd_step(u_z, u_xbc, u_dt, h0, conv0, lw)
    (r, w, k, v, a, g), shift_new = _rwkv_prep(u_r, shift0, lw, bsz, False)
    o, s_new = _rwkv_core([r, w, k, v, a], wkv0, lw, bsz)
    x2, x2b = _layer_tail(alpha, x, xb, y_ssm, o, g, u_gate, lw)
    return x2, x2b, h_new.reshape(ssm0.shape), conv_new, s_new, shift_new


def kernel(x_prompt, x_sample, state_ssm, state_conv, state_wkv, state_shift, meta_tokens, w_in, conv_w, conv_b,
           dt_bias, a_log, d_skip, ssm_norm_w, p_ssm, shift_mu, w0, w_lora_up, a0, a_lora_up, g_lora_up, k_k,
           k_a, r_k, lnx_g, lnx_b, p_rwkv, w_out, ln1_g, ln1_b, w_ffn_in, w_ffn_out, ln2_g, ln2_b):
    weights = {
        'w_in': w_in, 'conv_w': conv_w, 'conv_b': conv_b, 'dt_bias': dt_bias, 'a_log': a_log,
        'd_skip': d_skip, 'ssm_norm_w': ssm_norm_w, 'p_ssm': p_ssm, 'shift_mu': shift_mu,
        'w0': w0, 'w_lora_up': w_lora_up, 'a0': a0, 'a_lora_up': a_lora_up, 'g_lora_up': g_lora_up,
        'k_k': k_k, 'k_a': k_a, 'r_k': r_k, 'lnx_g': lnx_g, 'lnx_b': lnx_b, 'p_rwkv': p_rwkv,
        'w_out': w_out, 'ln1_g': ln1_g, 'ln1_b': ln1_b, 'w_ffn_in': w_ffn_in,
        'w_ffn_out': w_ffn_out, 'ln2_g': ln2_g, 'ln2_b': ln2_b,
    }
    depth = w_in.shape[0]
    alpha = (2 * depth) ** 0.25
    bsz, seq, d = x_prompt.shape
    n_meta = meta_tokens.shape[0]
    layers = [_prep_layer_weights(weights, l) for l in range(depth)]

    xm = jnp.pad(meta_tokens.astype(F32), ((0, SSD_CHUNK - n_meta), (0, 0)))
    xp = x_prompt.reshape(bsz * seq, d)
    xs = x_sample.reshape(x_sample.shape[0], d)
    xmb, xpb, xsb = xm.astype(BF16), xp.astype(BF16), xs.astype(BF16)
    bcast = lambda t: jnp.broadcast_to(t, (bsz,) + t.shape[1:])
    ssm_p, conv_p, wkv_p, shift_p, ssm_s, conv_s, wkv_s, shift_s = [], [], [], [], [], [], [], []
    for l in range(depth):
        lw = layers[l]
        z = lambda a: jnp.zeros((1,) + a.shape[2:], F32)
        xm, xmb, hm, cm, sm, shm = _seq_layer(alpha, xm, xmb, z(state_ssm), z(state_conv), z(state_wkv),
                                              z(state_shift), lw, 1, n_meta)
        xp, xpb, h, c, s, sh = _seq_layer(alpha, xp, xpb, bcast(hm), bcast(cm), bcast(sm), bcast(shm), lw,
                                          bsz, seq)
        ssm_p.append(h), conv_p.append(c), wkv_p.append(s), shift_p.append(sh)
        xs, xsb, h, c, s, sh = _step_layer(alpha, xs, xsb, state_ssm[l], state_conv[l], state_wkv[l],
                                           state_shift[l], lw)
        ssm_s.append(h), conv_s.append(c), wkv_s.append(s), shift_s.append(sh)
    st = jnp.stack
    return (xp.reshape(bsz, seq, d), xs.reshape(x_sample.shape), st(ssm_p), st(conv_p), st(wkv_p), st(shift_p),
            st(ssm_s), st(conv_s), st(wkv_s), st(shift_s))
```

```python
import functools
import math

import jax
import jax.numpy as jnp
from jax import lax
from jax.experimental import pallas as pl
from jax.experimental.pallas import tpu as pltpu

F32 = jnp.float32
BF16 = jnp.bfloat16

LANES = 128
N_META = 16
HEAD_DIM = 64
SSM_STATE = 128
SSM_GROUPS = 4
CONV_K = 4
SSD_CHUNK = 128
LN_EPS = 1e-5
RMS_EPS = 1e-5
GN_EPS = 64e-5
VMEM_LIMIT_BYTES = 56 * 1024 * 1024


def _cparams(*sem):
    return pltpu.CompilerParams(dimension_semantics=sem, vmem_limit_bytes=VMEM_LIMIT_BYTES)


def _pick_tile(n, cap):
    if n <= cap:
        return n
    best = LANES
    for t in range(LANES, cap + 1, LANES):
        if n % t == 0:
            best = t
    return best


def _silu(x):
    return x * jax.nn.sigmoid(x)


def _softplus(x):
    return jnp.maximum(x, 0.0) + jnp.log1p(jnp.exp(-jnp.abs(x)))


def _layer_norm(y, g, b):
    mu = jnp.mean(y, axis=-1, keepdims=True)
    d = y - mu
    var = jnp.mean(d * d, axis=-1, keepdims=True)
    return d * lax.rsqrt(var + LN_EPS) * g + b


def _mm_kernel(x_ref, w_ref, o_ref):
    o_ref[...] = jnp.dot(x_ref[...], w_ref[...], preferred_element_type=F32).astype(o_ref.dtype)


def _matmul(x, w, out_dtype=F32):
    m, k = x.shape
    n = w.shape[1]
    tm = _pick_tile(m, 1024)
    tn = _pick_tile(n, 1024)
    return pl.pallas_call(
        _mm_kernel,
        grid=(n // tn, m // tm),
        in_specs=[pl.BlockSpec((tm, k), lambda j, i: (i, 0)),
                  pl.BlockSpec((k, tn), lambda j, i: (0, j))],
        out_specs=pl.BlockSpec((tm, tn), lambda j, i: (i, j)),
        out_shape=jax.ShapeDtypeStruct((m, n), out_dtype),
        compiler_params=_cparams("parallel", "parallel"),
        name="matmul",
    )(x, w)


def _merge_kernel(alpha, ys_ref, yr_ref, g_ref, ug_ref, x_ref, ps_ref, pr_ref, wo_ref, lg_ref, lb_ref,
                  o_ref, ob_ref):
    d = o_ref.shape[1]
    a = jnp.dot(ys_ref[...], ps_ref[...], preferred_element_type=F32)
    yr = (yr_ref[...] * g_ref[...]).astype(BF16)
    b = jnp.dot(yr, pr_ref[...], preferred_element_type=F32)
    gates = jax.nn.sigmoid(ug_ref[...])
    merged = gates[:, :d] * a + gates[:, d:] * b
    y = alpha * x_ref[...] + jnp.dot(merged.astype(BF16), wo_ref[...], preferred_element_type=F32)
    out = _layer_norm(y, lg_ref[...], lb_ref[...])
    o_ref[...] = out
    ob_ref[...] = out.astype(BF16)


def _merge_out_ln(alpha, y_ssm, y_rwkv, g_rwkv, u_gate, x, p_ssm, p_rwkv, w_out, ln_g, ln_b):
    m, d = x.shape
    tm = _pick_tile(m, 512)
    row = lambda c: pl.BlockSpec((tm, c), lambda i: (i, 0))
    full = lambda a: pl.BlockSpec(a.shape, lambda i: (0, 0))
    return pl.pallas_call(
        functools.partial(_merge_kernel, alpha),
        grid=(m // tm,),
        in_specs=[row(y_ssm.shape[1]), row(d), row(d), row(2 * d), row(d),
                  full(p_ssm), full(p_rwkv), full(w_out), full(ln_g), full(ln_b)],
        out_specs=[row(d), row(d)],
        out_shape=[jax.ShapeDtypeStruct((m, d), F32), jax.ShapeDtypeStruct((m, d), BF16)],
        compiler_params=_cparams("parallel"),
        name="merge_out_ln",
    )(y_ssm, y_rwkv, g_rwkv, u_gate, x, p_ssm, p_rwkv, w_out, ln_g, ln_b)


def _swiglu_kernel(x_ref, wg_ref, wu_ref, o_ref):
    x = x_ref[...]
    hg = jnp.dot(x, wg_ref[...], preferred_element_type=F32)
    hu = jnp.dot(x, wu_ref[...], preferred_element_type=F32)
    o_ref[...] = (_silu(hg) * hu).astype(o_ref.dtype)


def _ffn_in(x, w_ffn_in):
    m, k = x.shape
    dff = w_ffn_in.shape[1] // 2
    tm = _pick_tile(m, 1024)
    tn = _pick_tile(dff, 1408)
    nj = dff // tn
    return pl.pallas_call(
        _swiglu_kernel,
        grid=(nj, m // tm),
        in_specs=[pl.BlockSpec((tm, k), lambda j, i: (i, 0)),
                  pl.BlockSpec((k, tn), lambda j, i: (0, j)),
                  pl.BlockSpec((k, tn), lambda j, i: (0, j + nj))],
        out_specs=pl.BlockSpec((tm, tn), lambda j, i: (i, j)),
        out_shape=jax.ShapeDtypeStruct((m, dff), BF16),
        compiler_params=_cparams("parallel", "parallel"),
        name="ffn_in_swiglu",
    )(x, w_ffn_in, w_ffn_in)


def _ffn_out_kernel(alpha, h_ref, w_ref, x_ref, lg_ref, lb_ref, o_ref, ob_ref):
    y = alpha * x_ref[...] + jnp.dot(h_ref[...], w_ref[...], preferred_element_type=F32)
    out = _layer_norm(y, lg_ref[...], lb_ref[...])
    o_ref[...] = out
    ob_ref[...] = out.astype(BF16)


def _ffn_out_ln(alpha, h, w_ffn_out, x, ln_g, ln_b):
    m, d = x.shape
    k = h.shape[1]
    tm = _pick_tile(m, 512)
    row = lambda c: pl.BlockSpec((tm, c), lambda i: (i, 0))
    full = lambda a: pl.BlockSpec(a.shape, lambda i: (0, 0))
    return pl.pallas_call(
        functools.partial(_ffn_out_kernel, alpha),
        grid=(m // tm,),
        in_specs=[row(k), full(w_ffn_out), row(d), full(ln_g), full(ln_b)],
        out_specs=[row(d), row(d)],
        out_shape=[jax.ShapeDtypeStruct((m, d), F32), jax.ShapeDtypeStruct((m, d), BF16)],
        compiler_params=_cparams("parallel"),
        name="ffn_out_ln",
    )(h, w_ffn_out, x, ln_g, ln_b)


def _ssd_chunk_kernel(l_real, nchunks,
                      uz_ref, uxbc_ref, udt_ref, h0_ref, cpre_ref, convw_ref, convb_ref, dtb_ref, alog_ref,
                      dskip_ref, normw_ref,
                      y_ref, hout_ref, ctail_ref,
                      ht_scr, cbuf_scr, xbc_scr, y_scr):
    q = SSD_CHUNK
    d_inner = uz_ref.shape[1]
    gw = d_inner // SSM_GROUPS
    c = pl.program_id(1)

    @pl.when(c == 0)
    def _():
        for kb in range(d_inner // LANES):
            ht_scr[:, kb * LANES:(kb + 1) * LANES] = h0_ref[0, kb * LANES:(kb + 1) * LANES, :].T
        cbuf_scr[0:8, :] = jnp.zeros((8, cbuf_scr.shape[1]), F32)
        cbuf_scr[8 - (CONV_K - 1):8, :] = cpre_ref[0]

    cbuf_scr[8:8 + q, :] = uxbc_ref[...]
    conv_dim = cbuf_scr.shape[1]
    for s in range(conv_dim // 512):
        cols = slice(s * 512, (s + 1) * 512)
        acc = convb_ref[:, cols] + cbuf_scr[5:5 + q, cols] * convw_ref[0:1, cols]
        for k in range(1, CONV_K):
            acc = acc + cbuf_scr[5 + k:5 + k + q, cols] * convw_ref[k:k + 1, cols]
        xbc_scr[:, cols] = _silu(acc)
    tail = cbuf_scr[l_real + 5:l_real + 8, :]
    cbuf_scr[5:8, :] = tail

    rows = lax.broadcasted_iota(jnp.int32, (q, LANES), 0)
    dt = _softplus(udt_ref[...] + dtb_ref[...])
    if l_real < q:
        dt = jnp.where(rows < l_real, dt, 0.0)
    da = dt * (-jnp.exp(alog_ref[...]))
    acum = da
    s = 1
    while s < q:
        acum = acum + jnp.where(rows >= s, pltpu.roll(acum, s, axis=0), 0.0)
        s *= 2
    acum_t = acum.T
    dt_t = dt.T
    a_last = acum[q - 1:q, :]
    st_t = dt_t * jnp.exp(acum_t[:, q - 1:q] - acum_t)
    ii = lax.broadcasted_iota(jnp.int32, (q, q), 0)
    jj = lax.broadcasted_iota(jnp.int32, (q, q), 1)
    causal = ii >= jj
    low = lax.broadcasted_iota(jnp.int32, (q, LANES), 1) < HEAD_DIM

    for g in range(SSM_GROUPS):
        bm = xbc_scr[:, d_inner + g * SSM_STATE:d_inner + (g + 1) * SSM_STATE]
        cm = xbc_scr[:, d_inner + (SSM_GROUPS + g) * SSM_STATE:d_inner + (SSM_GROUPS + g + 1) * SSM_STATE]
        cmb = cm.astype(BF16)
        cb = lax.dot_general(cmb, bm.astype(BF16), (((1,), (1,)), ((), ())), preferred_element_type=F32)
        bm_t = bm.T
        for pr in range(gw // LANES):
            lanes = slice(g * gw + pr * LANES, g * gw + (pr + 1) * LANES)
            hd0 = (g * gw + pr * LANES) // HEAD_DIM
            x_pair = xbc_scr[:, lanes]
            xb = x_pair.astype(BF16)
            yo = jnp.dot(cmb, ht_scr[:, lanes].astype(BF16), preferred_element_type=F32)
            yd, st, ea, cd = [], [], [], []
            for hd in (hd0, hd0 + 1):
                a_col = jnp.broadcast_to(acum[:, hd:hd + 1], (q, q))
                lmat = jnp.exp(jnp.where(causal, a_col - acum_t[hd:hd + 1, :], -jnp.inf))
                wd = (cb * lmat * dt_t[hd:hd + 1, :]).astype(BF16)
                yd.append(jnp.dot(wd, xb, preferred_element_type=F32))
                ea.append(jnp.exp(a_col))
                st.append(jnp.dot((bm_t * st_t[hd:hd + 1, :]).astype(BF16), xb, preferred_element_type=F32))
                cd.append(jnp.broadcast_to(jnp.exp(a_last[:, hd:hd + 1]), (SSM_STATE, LANES)))
            y_pair = jnp.where(low, yd[0] + yo * ea[0], yd[1] + yo * ea[1]) + x_pair * dskip_ref[:, lanes]
            y_scr[:, lanes] = y_pair
            ht_scr[:, lanes] = ht_scr[:, lanes] * jnp.where(low, cd[0], cd[1]) + jnp.where(low, st[0], st[1])

    for g in range(SSM_GROUPS):
        cols = slice(g * gw, (g + 1) * gw)
        yg = y_scr[:, cols] * _silu(uz_ref[:, cols])
        ms = jnp.mean(yg * yg, axis=-1, keepdims=True)
        y_ref[:, cols] = (yg * lax.rsqrt(ms + RMS_EPS) * normw_ref[:, cols]).astype(y_ref.dtype)

    @pl.when(c == nchunks - 1)
    def _():
        for kb in range(d_inner // LANES):
            hout_ref[0, kb * LANES:(kb + 1) * LANES, :] = ht_scr[:, kb * LANES:(kb + 1) * LANES].T
        ctail_ref[0] = cbuf_scr[5:8, :]


def _ssd_chunked(u_z, u_xbc, u_dt, h0, conv_pre, lw, bsz, l_real):
    m, d_inner = u_z.shape
    conv_dim = u_xbc.shape[1]
    q = SSD_CHUNK
    nchunks = m // bsz // q
    row = lambda cdim: pl.BlockSpec((q, cdim), lambda b, c: (b * nchunks + c, 0))
    full = lambda a: pl.BlockSpec(a.shape, lambda b, c: (0, 0))
    per_b = lambda a: pl.BlockSpec((1,) + a.shape[1:], lambda b, c: (b, 0, 0))
    hshape = jax.ShapeDtypeStruct(h0.shape, F32)
    cshape = jax.ShapeDtypeStruct(conv_pre.shape, F32)
    return pl.pallas_call(
        functools.partial(_ssd_chunk_kernel, min(l_real, q), nchunks),
        grid=(bsz, nchunks),
        in_specs=[row(d_inner), row(conv_dim), row(LANES), per_b(h0), per_b(conv_pre),
                  full(lw['conv_w']), full(lw['conv_b']), full(lw['dt_bias']), full(lw['a_log']),
                  full(lw['d_skip']), full(lw['ssm_norm_w'])],
        out_specs=[row(d_inner), per_b(h0), per_b(conv_pre)],
        out_shape=[jax.ShapeDtypeStruct((m, d_inner), BF16), hshape, cshape],
        scratch_shapes=[pltpu.VMEM((SSM_STATE, d_inner), F32),
                        pltpu.VMEM((q + 8, conv_dim), F32),
                        pltpu.VMEM((q, conv_dim), F32),
                        pltpu.VMEM((q, d_inner), F32)],
        compiler_params=_cparams("parallel", "arbitrary"),
        name="ssd_chunk",
    )(u_z, u_xbc, u_dt, h0, conv_pre, lw['conv_w'], lw['conv_b'], lw['dt_bias'], lw['a_log'],
      lw['d_skip'], lw['ssm_norm_w'])


def _row_to_col(row):
    r_i = lax.broadcasted_iota(jnp.int32, (LANES, LANES), 0)
    c_i = lax.broadcasted_iota(jnp.int32, (LANES, LANES), 1)
    return jnp.sum(jnp.where(r_i == c_i, jnp.broadcast_to(row, (LANES, LANES)), 0.0), axis=1, keepdims=True)


def _ssd_step_kernel(aliased, uz_ref, uxbc_ref, udtx_ref, h0_ref, cpre_ref, convw_ref, convb_ref, dtbx_ref,
                     alogx_ref, dskip_ref, normw_ref, *rest):
    y_ref, hout_ref, ctail_ref, ht_scr = rest[1:] if aliased else rest
    d_inner = uz_ref.shape[2]
    gw = d_inner // SSM_GROUPS
    u = uxbc_ref[0]
    pre = cpre_ref[0, 0]
    acc = convb_ref[...] + u * convw_ref[CONV_K - 1:CONV_K, :]
    for k in range(CONV_K - 1):
        acc = acc + pre[k:k + 1, :] * convw_ref[k:k + 1, :]
    xbc = _silu(acc)
    ctail_ref[0, 0:CONV_K - 2, :] = pre[1:, :]
    ctail_ref[0, CONV_K - 2:CONV_K - 1, :] = u
    dt = _softplus(udtx_ref[0] + dtbx_ref[...])
    dec = jnp.exp(dt * (-jnp.exp(alogx_ref[...])))
    x_row = xbc[:, :d_inner]
    xdt = x_row * dt
    for kb in range(d_inner // LANES):
        ht_scr[:, kb * LANES:(kb + 1) * LANES] = h0_ref[0, 0, kb * LANES:(kb + 1) * LANES, :].T
    y_parts = []
    for g in range(SSM_GROUPS):
        cols = slice(g * gw, (g + 1) * gw)
        bm = xbc[:, d_inner + g * SSM_STATE:d_inner + (g + 1) * SSM_STATE]
        cm = xbc[:, d_inner + (SSM_GROUPS + g) * SSM_STATE:d_inner + (SSM_GROUPS + g + 1) * SSM_STATE]
        hn = ht_scr[:, cols] * dec[:, cols] + _row_to_col(bm) * xdt[:, cols]
        ht_scr[:, cols] = hn
        y_parts.append(jnp.sum(hn * _row_to_col(cm), axis=0, keepdims=True))
    y_row = jnp.concatenate(y_parts, axis=1) + x_row * dskip_ref[...]
    y_row = y_row * _silu(uz_ref[0])
    outs = []
    for g in range(SSM_GROUPS):
        yg = y_row[:, g * gw:(g + 1) * gw]
        ms = jnp.mean(yg * yg, axis=-1, keepdims=True)
        outs.append(yg * lax.rsqrt(ms + RMS_EPS))
    y_ref[0] = (jnp.concatenate(outs, axis=1) * normw_ref[...]).astype(y_ref.dtype)
    for kb in range(d_inner // LANES):
        hout_ref[0, 0, kb * LANES:(kb + 1) * LANES, :] = ht_scr[:, kb * LANES:(kb + 1) * LANES].T


def _ssd_step(u_z, u_xbc, u_dtx, ssm_all, conv_all, layer, lw, h_acc):
    bsz, d_inner = u_z.shape
    r3 = lambda a: a.reshape(bsz, 1, a.shape[1])
    per_b = lambda a: pl.BlockSpec((1,) + a.shape[1:], lambda b: (b, 0, 0))
    per_lb = lambda a: pl.BlockSpec((1, 1) + a.shape[2:], lambda b: (layer, b, 0, 0))
    full = lambda a: pl.BlockSpec(a.shape, lambda b: (0, 0))
    uz3, ux3, ud3 = r3(u_z), r3(u_xbc), r3(u_dtx)
    cshape = conv_all.shape[1:]
    args = [uz3, ux3, ud3, ssm_all, conv_all, lw['conv_w'], lw['conv_b'], lw['dt_bias_x'], lw['a_log_x'],
            lw['d_skip'], lw['ssm_norm_w']]
    in_specs = [per_b(uz3), per_b(ux3), per_b(ud3), per_lb(ssm_all), per_lb(conv_all)]
    in_specs += [full(a) for a in args[5:]]
    aliases = {}
    if h_acc is not None:
        args.append(h_acc)
        in_specs.append(pl.BlockSpec(memory_space=pl.ANY))
        aliases = {len(args) - 1: 1}
    y, h, ct = pl.pallas_call(
        functools.partial(_ssd_step_kernel, h_acc is not None),
        grid=(bsz,),
        in_specs=in_specs,
        out_specs=[per_b(uz3), per_lb(ssm_all), pl.BlockSpec((1,) + cshape[1:], lambda b: (b, 0, 0))],
        out_shape=[jax.ShapeDtypeStruct((bsz, 1, d_inner), BF16), jax.ShapeDtypeStruct(ssm_all.shape, F32),
                   jax.ShapeDtypeStruct(cshape, F32)],
        scratch_shapes=[pltpu.VMEM((SSM_STATE, d_inner), F32)],
        input_output_aliases=aliases,
        compiler_params=_cparams("arbitrary"),
        name="ssd_step",
    )(*args)
    return y.reshape(bsz, d_inner), h, ct


def _rwkv_prep_kernel(seq_mode, l_real, ur_ref, prev_ref, mu_ref, w0_ref, a0_ref, lora_ref, gup_ref,
                      r_ref, w_ref, k_ref, v_ref, a_ref, g_ref, last_ref, carry_scr):
    t, cols = ur_ref.shape
    d = r_ref.shape[1]
    u = ur_ref[...]
    if seq_mode:
        c = pl.program_id(1)

        @pl.when(c == 0)
        def _():
            carry_scr[...] = jnp.broadcast_to(prev_ref[0], carry_scr.shape)

        rows = lax.broadcasted_iota(jnp.int32, (t, cols), 0)
        prev = jnp.where(rows == 0, jnp.broadcast_to(carry_scr[0:1, :], (t, cols)), pltpu.roll(u, 1, axis=0))
        lrow = (l_real - 1) % t
        carry_scr[...] = jnp.broadcast_to(u[lrow:lrow + 1, :], carry_scr.shape)
        last_ref[0] = u[lrow:lrow + 1, :]
    else:
        prev = prev_ref[...]
        last_ref[...] = u
    sh = u + (prev - u) * mu_ref[...]
    r_ref[...] = sh[:, 0:d]
    k_ref[...] = sh[:, d:2 * d]
    v_ref[...] = sh[:, 2 * d:3 * d]
    la = sh[:, 3 * d:3 * d + LANES]
    lane = lax.broadcasted_iota(jnp.int32, la.shape, 1)
    la = jnp.where(lane < LANES // 2, jnp.tanh(la), la).astype(BF16)
    pre = jnp.dot(la, lora_ref[...], preferred_element_type=F32)
    w_log = -_softplus(-(w0_ref[...] + pre[:, :d])) - 0.5
    w_ref[...] = jnp.exp(-jnp.exp(w_log))
    a_ref[...] = jax.nn.sigmoid(a0_ref[...] + pre[:, d:])
    gl = jax.nn.sigmoid(sh[:, 3 * d + LANES:]).astype(BF16)
    g_ref[...] = jnp.dot(gl, gup_ref[...], preferred_element_type=F32)


def _rwkv_prep(u_r, shift_prev, lw, bsz, seq_mode, l_real=None):
    m, cols = u_r.shape
    d = lw['w0'].shape[1]
    full = lambda a: pl.BlockSpec(a.shape, lambda *_: (0,) * a.ndim)
    outs = [jax.ShapeDtypeStruct((m, d), F32)] * 6
    args = (lw['shift_mu'], lw['w0'], lw['a0'], lw['lora_up'], lw['g_lora_up'])
    if seq_mode:
        seq = m // bsz
        t = _pick_tile(seq, 256)
        nblk = seq // t
        l_real = seq if l_real is None else l_real
        row = lambda cdim: pl.BlockSpec((t, cdim), lambda b, c: (b * nblk + c, 0))
        prev3 = shift_prev.reshape(bsz, 1, cols)
        per_b = pl.BlockSpec((1, 1, cols), lambda b, c: (b, 0, 0))
        res = pl.pallas_call(
            functools.partial(_rwkv_prep_kernel, True, l_real),
            grid=(bsz, nblk),
            in_specs=[row(cols), per_b] + [full(a) for a in args],
            out_specs=[row(d)] * 6 + [per_b],
            out_shape=outs + [jax.ShapeDtypeStruct((bsz, 1, cols), F32)],
            scratch_shapes=[pltpu.VMEM((8, cols), F32)],
            compiler_params=_cparams("parallel", "arbitrary"),
            name="rwkv_prep_seq",
        )(u_r, prev3, *args)
        return res[:6], res[6].reshape(bsz, cols)
    t = _pick_tile(m, 256)
    row = lambda cdim: pl.BlockSpec((t, cdim), lambda i: (i, 0))
    res = pl.pallas_call(
        functools.partial(_rwkv_prep_kernel, False, None),
        grid=(m // t,),
        in_specs=[row(cols), row(cols)] + [full(a) for a in args],
        out_specs=[row(d)] * 6 + [row(cols)],
        out_shape=outs + [jax.ShapeDtypeStruct((m, cols), F32)],
        scratch_shapes=[pltpu.VMEM((8, cols), F32)],
        compiler_params=_cparams("parallel"),
        name="rwkv_prep_step",
    )(u_r, shift_prev, *args)
    return res[:6], res[6]


def _rwkv_scan_kernel(tb, r_ref, w_ref, k_ref, v_ref, a_ref, s0_ref, kk_ref, ka_ref, rk_ref, lg_ref, lb_ref,
                      o_ref, sout_ref, s_scr, nkk_scr, b_scr, km_scr):
    hd = HEAD_DIM
    blk = pl.program_id(1)

    @pl.when(blk == 0)
    def _():
        s_scr[...] = s0_ref[...]

    def step(t, carry):
        k = k_ref[t]
        a = a_ref[t]
        kk = k * kk_ref[...]
        kk = kk * lax.rsqrt(jnp.maximum(jnp.sum(kk * kk, axis=0, keepdims=True), 1e-24))
        nkk_scr[...] = -kk
        b_scr[...] = kk * a
        km = k * (1.0 + (a - 1.0) * ka_ref[...])
        km_scr[...] = km
        bonus = jnp.sum(r_ref[t] * km * rk_ref[...], axis=0, keepdims=True)
        v = v_ref[t]
        sa = s_scr[0] * nkk_scr[0:1, :]
        for j in range(1, hd):
            sa = sa + s_scr[j] * nkk_scr[j:j + 1, :]
        o = None
        for j in range(hd):
            sj = s_scr[j] * w_ref[t, j:j + 1, :] + sa * b_scr[j:j + 1, :] + v * km_scr[j:j + 1, :]
            s_scr[j] = sj
            term = sj * r_ref[t, j:j + 1, :]
            o = term if o is None else o + term
        mu = jnp.mean(o, axis=0, keepdims=True)
        dlt = o - mu
        var = jnp.mean(dlt * dlt, axis=0, keepdims=True)
        on = dlt * lax.rsqrt(var + GN_EPS) * lg_ref[...] + lb_ref[...]
        o_ref[t] = on + bonus * v
        return carry

    lax.fori_loop(0, tb, step, 0)

    @pl.when(blk == pl.num_programs(1) - 1)
    def _():
        sout_ref[...] = s_scr[...]


def _rwkv_scan(r, w, k, v, a, s0, lw_c):
    t_total, hd, c = r.shape
    tb = 16 if t_total % 16 == 0 else t_total
    nb = t_total // tb
    ncl = c // LANES
    seq = pl.BlockSpec((tb, hd, LANES), lambda cl, i: (i, 0, cl))
    st = pl.BlockSpec((hd, hd, LANES), lambda cl, i: (0, 0, cl))
    par = pl.BlockSpec((hd, LANES), lambda cl, i: (0, cl))
    return pl.pallas_call(
        functools.partial(_rwkv_scan_kernel, tb),
        grid=(ncl, nb),
        in_specs=[seq] * 5 + [st] + [par] * 5,
        out_specs=[seq, st],
        out_shape=[jax.ShapeDtypeStruct((t_total, hd, c), F32), jax.ShapeDtypeStruct((hd, hd, c), F32)],
        scratch_shapes=[pltpu.VMEM((hd, hd, LANES), F32)] + [pltpu.VMEM((hd, LANES), F32)] * 3,
        compiler_params=_cparams("parallel", "arbitrary"),
        name="rwkv_scan",
    )(r, w, k, v, a, s0, lw_c['k_k'], lw_c['k_a'], lw_c['r_k'], lw_c['lnx_g'], lw_c['lnx_b'])


def _to_chain(x, bsz, heads):
    t = x.shape[0] // bsz
    return x.reshape(bsz, t, heads, HEAD_DIM).transpose(1, 3, 0, 2).reshape(t, HEAD_DIM, bsz * heads)


def _from_chain(x, bsz, heads):
    t = x.shape[0]
    return x.reshape(t, HEAD_DIM, bsz, heads).transpose(2, 0, 3, 1).reshape(bsz * t, heads * HEAD_DIM)


def _chain_param(p, bsz, heads):
    return jnp.tile(p.reshape(heads, HEAD_DIM).T, (1, bsz))


def _pad_lanes(x, c_pad):
    c = x.shape[-1]
    if c == c_pad:
        return x
    return jnp.pad(x, [(0, 0)] * (x.ndim - 1) + [(0, c_pad - c)])


def _rwkv_core(rwkva, s0, lw, bsz):
    d = rwkva[0].shape[1]
    heads = d // HEAD_DIM
    c = bsz * heads
    c_pad = -(-c // LANES) * LANES
    chain = [_pad_lanes(_to_chain(x, bsz, heads), c_pad) for x in rwkva]
    s0c = _pad_lanes(s0.transpose(3, 2, 0, 1).reshape(HEAD_DIM, HEAD_DIM, c), c_pad)
    lw_c = {n: _pad_lanes(_chain_param(lw[n], bsz, heads), c_pad) for n in ('k_k', 'k_a', 'r_k', 'lnx_g', 'lnx_b')}
    o, s = _rwkv_scan(*chain, s0c, lw_c)
    o = _from_chain(o[..., :c], bsz, heads)
    s = s[..., :c].reshape(HEAD_DIM, HEAD_DIM, bsz, heads).transpose(2, 3, 1, 0)
    return o, s


def _prep_layer_weights(w, l):
    d = w['w_in'].shape[1]
    d_inner = w['p_ssm'].shape[1]
    heads_ssm = w['dt_bias'].shape[1]
    conv_dim = w['conv_w'].shape[2]
    rcols = w['shift_mu'].shape[1]
    dl = w['w_lora_up'].shape[1]
    s0, s1, s2 = d_inner, d_inner + conv_dim, d_inner + conv_dim + heads_ssm
    s3 = s2 + rcols
    w_in = w['w_in'][l]
    pad_h = lambda v, fill: jnp.pad(v, (0, LANES - heads_ssm), constant_values=fill).reshape(1, LANES)
    row = lambda v: v.reshape(1, -1)
    lora = jnp.zeros((LANES, 2 * d), F32)
    lora = lora.at[:dl, :d].set(w['w_lora_up'][l]).at[dl:dl + w['a_lora_up'].shape[1], d:].set(w['a_lora_up'][l])
    return {
        'w_z': w_in[:, :s0].astype(BF16),
        'w_xbc': w_in[:, s0:s1].astype(BF16),
        'w_dt': jnp.pad(w_in[:, s1:s2], ((0, 0), (0, LANES - heads_ssm))).astype(BF16),
        'w_dt_x': jnp.repeat(w_in[:, s1:s2], HEAD_DIM, axis=1).astype(BF16),
        'dt_bias_x': row(jnp.repeat(w['dt_bias'][l], HEAD_DIM)),
        'a_log_x': row(jnp.repeat(w['a_log'][l], HEAD_DIM)),
        'w_r': w_in[:, s2:s3].astype(BF16),
        'w_gate': w_in[:, s3:].astype(BF16),
        'conv_w': w['conv_w'][l], 'conv_b': row(w['conv_b'][l]),
        'dt_bias': pad_h(w['dt_bias'][l], 0.0), 'a_log': pad_h(w['a_log'][l], 0.0),
        'd_skip': row(jnp.repeat(w['d_skip'][l], HEAD_DIM)),
        'ssm_norm_w': row(w['ssm_norm_w'][l]),
        'p_ssm': w['p_ssm'][l].astype(BF16),
        'shift_mu': row(w['shift_mu'][l]), 'w0': row(w['w0'][l]), 'a0': row(w['a0'][l]),
        'lora_up': lora.astype(BF16), 'g_lora_up': w['g_lora_up'][l].astype(BF16),
        'k_k': w['k_k'][l], 'k_a': w['k_a'][l], 'r_k': w['r_k'][l].reshape(-1),
        'lnx_g': w['lnx_g'][l], 'lnx_b': w['lnx_b'][l],
        'p_rwkv': w['p_rwkv'][l].astype(BF16), 'w_out': w['w_out'][l].astype(BF16),
        'ln1_g': row(w['ln1_g'][l]), 'ln1_b': row(w['ln1_b'][l]),
        'w_ffn_in': w['w_ffn_in'][l].astype(BF16), 'w_ffn_out': w['w_ffn_out'][l].astype(BF16),
        'ln2_g': row(w['ln2_g'][l]), 'ln2_b': row(w['ln2_b'][l]),
    }


def _layer_tail(alpha, x, xb, y_ssm, o_rwkv, g_rwkv, u_gate, lw):
    del xb
    x1, x1b = _merge_out_ln(alpha, y_ssm, o_rwkv, g_rwkv, u_gate, x, lw['p_ssm'], lw['p_rwkv'], lw['w_out'],
                            lw['ln1_g'], lw['ln1_b'])
    hmid = _ffn_in(x1b, lw['w_ffn_in'])
    return _ffn_out_ln(alpha, hmid, lw['w_ffn_out'], x1, lw['ln2_g'], lw['ln2_b'])


def _in_proj(xb, lw, dt_key='w_dt'):
    return (_matmul(xb, lw['w_z']), _matmul(xb, lw['w_xbc']), _matmul(xb, lw[dt_key]),
            _matmul(xb, lw['w_r']), _matmul(xb, lw['w_gate']))


def _seq_layer(alpha, x, xb, ssm0, conv0, wkv0, shift0, lw, bsz, l_real):
    l_pad = x.shape[0] // bsz
    u_z, u_xbc, u_dt, u_r, u_gate = _in_proj(xb, lw)
    h0 = ssm0.reshape(bsz, -1, SSM_STATE)
    y_ssm, h_new, conv_new = _ssd_chunked(u_z, u_xbc, u_dt, h0, conv0, lw, bsz, l_real)
    (r, w, k, v, a, g), shift_new = _rwkv_prep(u_r, shift0, lw, bsz, True, l_real)
    if l_real < l_pad:
        cut = lambda t: t.reshape(bsz, l_pad, -1)[:, :l_real].reshape(bsz * l_real, -1)
        o, s_new = _rwkv_core([cut(t) for t in (r, w, k, v, a)], wkv0, lw, bsz)
        o = jnp.pad(o.reshape(bsz, l_real, -1), ((0, 0), (0, l_pad - l_real), (0, 0))).reshape(bsz * l_pad, -1)
    else:
        o, s_new = _rwkv_core([r, w, k, v, a], wkv0, lw, bsz)
    x2, x2b = _layer_tail(alpha, x, xb, y_ssm, o, g, u_gate, lw)
    return x2, x2b, h_new.reshape(ssm0.shape), conv_new, s_new, shift_new


def _step_layer(alpha, x, xb, ssm_all, conv_all, layer, h_acc, wkv0, shift0, lw):
    bsz = x.shape[0]
    u_z, u_xbc, u_dtx, u_r, u_gate = _in_proj(xb, lw, 'w_dt_x')
    y_ssm, h_acc, conv_new = _ssd_step(u_z, u_xbc, u_dtx, ssm_all, conv_all, layer, lw, h_acc)
    (r, w, k, v, a, g), shift_new = _rwkv_prep(u_r, shift0, lw, bsz, False)
    o, s_new = _rwkv_core([r, w, k, v, a], wkv0, lw, bsz)
    x2, x2b = _layer_tail(alpha, x, xb, y_ssm, o, g, u_gate, lw)
    return x2, x2b, h_acc, conv_new, s_new, shift_new


def kernel(x_prompt, x_sample, state_ssm, state_conv, state_wkv, state_shift, meta_tokens, w_in, conv_w, conv_b,
           dt_bias, a_log, d_skip, ssm_norm_w, p_ssm, shift_mu, w0, w_lora_up, a0, a_lora_up, g_lora_up, k_k,
           k_a, r_k, lnx_g, lnx_b, p_rwkv, w_out, ln1_g, ln1_b, w_ffn_in, w_ffn_out, ln2_g, ln2_b):
    weights = {
        'w_in': w_in, 'conv_w': conv_w, 'conv_b': conv_b, 'dt_bias': dt_bias, 'a_log': a_log,
        'd_skip': d_skip, 'ssm_norm_w': ssm_norm_w, 'p_ssm': p_ssm, 'shift_mu': shift_mu,
        'w0': w0, 'w_lora_up': w_lora_up, 'a0': a0, 'a_lora_up': a_lora_up, 'g_lora_up': g_lora_up,
        'k_k': k_k, 'k_a': k_a, 'r_k': r_k, 'lnx_g': lnx_g, 'lnx_b': lnx_b, 'p_rwkv': p_rwkv,
        'w_out': w_out, 'ln1_g': ln1_g, 'ln1_b': ln1_b, 'w_ffn_in': w_ffn_in,
        'w_ffn_out': w_ffn_out, 'ln2_g': ln2_g, 'ln2_b': ln2_b,
    }
    depth = w_in.shape[0]
    alpha = (2 * depth) ** 0.25
    bsz, seq, d = x_prompt.shape
    n_meta = meta_tokens.shape[0]
    layers = [_prep_layer_weights(weights, l) for l in range(depth)]

    xm = jnp.pad(meta_tokens.astype(F32), ((0, SSD_CHUNK - n_meta), (0, 0)))
    xp = x_prompt.reshape(bsz * seq, d)
    xs = x_sample.reshape(x_sample.shape[0], d)
    xmb, xpb, xsb = xm.astype(BF16), xp.astype(BF16), xs.astype(BF16)
    bcast = lambda t: jnp.broadcast_to(t, (bsz,) + t.shape[1:])
    ssm_p, conv_p, wkv_p, shift_p, conv_s, wkv_s, shift_s = [], [], [], [], [], [], []
    ssm_all = state_ssm.reshape(state_ssm.shape[:2] + (-1, SSM_STATE))
    ssm_s = None
    for l in range(depth):
        lw = layers[l]
        z = lambda a: jnp.zeros((1,) + a.shape[2:], F32)
        xm, xmb, hm, cm, sm, shm = _seq_layer(alpha, xm, xmb, z(state_ssm), z(state_conv), z(state_wkv),
                                              z(state_shift), lw, 1, n_meta)
        xp, xpb, h, c, s, sh = _seq_layer(alpha, xp, xpb, bcast(hm), bcast(cm), bcast(sm), bcast(shm), lw,
                                          bsz, seq)
        ssm_p.append(h), conv_p.append(c), wkv_p.append(s), shift_p.append(sh)
        xs, xsb, ssm_s, c, s, sh = _step_layer(alpha, xs, xsb, ssm_all, state_conv, l, ssm_s, state_wkv[l],
                                               state_shift[l], lw)
        conv_s.append(c), wkv_s.append(s), shift_s.append(sh)
    st = jnp.stack
    return (xp.reshape(bsz, seq, d), xs.reshape(x_sample.shape), st(ssm_p), st(conv_p), st(wkv_p), st(shift_p),
            ssm_s.reshape(state_ssm.shape), st(conv_s), st(wkv_s), st(shift_s))
```

```python
import functools
import math

import jax
import jax.numpy as jnp
from jax import lax
from jax.experimental import pallas as pl
from jax.experimental.pallas import tpu as pltpu

F32 = jnp.float32
BF16 = jnp.bfloat16

LANES = 128
N_META = 16
HEAD_DIM = 64
SSM_STATE = 128
SSM_GROUPS = 4
CONV_K = 4
SSD_CHUNK = 128
CONV_PITCH = 3
LN_EPS = 1e-5
RMS_EPS = 1e-5
GN_EPS = 64e-5
VMEM_LIMIT_BYTES = 56 * 1024 * 1024


def _cparams(*sem):
    return pltpu.CompilerParams(dimension_semantics=sem, vmem_limit_bytes=VMEM_LIMIT_BYTES)


def _pick_tile(n, cap):
    if n <= cap:
        return n
    best = LANES
    for t in range(LANES, cap + 1, LANES):
        if n % t == 0:
            best = t
    return best


def _silu(x):
    return x * jax.nn.sigmoid(x)


def _softplus(x):
    return jnp.maximum(x, 0.0) + jnp.log(1.0 + jnp.exp(-jnp.abs(x)))


def _layer_norm(y, g, b):
    mu = jnp.mean(y, axis=-1, keepdims=True)
    d = y - mu
    var = jnp.mean(d * d, axis=-1, keepdims=True)
    return d * lax.rsqrt(var + LN_EPS) * g + b


def _mm_kernel(x_ref, w_ref, o_ref):
    o_ref[...] = jnp.dot(x_ref[...], w_ref[...], preferred_element_type=F32).astype(o_ref.dtype)


def _matmul(x, w, out_dtype=F32):
    m, k = x.shape
    n = w.shape[1]
    tm = _pick_tile(m, 1024)
    tn = _pick_tile(n, 1792)
    return pl.pallas_call(
        _mm_kernel,
        grid=(n // tn, m // tm),
        in_specs=[pl.BlockSpec((tm, k), lambda j, i: (i, 0)),
                  pl.BlockSpec((k, tn), lambda j, i: (0, j))],
        out_specs=pl.BlockSpec((tm, tn), lambda j, i: (i, j)),
        out_shape=jax.ShapeDtypeStruct((m, n), out_dtype),
        compiler_params=_cparams("parallel", "parallel"),
        name="matmul",
    )(x, w)


def _merge_kernel(alpha, ys_ref, yr_ref, g_ref, ug_ref, x_ref, ps_ref, pr_ref, wo_ref, lg_ref, lb_ref,
                  o_ref, ob_ref):
    d = o_ref.shape[1]
    a = jnp.dot(ys_ref[...], ps_ref[...], preferred_element_type=F32)
    yr = (yr_ref[...] * g_ref[...]).astype(BF16)
    b = jnp.dot(yr, pr_ref[...], preferred_element_type=F32)
    gates = jax.nn.sigmoid(ug_ref[...])
    merged = gates[:, :d] * a + gates[:, d:] * b
    y = alpha * x_ref[...] + jnp.dot(merged.astype(BF16), wo_ref[...], preferred_element_type=F32)
    out = _layer_norm(y, lg_ref[...], lb_ref[...])
    o_ref[...] = out
    ob_ref[...] = out.astype(BF16)


def _merge_out_ln(alpha, y_ssm, y_rwkv, g_rwkv, u_gate, x, p_ssm, p_rwkv, w_out, ln_g, ln_b):
    m, d = x.shape
    tm = _pick_tile(m, 512)
    row = lambda c: pl.BlockSpec((tm, c), lambda i: (i, 0))
    full = lambda a: pl.BlockSpec(a.shape, lambda i: (0, 0))
    return pl.pallas_call(
        functools.partial(_merge_kernel, alpha),
        grid=(m // tm,),
        in_specs=[row(y_ssm.shape[1]), row(d), row(d), row(2 * d), row(d),
                  full(p_ssm), full(p_rwkv), full(w_out), full(ln_g), full(ln_b)],
        out_specs=[row(d), row(d)],
        out_shape=[jax.ShapeDtypeStruct((m, d), F32), jax.ShapeDtypeStruct((m, d), BF16)],
        compiler_params=_cparams("parallel"),
        name="merge_out_ln",
    )(y_ssm, y_rwkv, g_rwkv, u_gate, x, p_ssm, p_rwkv, w_out, ln_g, ln_b)


def _swiglu_kernel(x_ref, wg_ref, wu_ref, o_ref):
    x = x_ref[...]
    hg = jnp.dot(x, wg_ref[...], preferred_element_type=F32)
    hu = jnp.dot(x, wu_ref[...], preferred_element_type=F32)
    o_ref[...] = (_silu(hg) * hu).astype(o_ref.dtype)


def _ffn_in(x, w_ffn_in):
    m, k = x.shape
    dff = w_ffn_in.shape[1] // 2
    tm = _pick_tile(m, 1024)
    tn = _pick_tile(dff, 1408)
    nj = dff // tn
    return pl.pallas_call(
        _swiglu_kernel,
        grid=(nj, m // tm),
        in_specs=[pl.BlockSpec((tm, k), lambda j, i: (i, 0)),
                  pl.BlockSpec((k, tn), lambda j, i: (0, j)),
                  pl.BlockSpec((k, tn), lambda j, i: (0, j + nj))],
        out_specs=pl.BlockSpec((tm, tn), lambda j, i: (i, j)),
        out_shape=jax.ShapeDtypeStruct((m, dff), BF16),
        compiler_params=_cparams("parallel", "parallel"),
        name="ffn_in_swiglu",
    )(x, w_ffn_in, w_ffn_in)


def _ffn_out_kernel(alpha, h_ref, w_ref, x_ref, lg_ref, lb_ref, o_ref, ob_ref):
    y = alpha * x_ref[...] + jnp.dot(h_ref[...], w_ref[...], preferred_element_type=F32)
    out = _layer_norm(y, lg_ref[...], lb_ref[...])
    o_ref[...] = out
    ob_ref[...] = out.astype(BF16)


def _ffn_out_ln(alpha, h, w_ffn_out, x, ln_g, ln_b):
    m, d = x.shape
    k = h.shape[1]
    tm = _pick_tile(m, 512)
    row = lambda c: pl.BlockSpec((tm, c), lambda i: (i, 0))
    full = lambda a: pl.BlockSpec(a.shape, lambda i: (0, 0))
    return pl.pallas_call(
        functools.partial(_ffn_out_kernel, alpha),
        grid=(m // tm,),
        in_specs=[row(k), full(w_ffn_out), row(d), full(ln_g), full(ln_b)],
        out_specs=[row(d), row(d)],
        out_shape=[jax.ShapeDtypeStruct((m, d), F32), jax.ShapeDtypeStruct((m, d), BF16)],
        compiler_params=_cparams("parallel"),
        name="ffn_out_ln",
    )(h, w_ffn_out, x, ln_g, ln_b)


def _ssd_chunk_kernel(l_real, nchunks,
                      uz_ref, uxbc_ref, udt_ref, h0_ref, cpre_ref, convw_ref, convb_ref, dtb_ref, alog_ref,
                      dskip_ref, normw_ref,
                      y_ref, hout_ref, ctail_ref,
                      ht_scr, cbuf_scr, xbc_scr, y_scr):
    q = SSD_CHUNK
    d_inner = uz_ref.shape[1]
    gw = d_inner // SSM_GROUPS
    c = pl.program_id(1)

    @pl.when(c == 0)
    def _():
        for kb in range(d_inner // LANES):
            ht_scr[:, kb * LANES:(kb + 1) * LANES] = h0_ref[0, kb * LANES:(kb + 1) * LANES, :].T
        for ct in range(cbuf_scr.shape[0]):
            cbuf_scr[ct, pl.ds(5 * CONV_PITCH, CONV_K - 1, stride=CONV_PITCH), :] = \
                cpre_ref[0, :, ct * LANES:(ct + 1) * LANES]

    rows_at = lambda r0, n: pl.ds(r0 * CONV_PITCH, n, stride=CONV_PITCH)
    for ct in range(cbuf_scr.shape[0]):
        cols = slice(ct * LANES, (ct + 1) * LANES)
        cbuf_scr[ct, rows_at(8, q), :] = uxbc_ref[:, cols]
        acc = convb_ref[:, cols] + cbuf_scr[ct, rows_at(5, q), :] * convw_ref[0:1, cols]
        for k in range(1, CONV_K):
            acc = acc + cbuf_scr[ct, rows_at(5 + k, q), :] * convw_ref[k:k + 1, cols]
        xbc_scr[:, cols] = _silu(acc)
        tail = cbuf_scr[ct, rows_at(l_real + 5, CONV_K - 1), :]
        cbuf_scr[ct, rows_at(5, CONV_K - 1), :] = tail

    rows = lax.broadcasted_iota(jnp.int32, (q, LANES), 0)
    dt = _softplus(udt_ref[...] + dtb_ref[...])
    if l_real < q:
        dt = jnp.where(rows < l_real, dt, 0.0)
    da = dt * (-jnp.exp(alog_ref[...]))
    acum = da
    s = 1
    while s < q:
        acum = acum + jnp.where(rows >= s, pltpu.roll(acum, s, axis=0), 0.0)
        s *= 2
    acum_t = acum.T
    dt_t = dt.T
    a_last = acum[q - 1:q, :]
    st_t = dt_t * jnp.exp(acum_t[:, q - 1:q] - acum_t)
    ii = lax.broadcasted_iota(jnp.int32, (q, q), 0)
    jj = lax.broadcasted_iota(jnp.int32, (q, q), 1)
    causal = ii >= jj
    low = lax.broadcasted_iota(jnp.int32, (q, LANES), 1) < HEAD_DIM

    for g in range(SSM_GROUPS):
        bm = xbc_scr[:, d_inner + g * SSM_STATE:d_inner + (g + 1) * SSM_STATE]
        cm = xbc_scr[:, d_inner + (SSM_GROUPS + g) * SSM_STATE:d_inner + (SSM_GROUPS + g + 1) * SSM_STATE]
        cmb = cm.astype(BF16)
        cb = lax.dot_general(cmb, bm.astype(BF16), (((1,), (1,)), ((), ())), preferred_element_type=F32)
        bm_t = bm.T
        for pr in range(gw // LANES):
            lanes = slice(g * gw + pr * LANES, g * gw + (pr + 1) * LANES)
            hd0 = (g * gw + pr * LANES) // HEAD_DIM
            x_pair = xbc_scr[:, lanes]
            xb = x_pair.astype(BF16)
            yo = jnp.dot(cmb, ht_scr[:, lanes].astype(BF16), preferred_element_type=F32)
            yd, st, ea, cd = [], [], [], []
            for hd in (hd0, hd0 + 1):
                a_col = jnp.broadcast_to(acum[:, hd:hd + 1], (q, q))
                lmat = jnp.exp(jnp.where(causal, a_col - acum_t[hd:hd + 1, :], -jnp.inf))
                wd = (cb * lmat * dt_t[hd:hd + 1, :]).astype(BF16)
                yd.append(jnp.dot(wd, xb, preferred_element_type=F32))
                ea.append(jnp.exp(a_col))
                st.append(jnp.dot((bm_t * st_t[hd:hd + 1, :]).astype(BF16), xb, preferred_element_type=F32))
                cd.append(jnp.broadcast_to(jnp.exp(a_last[:, hd:hd + 1]), (SSM_STATE, LANES)))
            y_pair = jnp.where(low, yd[0] + yo * ea[0], yd[1] + yo * ea[1]) + x_pair * dskip_ref[:, lanes]
            y_scr[:, lanes] = y_pair
            ht_scr[:, lanes] = ht_scr[:, lanes] * jnp.where(low, cd[0], cd[1]) + jnp.where(low, st[0], st[1])

    for g in range(SSM_GROUPS):
        cols = slice(g * gw, (g + 1) * gw)
        yg = y_scr[:, cols] * _silu(uz_ref[:, cols])
        ms = jnp.mean(yg * yg, axis=-1, keepdims=True)
        y_ref[:, cols] = (yg * lax.rsqrt(ms + RMS_EPS) * normw_ref[:, cols]).astype(y_ref.dtype)

    @pl.when(c == nchunks - 1)
    def _():
        for kb in range(d_inner // LANES):
            hout_ref[0, kb * LANES:(kb + 1) * LANES, :] = ht_scr[:, kb * LANES:(kb + 1) * LANES].T
        for ct in range(cbuf_scr.shape[0]):
            ctail_ref[0, :, ct * LANES:(ct + 1) * LANES] = \
                cbuf_scr[ct, pl.ds(5 * CONV_PITCH, CONV_K - 1, stride=CONV_PITCH), :]


def _ssd_chunked(u_z, u_xbc, u_dt, h0, conv_pre, lw, bsz, l_real):
    m, d_inner = u_z.shape
    conv_dim = u_xbc.shape[1]
    q = SSD_CHUNK
    nchunks = m // bsz // q
    row = lambda cdim: pl.BlockSpec((q, cdim), lambda b, c: (b * nchunks + c, 0))
    full = lambda a: pl.BlockSpec(a.shape, lambda b, c: (0, 0))
    per_b = lambda a: pl.BlockSpec((1,) + a.shape[1:], lambda b, c: (b, 0, 0))
    hshape = jax.ShapeDtypeStruct(h0.shape, F32)
    cshape = jax.ShapeDtypeStruct(conv_pre.shape, F32)
    return pl.pallas_call(
        functools.partial(_ssd_chunk_kernel, min(l_real, q), nchunks),
        grid=(bsz, nchunks),
        in_specs=[row(d_inner), row(conv_dim), row(LANES), per_b(h0), per_b(conv_pre),
                  full(lw['conv_w']), full(lw['conv_b']), full(lw['dt_bias']), full(lw['a_log']),
                  full(lw['d_skip']), full(lw['ssm_norm_w'])],
        out_specs=[row(d_inner), per_b(h0), per_b(conv_pre)],
        out_shape=[jax.ShapeDtypeStruct((m, d_inner), BF16), hshape, cshape],
        scratch_shapes=[pltpu.VMEM((SSM_STATE, d_inner), F32),
                        pltpu.VMEM((conv_dim // LANES, (q + 8) * CONV_PITCH, LANES), F32),
                        pltpu.VMEM((q, conv_dim), F32),
                        pltpu.VMEM((q, d_inner), F32)],
        compiler_params=_cparams("parallel", "arbitrary"),
        name="ssd_chunk",
    )(u_z, u_xbc, u_dt, h0, conv_pre, lw['conv_w'], lw['conv_b'], lw['dt_bias'], lw['a_log'],
      lw['d_skip'], lw['ssm_norm_w'])


def _row_to_col(row):
    r_i = lax.broadcasted_iota(jnp.int32, (LANES, LANES), 0)
    c_i = lax.broadcasted_iota(jnp.int32, (LANES, LANES), 1)
    return jnp.sum(jnp.where(r_i == c_i, jnp.broadcast_to(row, (LANES, LANES)), 0.0), axis=1, keepdims=True)


def _ssd_step_kernel(aliased, uz_ref, uxbc_ref, udtx_ref, h0_ref, cpre_ref, convw_ref, convb_ref, dtbx_ref,
                     alogx_ref, dskip_ref, normw_ref, *rest):
    y_ref, hout_ref, ctail_ref, ht_scr = rest[1:] if aliased else rest
    d_inner = uz_ref.shape[2]
    gw = d_inner // SSM_GROUPS
    u = uxbc_ref[0]
    pre = cpre_ref[0, 0]
    acc = convb_ref[...] + u * convw_ref[CONV_K - 1:CONV_K, :]
    for k in range(CONV_K - 1):
        acc = acc + pre[k:k + 1, :] * convw_ref[k:k + 1, :]
    xbc = _silu(acc)
    ctail_ref[0, 0:CONV_K - 2, :] = pre[1:, :]
    ctail_ref[0, CONV_K - 2:CONV_K - 1, :] = u
    dt = _softplus(udtx_ref[0] + dtbx_ref[...])
    dec = jnp.exp(dt * (-jnp.exp(alogx_ref[...])))
    x_row = xbc[:, :d_inner]
    xdt = x_row * dt
    for kb in range(d_inner // LANES):
        ht_scr[:, kb * LANES:(kb + 1) * LANES] = h0_ref[0, 0, kb * LANES:(kb + 1) * LANES, :].T
    y_parts = []
    for g in range(SSM_GROUPS):
        cols = slice(g * gw, (g + 1) * gw)
        bm = xbc[:, d_inner + g * SSM_STATE:d_inner + (g + 1) * SSM_STATE]
        cm = xbc[:, d_inner + (SSM_GROUPS + g) * SSM_STATE:d_inner + (SSM_GROUPS + g + 1) * SSM_STATE]
        hn = ht_scr[:, cols] * dec[:, cols] + _row_to_col(bm) * xdt[:, cols]
        ht_scr[:, cols] = hn
        y_parts.append(jnp.sum(hn * _row_to_col(cm), axis=0, keepdims=True))
    y_row = jnp.concatenate(y_parts, axis=1) + x_row * dskip_ref[...]
    y_row = y_row * _silu(uz_ref[0])
    outs = []
    for g in range(SSM_GROUPS):
        yg = y_row[:, g * gw:(g + 1) * gw]
        ms = jnp.mean(yg * yg, axis=-1, keepdims=True)
        outs.append(yg * lax.rsqrt(ms + RMS_EPS))
    y_ref[0] = (jnp.concatenate(outs, axis=1) * normw_ref[...]).astype(y_ref.dtype)
    for kb in range(d_inner // LANES):
        hout_ref[0, 0, kb * LANES:(kb + 1) * LANES, :] = ht_scr[:, kb * LANES:(kb + 1) * LANES].T


def _ssd_step(u_z, u_xbc, u_dtx, ssm_all, conv_all, layer, lw, h_acc):
    bsz, d_inner = u_z.shape
    r3 = lambda a: a.reshape(bsz, 1, a.shape[1])
    per_b = lambda a: pl.BlockSpec((1,) + a.shape[1:], lambda b: (b, 0, 0))
    per_lb = lambda a: pl.BlockSpec((1, 1) + a.shape[2:], lambda b: (layer, b, 0, 0))
    full = lambda a: pl.BlockSpec(a.shape, lambda b: (0, 0))
    uz3, ux3, ud3 = r3(u_z), r3(u_xbc), r3(u_dtx)
    cshape = conv_all.shape[1:]
    args = [uz3, ux3, ud3, ssm_all, conv_all, lw['conv_w'], lw['conv_b'], lw['dt_bias_x'], lw['a_log_x'],
            lw['d_skip'], lw['ssm_norm_w']]
    in_specs = [per_b(uz3), per_b(ux3), per_b(ud3), per_lb(ssm_all), per_lb(conv_all)]
    in_specs += [full(a) for a in args[5:]]
    aliases = {}
    if h_acc is not None:
        args.append(h_acc)
        in_specs.append(pl.BlockSpec(memory_space=pl.ANY))
        aliases = {len(args) - 1: 1}
    y, h, ct = pl.pallas_call(
        functools.partial(_ssd_step_kernel, h_acc is not None),
        grid=(bsz,),
        in_specs=in_specs,
        out_specs=[per_b(uz3), per_lb(ssm_all), pl.BlockSpec((1,) + cshape[1:], lambda b: (b, 0, 0))],
        out_shape=[jax.ShapeDtypeStruct((bsz, 1, d_inner), BF16), jax.ShapeDtypeStruct(ssm_all.shape, F32),
                   jax.ShapeDtypeStruct(cshape, F32)],
        scratch_shapes=[pltpu.VMEM((SSM_STATE, d_inner), F32)],
        input_output_aliases=aliases,
        compiler_params=_cparams("arbitrary"),
        name="ssd_step",
    )(*args)
    return y.reshape(bsz, d_inner), h, ct


def _rwkv_prep_kernel(seq_mode, l_real, ur_ref, prev_ref, mu_ref, w0_ref, a0_ref, lora_ref, gup_ref,
                      r_ref, w_ref, k_ref, v_ref, a_ref, g_ref, last_ref, carry_scr):
    t, cols = ur_ref.shape
    d = r_ref.shape[1]
    u = ur_ref[...]
    if seq_mode:
        c = pl.program_id(1)

        @pl.when(c == 0)
        def _():
            carry_scr[...] = jnp.broadcast_to(prev_ref[0], carry_scr.shape)

        rows = lax.broadcasted_iota(jnp.int32, (t, cols), 0)
        prev = jnp.where(rows == 0, jnp.broadcast_to(carry_scr[0:1, :], (t, cols)), pltpu.roll(u, 1, axis=0))
        lrow = (l_real - 1) % t
        carry_scr[...] = jnp.broadcast_to(u[lrow:lrow + 1, :], carry_scr.shape)
        last_ref[0] = u[lrow:lrow + 1, :]
    else:
        prev = prev_ref[...]
        last_ref[...] = u
    sh = u + (prev - u) * mu_ref[...]
    r_ref[...] = sh[:, 0:d]
    k_ref[...] = sh[:, d:2 * d]
    v_ref[...] = sh[:, 2 * d:3 * d]
    la = sh[:, 3 * d:3 * d + LANES]
    lane = lax.broadcasted_iota(jnp.int32, la.shape, 1)
    la = jnp.where(lane < LANES // 2, jnp.tanh(la), la).astype(BF16)
    pre = jnp.dot(la, lora_ref[...], preferred_element_type=F32)
    w_log = -_softplus(-(w0_ref[...] + pre[:, :d])) - 0.5
    w_ref[...] = jnp.exp(-jnp.exp(w_log))
    a_ref[...] = jax.nn.sigmoid(a0_ref[...] + pre[:, d:])
    gl = jax.nn.sigmoid(sh[:, 3 * d + LANES:]).astype(BF16)
    g_ref[...] = jnp.dot(gl, gup_ref[...], preferred_element_type=F32)


def _rwkv_prep(u_r, shift_prev, lw, bsz, seq_mode, l_real=None):
    m, cols = u_r.shape
    d = lw['w0'].shape[1]
    full = lambda a: pl.BlockSpec(a.shape, lambda *_: (0,) * a.ndim)
    outs = [jax.ShapeDtypeStruct((m, d), F32)] * 6
    args = (lw['shift_mu'], lw['w0'], lw['a0'], lw['lora_up'], lw['g_lora_up'])
    if seq_mode:
        seq = m // bsz
        t = _pick_tile(seq, 256)
        nblk = seq // t
        l_real = seq if l_real is None else l_real
        row = lambda cdim: pl.BlockSpec((t, cdim), lambda b, c: (b * nblk + c, 0))
        prev3 = shift_prev.reshape(bsz, 1, cols)
        per_b = pl.BlockSpec((1, 1, cols), lambda b, c: (b, 0, 0))
        res = pl.pallas_call(
            functools.partial(_rwkv_prep_kernel, True, l_real),
            grid=(bsz, nblk),
            in_specs=[row(cols), per_b] + [full(a) for a in args],
            out_specs=[row(d)] * 6 + [per_b],
            out_shape=outs + [jax.ShapeDtypeStruct((bsz, 1, cols), F32)],
            scratch_shapes=[pltpu.VMEM((8, cols), F32)],
            compiler_params=_cparams("parallel", "arbitrary"),
            name="rwkv_prep_seq",
        )(u_r, prev3, *args)
        return res[:6], res[6].reshape(bsz, cols)
    t = _pick_tile(m, 256)
    row = lambda cdim: pl.BlockSpec((t, cdim), lambda i: (i, 0))
    res = pl.pallas_call(
        functools.partial(_rwkv_prep_kernel, False, None),
        grid=(m // t,),
        in_specs=[row(cols), row(cols)] + [full(a) for a in args],
        out_specs=[row(d)] * 6 + [row(cols)],
        out_shape=outs + [jax.ShapeDtypeStruct((m, cols), F32)],
        scratch_shapes=[pltpu.VMEM((8, cols), F32)],
        compiler_params=_cparams("parallel"),
        name="rwkv_prep_step",
    )(u_r, shift_prev, *args)
    return res[:6], res[6]


def _rwkv_scan_kernel(tb, r_ref, w_ref, k_ref, v_ref, a_ref, s0_ref, kk_ref, ka_ref, rk_ref, lg_ref, lb_ref,
                      o_ref, sout_ref, s_scr, nkk_scr, b_scr, km_scr):
    hd = HEAD_DIM
    blk = pl.program_id(1)

    @pl.when(blk == 0)
    def _():
        s_scr[...] = s0_ref[...]

    pad_rows = o_ref.shape[1] - hd

    def step(t, carry):
        k = k_ref[t, 0:hd, :]
        a = a_ref[t, 0:hd, :]
        kk = k * kk_ref[...]
        kk = kk * lax.rsqrt(jnp.maximum(jnp.sum(kk * kk, axis=0, keepdims=True), 1e-24))
        nkk_scr[...] = -kk
        b_scr[...] = kk * a
        km = k * (1.0 + (a - 1.0) * ka_ref[...])
        km_scr[...] = km
        bonus = jnp.sum(r_ref[t, 0:hd, :] * km * rk_ref[...], axis=0, keepdims=True)
        v = v_ref[t, 0:hd, :]
        sa = s_scr[0] * nkk_scr[0:1, :]
        for j in range(1, hd):
            sa = sa + s_scr[j] * nkk_scr[j:j + 1, :]
        o = None
        for j in range(hd):
            sj = s_scr[j] * w_ref[t, j:j + 1, :] + sa * b_scr[j:j + 1, :] + v * km_scr[j:j + 1, :]
            s_scr[j] = sj
            term = sj * r_ref[t, j:j + 1, :]
            o = term if o is None else o + term
        mu = jnp.mean(o, axis=0, keepdims=True)
        dlt = o - mu
        var = jnp.mean(dlt * dlt, axis=0, keepdims=True)
        on = dlt * lax.rsqrt(var + GN_EPS) * lg_ref[...] + lb_ref[...]
        o_ref[t, 0:hd, :] = on + bonus * v
        if pad_rows:
            o_ref[t, hd:hd + pad_rows, :] = jnp.zeros((pad_rows, LANES), F32)
        return carry

    lax.fori_loop(0, tb, step, 0)

    @pl.when(blk == pl.num_programs(1) - 1)
    def _():
        sout_ref[...] = s_scr[...]


def _rwkv_scan(r, w, k, v, a, s0, lw_c):
    t_total, rows, c = r.shape
    hd = HEAD_DIM
    tb = 32 if t_total % 32 == 0 else t_total
    nb = t_total // tb
    ncl = c // LANES
    seq = pl.BlockSpec((tb, rows, LANES), lambda cl, i: (i, 0, cl))
    st = pl.BlockSpec((hd, hd, LANES), lambda cl, i: (0, 0, cl))
    par = pl.BlockSpec((hd, LANES), lambda cl, i: (0, cl))
    return pl.pallas_call(
        functools.partial(_rwkv_scan_kernel, tb),
        grid=(ncl, nb),
        in_specs=[seq] * 5 + [st] + [par] * 5,
        out_specs=[seq, st],
        out_shape=[jax.ShapeDtypeStruct((t_total, rows, c), F32), jax.ShapeDtypeStruct((hd, hd, c), F32)],
        scratch_shapes=[pltpu.VMEM((hd, hd, LANES), F32)] + [pltpu.VMEM((hd, LANES), F32)] * 3,
        compiler_params=_cparams("parallel", "arbitrary"),
        name="rwkv_scan",
    )(r, w, k, v, a, s0, lw_c['k_k'], lw_c['k_a'], lw_c['r_k'], lw_c['lnx_g'], lw_c['lnx_b'])


def _to_chain(x, bsz, heads):
    t = x.shape[0] // bsz
    return x.reshape(bsz, t, heads, HEAD_DIM).transpose(1, 3, 0, 2).reshape(t, HEAD_DIM, bsz * heads)


def _from_chain(x, bsz, heads):
    t = x.shape[0]
    return x.reshape(t, HEAD_DIM, bsz, heads).transpose(2, 0, 3, 1).reshape(bsz * t, heads * HEAD_DIM)


CHAIN_PITCH = HEAD_DIM + 8


def _to_chain_kernel(x_ref, o_ref, y_scr):
    nb = x_ref.shape[0]
    nk = x_ref.shape[2] // LANES
    for b in range(nb):
        for k in range(nk):
            tt = x_ref[b, :, k * LANES:(k + 1) * LANES].T
            m0 = b * 2 * nk + 2 * k
            y_scr[m0 * CHAIN_PITCH:m0 * CHAIN_PITCH + HEAD_DIM, :] = tt[:HEAD_DIM, :]
            y_scr[(m0 + 1) * CHAIN_PITCH:(m0 + 1) * CHAIN_PITCH + HEAD_DIM, :] = tt[HEAD_DIM:, :]
    for j in range(HEAD_DIM):
        g = y_scr[pl.ds(j, LANES, stride=CHAIN_PITCH), :]
        o_ref[pl.ds(j, LANES, stride=CHAIN_PITCH), :] = g.T
    for j in range(HEAD_DIM, CHAIN_PITCH):
        o_ref[pl.ds(j, LANES, stride=CHAIN_PITCH), :] = jnp.zeros((LANES, LANES), F32)


def _to_chain_pallas(x, bsz):
    m, d = x.shape
    t_total = m // bsz
    out = pl.pallas_call(
        _to_chain_kernel,
        grid=(t_total // LANES,),
        in_specs=[pl.BlockSpec((bsz, LANES, d), lambda i: (0, i, 0))],
        out_specs=pl.BlockSpec((LANES * CHAIN_PITCH, LANES), lambda i: (i, 0)),
        out_shape=jax.ShapeDtypeStruct((t_total * CHAIN_PITCH, LANES), F32),
        scratch_shapes=[pltpu.VMEM((LANES * CHAIN_PITCH, LANES), F32)],
        compiler_params=_cparams("parallel"),
        name="to_chain",
    )(x.reshape(bsz, t_total, d))
    return out.reshape(t_total, CHAIN_PITCH, LANES)


def _from_chain_kernel(x_ref, o_ref, y_scr):
    nb = o_ref.shape[0]
    nk = o_ref.shape[2] // LANES
    for i in range(HEAD_DIM):
        g = x_ref[pl.ds(i, LANES, stride=CHAIN_PITCH), :]
        y_scr[pl.ds(i, LANES, stride=CHAIN_PITCH), :] = g.T
    for b in range(nb):
        for k in range(nk):
            m0 = b * 2 * nk + 2 * k
            tt = jnp.concatenate([y_scr[m0 * CHAIN_PITCH:m0 * CHAIN_PITCH + HEAD_DIM, :],
                                  y_scr[(m0 + 1) * CHAIN_PITCH:(m0 + 1) * CHAIN_PITCH + HEAD_DIM, :]], axis=0)
            o_ref[b, :, k * LANES:(k + 1) * LANES] = tt.T


def _from_chain_pallas(x, bsz, d):
    t_total = x.shape[0]
    out = pl.pallas_call(
        _from_chain_kernel,
        grid=(t_total // LANES,),
        in_specs=[pl.BlockSpec((LANES * CHAIN_PITCH, LANES), lambda i: (i, 0))],
        out_specs=pl.BlockSpec((bsz, LANES, d), lambda i: (0, i, 0)),
        out_shape=jax.ShapeDtypeStruct((bsz, t_total, d), F32),
        scratch_shapes=[pltpu.VMEM((LANES * CHAIN_PITCH, LANES), F32)],
        compiler_params=_cparams("parallel"),
        name="from_chain",
    )(x.reshape(t_total * CHAIN_PITCH, LANES))
    return out.reshape(bsz * t_total, d)


def _chain_param(p, bsz, heads):
    return jnp.tile(p.reshape(heads, HEAD_DIM).T, (1, bsz))


def _pad_lanes(x, c_pad):
    c = x.shape[-1]
    if c == c_pad:
        return x
    return jnp.pad(x, [(0, 0)] * (x.ndim - 1) + [(0, c_pad - c)])


def _rwkv_core(rwkva, s0, lw, bsz):
    d = rwkva[0].shape[1]
    heads = d // HEAD_DIM
    c = bsz * heads
    c_pad = -(-c // LANES) * LANES
    in_kernel_relayout = c == LANES and (rwkva[0].shape[0] // bsz) % LANES == 0
    if in_kernel_relayout:
        chain = [_to_chain_pallas(x, bsz) for x in rwkva]
    else:
        chain = [_pad_lanes(_to_chain(x, bsz, heads), c_pad) for x in rwkva]
    s0c = _pad_lanes(s0.transpose(3, 2, 0, 1).reshape(HEAD_DIM, HEAD_DIM, c), c_pad)
    lw_c = {n: _pad_lanes(_chain_param(lw[n], bsz, heads), c_pad) for n in ('k_k', 'k_a', 'r_k', 'lnx_g', 'lnx_b')}
    o, s = _rwkv_scan(*chain, s0c, lw_c)
    if in_kernel_relayout:
        o = _from_chain_pallas(o, bsz, d)
    else:
        o = _from_chain(o[..., :c], bsz, heads)
    s = s[..., :c].reshape(HEAD_DIM, HEAD_DIM, bsz, heads).transpose(2, 3, 1, 0)
    return o, s


def _prep_layer_weights(w, l):
    d = w['w_in'].shape[1]
    d_inner = w['p_ssm'].shape[1]
    heads_ssm = w['dt_bias'].shape[1]
    conv_dim = w['conv_w'].shape[2]
    rcols = w['shift_mu'].shape[1]
    dl = w['w_lora_up'].shape[1]
    s0, s1, s2 = d_inner, d_inner + conv_dim, d_inner + conv_dim + heads_ssm
    s3 = s2 + rcols
    w_in = w['w_in'][l]
    pad_h = lambda v, fill: jnp.pad(v, (0, LANES - heads_ssm), constant_values=fill).reshape(1, LANES)
    row = lambda v: v.reshape(1, -1)
    lora = jnp.zeros((LANES, 2 * d), F32)
    lora = lora.at[:dl, :d].set(w['w_lora_up'][l]).at[dl:dl + w['a_lora_up'].shape[1], d:].set(w['a_lora_up'][l])
    return {
        'w_z': w_in[:, :s0].astype(BF16),
        'w_xbc': w_in[:, s0:s1].astype(BF16),
        'w_dt': jnp.pad(w_in[:, s1:s2], ((0, 0), (0, LANES - heads_ssm))).astype(BF16),
        'w_dt_x': jnp.repeat(w_in[:, s1:s2], HEAD_DIM, axis=1).astype(BF16),
        'dt_bias_x': row(jnp.repeat(w['dt_bias'][l], HEAD_DIM)),
        'a_log_x': row(jnp.repeat(w['a_log'][l], HEAD_DIM)),
        'w_r': w_in[:, s2:s3].astype(BF16),
        'w_gate': w_in[:, s3:].astype(BF16),
        'conv_w': w['conv_w'][l], 'conv_b': row(w['conv_b'][l]),
        'dt_bias': pad_h(w['dt_bias'][l], 0.0), 'a_log': pad_h(w['a_log'][l], 0.0),
        'd_skip': row(jnp.repeat(w['d_skip'][l], HEAD_DIM)),
        'ssm_norm_w': row(w['ssm_norm_w'][l]),
        'p_ssm': w['p_ssm'][l].astype(BF16),
        'shift_mu': row(w['shift_mu'][l]), 'w0': row(w['w0'][l]), 'a0': row(w['a0'][l]),
        'lora_up': lora.astype(BF16), 'g_lora_up': w['g_lora_up'][l].astype(BF16),
        'k_k': w['k_k'][l], 'k_a': w['k_a'][l], 'r_k': w['r_k'][l].reshape(-1),
        'lnx_g': w['lnx_g'][l], 'lnx_b': w['lnx_b'][l],
        'p_rwkv': w['p_rwkv'][l].astype(BF16), 'w_out': w['w_out'][l].astype(BF16),
        'ln1_g': row(w['ln1_g'][l]), 'ln1_b': row(w['ln1_b'][l]),
        'w_ffn_in': w['w_ffn_in'][l].astype(BF16), 'w_ffn_out': w['w_ffn_out'][l].astype(BF16),
        'ln2_g': row(w['ln2_g'][l]), 'ln2_b': row(w['ln2_b'][l]),
    }


def _layer_tail(alpha, x, xb, y_ssm, o_rwkv, g_rwkv, u_gate, lw):
    del xb
    x1, x1b = _merge_out_ln(alpha, y_ssm, o_rwkv, g_rwkv, u_gate, x, lw['p_ssm'], lw['p_rwkv'], lw['w_out'],
                            lw['ln1_g'], lw['ln1_b'])
    hmid = _ffn_in(x1b, lw['w_ffn_in'])
    return _ffn_out_ln(alpha, hmid, lw['w_ffn_out'], x1, lw['ln2_g'], lw['ln2_b'])


def _in_proj(xb, lw, dt_key='w_dt'):
    return (_matmul(xb, lw['w_z']), _matmul(xb, lw['w_xbc']), _matmul(xb, lw[dt_key]),
            _matmul(xb, lw['w_r']), _matmul(xb, lw['w_gate']))


def _seq_layer(alpha, x, xb, ssm0, conv0, wkv0, shift0, lw, bsz, l_real):
    l_pad = x.shape[0] // bsz
    u_z, u_xbc, u_dt, u_r, u_gate = _in_proj(xb, lw)
    h0 = ssm0.reshape(bsz, -1, SSM_STATE)
    y_ssm, h_new, conv_new = _ssd_chunked(u_z, u_xbc, u_dt, h0, conv0, lw, bsz, l_real)
    (r, w, k, v, a, g), shift_new = _rwkv_prep(u_r, shift0, lw, bsz, True, l_real)
    if l_real < l_pad:
        cut = lambda t: t.reshape(bsz, l_pad, -1)[:, :l_real].reshape(bsz * l_real, -1)
        o, s_new = _rwkv_core([cut(t) for t in (r, w, k, v, a)], wkv0, lw, bsz)
        o = jnp.pad(o.reshape(bsz, l_real, -1), ((0, 0), (0, l_pad - l_real), (0, 0))).reshape(bsz * l_pad, -1)
    else:
        o, s_new = _rwkv_core([r, w, k, v, a], wkv0, lw, bsz)
    x2, x2b = _layer_tail(alpha, x, xb, y_ssm, o, g, u_gate, lw)
    return x2, x2b, h_new.reshape(ssm0.shape), conv_new, s_new, shift_new


def _step_layer(alpha, x, xb, ssm_all, conv_all, layer, h_acc, wkv0, shift0, lw):
    bsz = x.shape[0]
    u_z, u_xbc, u_dtx, u_r, u_gate = _in_proj(xb, lw, 'w_dt_x')
    y_ssm, h_acc, conv_new = _ssd_step(u_z, u_xbc, u_dtx, ssm_all, conv_all, layer, lw, h_acc)
    (r, w, k, v, a, g), shift_new = _rwkv_prep(u_r, shift0, lw, bsz, False)
    o, s_new = _rwkv_core([r, w, k, v, a], wkv0, lw, bsz)
    x2, x2b = _layer_tail(alpha, x, xb, y_ssm, o, g, u_gate, lw)
    return x2, x2b, h_acc, conv_new, s_new, shift_new


def kernel(x_prompt, x_sample, state_ssm, state_conv, state_wkv, state_shift, meta_tokens, w_in, conv_w, conv_b,
           dt_bias, a_log, d_skip, ssm_norm_w, p_ssm, shift_mu, w0, w_lora_up, a0, a_lora_up, g_lora_up, k_k,
           k_a, r_k, lnx_g, lnx_b, p_rwkv, w_out, ln1_g, ln1_b, w_ffn_in, w_ffn_out, ln2_g, ln2_b):
    weights = {
        'w_in': w_in, 'conv_w': conv_w, 'conv_b': conv_b, 'dt_bias': dt_bias, 'a_log': a_log,
        'd_skip': d_skip, 'ssm_norm_w': ssm_norm_w, 'p_ssm': p_ssm, 'shift_mu': shift_mu,
        'w0': w0, 'w_lora_up': w_lora_up, 'a0': a0, 'a_lora_up': a_lora_up, 'g_lora_up': g_lora_up,
        'k_k': k_k, 'k_a': k_a, 'r_k': r_k, 'lnx_g': lnx_g, 'lnx_b': lnx_b, 'p_rwkv': p_rwkv,
        'w_out': w_out, 'ln1_g': ln1_g, 'ln1_b': ln1_b, 'w_ffn_in': w_ffn_in,
        'w_ffn_out': w_ffn_out, 'ln2_g': ln2_g, 'ln2_b': ln2_b,
    }
    depth = w_in.shape[0]
    alpha = (2 * depth) ** 0.25
    bsz, seq, d = x_prompt.shape
    n_meta = meta_tokens.shape[0]
    layers = [_prep_layer_weights(weights, l) for l in range(depth)]

    xm = jnp.pad(meta_tokens.astype(F32), ((0, SSD_CHUNK - n_meta), (0, 0)))
    xp = x_prompt.reshape(bsz * seq, d)
    xs = x_sample.reshape(x_sample.shape[0], d)
    xmb, xpb, xsb = xm.astype(BF16), xp.astype(BF16), xs.astype(BF16)
    bcast = lambda t: jnp.broadcast_to(t, (bsz,) + t.shape[1:])
    ssm_p, conv_p, wkv_p, shift_p, conv_s, wkv_s, shift_s = [], [], [], [], [], [], []
    ssm_all = state_ssm.reshape(state_ssm.shape[:2] + (-1, SSM_STATE))
    ssm_s = None
    for l in range(depth):
        lw = layers[l]
        z = lambda a: jnp.zeros((1,) + a.shape[2:], F32)
        xm, xmb, hm, cm, sm, shm = _seq_layer(alpha, xm, xmb, z(state_ssm), z(state_conv), z(state_wkv),
                                              z(state_shift), lw, 1, n_meta)
        xp, xpb, h, c, s, sh = _seq_layer(alpha, xp, xpb, bcast(hm), bcast(cm), bcast(sm), bcast(shm), lw,
                                          bsz, seq)
        ssm_p.append(h), conv_p.append(c), wkv_p.append(s), shift_p.append(sh)
        xs, xsb, ssm_s, c, s, sh = _step_layer(alpha, xs, xsb, ssm_all, state_conv, l, ssm_s, state_wkv[l],
                                               state_shift[l], lw)
        conv_s.append(c), wkv_s.append(s), shift_s.append(sh)
    st = jnp.stack
    return (xp.reshape(bsz, seq, d), xs.reshape(x_sample.shape), st(ssm_p), st(conv_p), st(wkv_p), st(shift_p),
            ssm_s.reshape(state_ssm.shape), st(conv_s), st(wkv_s), st(shift_s))
```

```python
import functools
import math

import jax
import jax.numpy as jnp
from jax import lax
from jax.experimental import pallas as pl
from jax.experimental.pallas import tpu as pltpu

F32 = jnp.float32
BF16 = jnp.bfloat16

LANES = 128
N_META = 16
HEAD_DIM = 64
SSM_STATE = 128
SSM_GROUPS = 4
CONV_K = 4
SSD_CHUNK = 128
CONV_PITCH = 3
SCAN_UNROLL = 16
LN_EPS = 1e-5
RMS_EPS = 1e-5
GN_EPS = 64e-5
VMEM_LIMIT_BYTES = 56 * 1024 * 1024


def _cparams(*sem):
    return pltpu.CompilerParams(dimension_semantics=sem, vmem_limit_bytes=VMEM_LIMIT_BYTES)


def _pick_tile(n, cap):
    if n <= cap:
        return n
    best = LANES
    for t in range(LANES, cap + 1, LANES):
        if n % t == 0:
            best = t
    return best


def _silu(x):
    return x * jax.nn.sigmoid(x)


def _softplus(x):
    return jnp.maximum(x, 0.0) + jnp.log(1.0 + jnp.exp(-jnp.abs(x)))


def _layer_norm(y, g, b):
    mu = jnp.mean(y, axis=-1, keepdims=True)
    d = y - mu
    var = jnp.mean(d * d, axis=-1, keepdims=True)
    return d * lax.rsqrt(var + LN_EPS) * g + b


def _mm_kernel(x_ref, w_ref, o_ref, *wb_scr):
    if wb_scr:
        @pl.when(pl.program_id(1) == 0)
        def _():
            wb_scr[0][...] = w_ref[...].astype(BF16)

        w = wb_scr[0][...]
    else:
        w = w_ref[...]
    o_ref[...] = jnp.dot(x_ref[...], w, preferred_element_type=F32).astype(o_ref.dtype)


def _matmul(x, w, out_dtype=F32, layer=None, col0=0, n=None):
    m, k = x.shape
    n = w.shape[-1] if n is None else n
    tm = _pick_tile(m, 1024)
    tn = _pick_tile(n, 1792)
    assert col0 % LANES == 0 and n % LANES == 0
    while col0 % tn or n % tn:
        tn -= LANES
    jb = col0 // tn
    if layer is None:
        w_spec = pl.BlockSpec((k, tn), lambda j, i: (0, jb + j))
    else:
        w_spec = pl.BlockSpec((None, k, tn), lambda j, i: (layer, 0, jb + j))
    scratch = [pltpu.VMEM((k, tn), BF16)] if w.dtype == F32 else []
    return pl.pallas_call(
        _mm_kernel,
        grid=(n // tn, m // tm),
        in_specs=[pl.BlockSpec((tm, k), lambda j, i: (i, 0)), w_spec],
        out_specs=pl.BlockSpec((tm, tn), lambda j, i: (i, j)),
        out_shape=jax.ShapeDtypeStruct((m, n), out_dtype),
        scratch_shapes=scratch,
        compiler_params=_cparams("parallel", "arbitrary"),
        name="matmul",
    )(x, w)


def _merge_kernel(alpha, ys_ref, yr_ref, g_ref, ug_ref, x_ref, ps_ref, pr_ref, wo_ref, lg_ref, lb_ref,
                  o_ref, ob_ref):
    d = o_ref.shape[1]
    a = jnp.dot(ys_ref[...], ps_ref[...], preferred_element_type=F32)
    yr = (yr_ref[...] * g_ref[...]).astype(BF16)
    b = jnp.dot(yr, pr_ref[...], preferred_element_type=F32)
    gates = jax.nn.sigmoid(ug_ref[...])
    merged = gates[:, :d] * a + gates[:, d:] * b
    y = alpha * x_ref[...] + jnp.dot(merged.astype(BF16), wo_ref[...], preferred_element_type=F32)
    out = _layer_norm(y, lg_ref[...], lb_ref[...])
    o_ref[...] = out
    ob_ref[...] = out.astype(BF16)


def _merge_out_ln(alpha, y_ssm, y_rwkv, g_rwkv, u_gate, x, p_ssm, p_rwkv, w_out, ln_g, ln_b):
    m, d = x.shape
    tm = _pick_tile(m, 512)
    row = lambda c: pl.BlockSpec((tm, c), lambda i: (i, 0))
    full = lambda a: pl.BlockSpec(a.shape, lambda i: (0, 0))
    return pl.pallas_call(
        functools.partial(_merge_kernel, alpha),
        grid=(m // tm,),
        in_specs=[row(y_ssm.shape[1]), row(d), row(d), row(2 * d), row(d),
                  full(p_ssm), full(p_rwkv), full(w_out), full(ln_g), full(ln_b)],
        out_specs=[row(d), row(d)],
        out_shape=[jax.ShapeDtypeStruct((m, d), F32), jax.ShapeDtypeStruct((m, d), BF16)],
        compiler_params=_cparams("parallel"),
        name="merge_out_ln",
    )(y_ssm, y_rwkv, g_rwkv, u_gate, x, p_ssm, p_rwkv, w_out, ln_g, ln_b)


def _swiglu_kernel(x_ref, wg_ref, wu_ref, o_ref, wgb_scr, wub_scr):
    @pl.when(pl.program_id(1) == 0)
    def _():
        wgb_scr[...] = wg_ref[...].astype(BF16)
        wub_scr[...] = wu_ref[...].astype(BF16)

    x = x_ref[...]
    hg = jnp.dot(x, wgb_scr[...], preferred_element_type=F32)
    hu = jnp.dot(x, wub_scr[...], preferred_element_type=F32)
    o_ref[...] = (_silu(hg) * hu).astype(o_ref.dtype)


def _ffn_in(x, w_ffn_in_all, layer):
    m, k = x.shape
    dff = w_ffn_in_all.shape[2] // 2
    tm = _pick_tile(m, 512)
    tn = _pick_tile(dff, 1408)
    nj = dff // tn
    return pl.pallas_call(
        _swiglu_kernel,
        grid=(nj, m // tm),
        in_specs=[pl.BlockSpec((tm, k), lambda j, i: (i, 0)),
                  pl.BlockSpec((None, k, tn), lambda j, i: (layer, 0, j)),
                  pl.BlockSpec((None, k, tn), lambda j, i: (layer, 0, j + nj))],
        out_specs=pl.BlockSpec((tm, tn), lambda j, i: (i, j)),
        out_shape=jax.ShapeDtypeStruct((m, dff), BF16),
        scratch_shapes=[pltpu.VMEM((k, tn), BF16)] * 2,
        compiler_params=_cparams("parallel", "arbitrary"),
        name="ffn_in_swiglu",
    )(x, w_ffn_in_all, w_ffn_in_all)


def _ffn_out_kernel(alpha, h_ref, w_ref, x_ref, lg_ref, lb_ref, o_ref, ob_ref):
    y = alpha * x_ref[...] + jnp.dot(h_ref[...], w_ref[...], preferred_element_type=F32)
    out = _layer_norm(y, lg_ref[...], lb_ref[...])
    o_ref[...] = out
    ob_ref[...] = out.astype(BF16)


def _ffn_out_ln(alpha, h, w_ffn_out, x, ln_g, ln_b):
    m, d = x.shape
    k = h.shape[1]
    tm = _pick_tile(m, 512)
    row = lambda c: pl.BlockSpec((tm, c), lambda i: (i, 0))
    full = lambda a: pl.BlockSpec(a.shape, lambda i: (0, 0))
    return pl.pallas_call(
        functools.partial(_ffn_out_kernel, alpha),
        grid=(m // tm,),
        in_specs=[row(k), full(w_ffn_out), row(d), full(ln_g), full(ln_b)],
        out_specs=[row(d), row(d)],
        out_shape=[jax.ShapeDtypeStruct((m, d), F32), jax.ShapeDtypeStruct((m, d), BF16)],
        compiler_params=_cparams("parallel"),
        name="ffn_out_ln",
    )(h, w_ffn_out, x, ln_g, ln_b)


def _ssd_chunk_kernel(l_real, nchunks,
                      uz_ref, uxbc_ref, udt_ref, h0_ref, cpre_ref, convw_ref, convb_ref, dtb_ref, alog_ref,
                      dskip_ref, normw_ref,
                      y_ref, hout_ref, ctail_ref,
                      ht_scr, cbuf_scr, xbc_scr, y_scr):
    q = SSD_CHUNK
    d_inner = uz_ref.shape[1]
    gw = d_inner // SSM_GROUPS
    c = pl.program_id(1)

    @pl.when(c == 0)
    def _():
        for kb in range(d_inner // LANES):
            ht_scr[:, kb * LANES:(kb + 1) * LANES] = h0_ref[0, kb * LANES:(kb + 1) * LANES, :].T
        for ct in range(cbuf_scr.shape[0]):
            cbuf_scr[ct, pl.ds(5 * CONV_PITCH, CONV_K - 1, stride=CONV_PITCH), :] = \
                cpre_ref[0, :, ct * LANES:(ct + 1) * LANES]

    rows_at = lambda r0, n: pl.ds(r0 * CONV_PITCH, n, stride=CONV_PITCH)
    for ct in range(cbuf_scr.shape[0]):
        cols = slice(ct * LANES, (ct + 1) * LANES)
        cbuf_scr[ct, rows_at(8, q), :] = uxbc_ref[:, cols]
        acc = convb_ref[:, cols] + cbuf_scr[ct, rows_at(5, q), :] * convw_ref[0:1, cols]
        for k in range(1, CONV_K):
            acc = acc + cbuf_scr[ct, rows_at(5 + k, q), :] * convw_ref[k:k + 1, cols]
        xbc_scr[:, cols] = _silu(acc)
        tail = cbuf_scr[ct, rows_at(l_real + 5, CONV_K - 1), :]
        cbuf_scr[ct, rows_at(5, CONV_K - 1), :] = tail

    rows = lax.broadcasted_iota(jnp.int32, (q, LANES), 0)
    dt = _softplus(udt_ref[...] + dtb_ref[...])
    if l_real < q:
        dt = jnp.where(rows < l_real, dt, 0.0)
    da = dt * (-jnp.exp(alog_ref[...]))
    acum = da
    s = 1
    while s < q:
        acum = acum + jnp.where(rows >= s, pltpu.roll(acum, s, axis=0), 0.0)
        s *= 2
    acum_t = acum.T
    dt_t = dt.T
    a_last = acum[q - 1:q, :]
    st_t = dt_t * jnp.exp(acum_t[:, q - 1:q] - acum_t)
    ii = lax.broadcasted_iota(jnp.int32, (q, q), 0)
    jj = lax.broadcasted_iota(jnp.int32, (q, q), 1)
    causal = ii >= jj
    low = lax.broadcasted_iota(jnp.int32, (q, LANES), 1) < HEAD_DIM

    for g in range(SSM_GROUPS):
        bm = xbc_scr[:, d_inner + g * SSM_STATE:d_inner + (g + 1) * SSM_STATE]
        cm = xbc_scr[:, d_inner + (SSM_GROUPS + g) * SSM_STATE:d_inner + (SSM_GROUPS + g + 1) * SSM_STATE]
        cmb = cm.astype(BF16)
        cb = lax.dot_general(cmb, bm.astype(BF16), (((1,), (1,)), ((), ())), preferred_element_type=F32)
        bm_t = bm.T
        for pr in range(gw // LANES):
            lanes = slice(g * gw + pr * LANES, g * gw + (pr + 1) * LANES)
            hd0 = (g * gw + pr * LANES) // HEAD_DIM
            x_pair = xbc_scr[:, lanes]
            xb = x_pair.astype(BF16)
            yo = jnp.dot(cmb, ht_scr[:, lanes].astype(BF16), preferred_element_type=F32)
            yd, st, ea, cd = [], [], [], []
            for hd in (hd0, hd0 + 1):
                a_col = jnp.broadcast_to(acum[:, hd:hd + 1], (q, q))
                lmat = jnp.exp(jnp.where(causal, a_col - acum_t[hd:hd + 1, :], -jnp.inf))
                wd = (cb * lmat * dt_t[hd:hd + 1, :]).astype(BF16)
                yd.append(jnp.dot(wd, xb, preferred_element_type=F32))
                ea.append(jnp.exp(a_col))
                st.append(jnp.dot((bm_t * st_t[hd:hd + 1, :]).astype(BF16), xb, preferred_element_type=F32))
                cd.append(jnp.broadcast_to(jnp.exp(a_last[:, hd:hd + 1]), (SSM_STATE, LANES)))
            y_pair = jnp.where(low, yd[0] + yo * ea[0], yd[1] + yo * ea[1]) + x_pair * dskip_ref[:, lanes]
            y_scr[:, lanes] = y_pair
            ht_scr[:, lanes] = ht_scr[:, lanes] * jnp.where(low, cd[0], cd[1]) + jnp.where(low, st[0], st[1])

    for g in range(SSM_GROUPS):
        cols = slice(g * gw, (g + 1) * gw)
        yg = y_scr[:, cols] * _silu(uz_ref[:, cols])
        ms = jnp.mean(yg * yg, axis=-1, keepdims=True)
        y_ref[:, cols] = (yg * lax.rsqrt(ms + RMS_EPS) * normw_ref[:, cols]).astype(y_ref.dtype)

    @pl.when(c == nchunks - 1)
    def _():
        for kb in range(d_inner // LANES):
            hout_ref[0, kb * LANES:(kb + 1) * LANES, :] = ht_scr[:, kb * LANES:(kb + 1) * LANES].T
        for ct in range(cbuf_scr.shape[0]):
            ctail_ref[0, :, ct * LANES:(ct + 1) * LANES] = \
                cbuf_scr[ct, pl.ds(5 * CONV_PITCH, CONV_K - 1, stride=CONV_PITCH), :]


def _ssd_chunked(u_z, u_xbc, u_dt, h0, conv_pre, lw, bsz, l_real):
    m, d_inner = u_z.shape
    conv_dim = u_xbc.shape[1]
    q = SSD_CHUNK
    nchunks = m // bsz // q
    row = lambda cdim: pl.BlockSpec((q, cdim), lambda b, c: (b * nchunks + c, 0))
    full = lambda a: pl.BlockSpec(a.shape, lambda b, c: (0, 0))
    per_b = lambda a: pl.BlockSpec((1,) + a.shape[1:], lambda b, c: (b, 0, 0))
    hshape = jax.ShapeDtypeStruct(h0.shape, F32)
    cshape = jax.ShapeDtypeStruct(conv_pre.shape, F32)
    return pl.pallas_call(
        functools.partial(_ssd_chunk_kernel, min(l_real, q), nchunks),
        grid=(bsz, nchunks),
        in_specs=[row(d_inner), row(conv_dim), row(LANES), per_b(h0), per_b(conv_pre),
                  full(lw['conv_w']), full(lw['conv_b']), full(lw['dt_bias']), full(lw['a_log']),
                  full(lw['d_skip']), full(lw['ssm_norm_w'])],
        out_specs=[row(d_inner), per_b(h0), per_b(conv_pre)],
        out_shape=[jax.ShapeDtypeStruct((m, d_inner), BF16), hshape, cshape],
        scratch_shapes=[pltpu.VMEM((SSM_STATE, d_inner), F32),
                        pltpu.VMEM((conv_dim // LANES, (q + 8) * CONV_PITCH, LANES), F32),
                        pltpu.VMEM((q, conv_dim), F32),
                        pltpu.VMEM((q, d_inner), F32)],
        compiler_params=_cparams("parallel", "arbitrary"),
        name="ssd_chunk",
    )(u_z, u_xbc, u_dt, h0, conv_pre, lw['conv_w'], lw['conv_b'], lw['dt_bias'], lw['a_log'],
      lw['d_skip'], lw['ssm_norm_w'])


def _row_to_col(row):
    r_i = lax.broadcasted_iota(jnp.int32, (LANES, LANES), 0)
    c_i = lax.broadcasted_iota(jnp.int32, (LANES, LANES), 1)
    return jnp.sum(jnp.where(r_i == c_i, jnp.broadcast_to(row, (LANES, LANES)), 0.0), axis=1, keepdims=True)


def _ssd_step_kernel(aliased, uz_ref, uxbc_ref, udtx_ref, h0_ref, cpre_ref, convw_ref, convb_ref, dtbx_ref,
                     alogx_ref, dskip_ref, normw_ref, *rest):
    y_ref, hout_ref, ctail_ref, ht_scr = rest[1:] if aliased else rest
    d_inner = uz_ref.shape[2]
    gw = d_inner // SSM_GROUPS
    u = uxbc_ref[0]
    pre = cpre_ref[0, 0]
    acc = convb_ref[...] + u * convw_ref[CONV_K - 1:CONV_K, :]
    for k in range(CONV_K - 1):
        acc = acc + pre[k:k + 1, :] * convw_ref[k:k + 1, :]
    xbc = _silu(acc)
    ctail_ref[0, 0:CONV_K - 2, :] = pre[1:, :]
    ctail_ref[0, CONV_K - 2:CONV_K - 1, :] = u
    dt = _softplus(udtx_ref[0] + dtbx_ref[...])
    dec = jnp.exp(dt * (-jnp.exp(alogx_ref[...])))
    x_row = xbc[:, :d_inner]
    xdt = x_row * dt
    for kb in range(d_inner // LANES):
        ht_scr[:, kb * LANES:(kb + 1) * LANES] = h0_ref[0, 0, kb * LANES:(kb + 1) * LANES, :].T
    y_parts = []
    for g in range(SSM_GROUPS):
        cols = slice(g * gw, (g + 1) * gw)
        bm = xbc[:, d_inner + g * SSM_STATE:d_inner + (g + 1) * SSM_STATE]
        cm = xbc[:, d_inner + (SSM_GROUPS + g) * SSM_STATE:d_inner + (SSM_GROUPS + g + 1) * SSM_STATE]
        hn = ht_scr[:, cols] * dec[:, cols] + _row_to_col(bm) * xdt[:, cols]
        ht_scr[:, cols] = hn
        y_parts.append(jnp.sum(hn * _row_to_col(cm), axis=0, keepdims=True))
    y_row = jnp.concatenate(y_parts, axis=1) + x_row * dskip_ref[...]
    y_row = y_row * _silu(uz_ref[0])
    outs = []
    for g in range(SSM_GROUPS):
        yg = y_row[:, g * gw:(g + 1) * gw]
        ms = jnp.mean(yg * yg, axis=-1, keepdims=True)
        outs.append(yg * lax.rsqrt(ms + RMS_EPS))
    y_ref[0] = (jnp.concatenate(outs, axis=1) * normw_ref[...]).astype(y_ref.dtype)
    for kb in range(d_inner // LANES):
        hout_ref[0, 0, kb * LANES:(kb + 1) * LANES, :] = ht_scr[:, kb * LANES:(kb + 1) * LANES].T


def _ssd_step(u_z, u_xbc, u_dtx, ssm_all, conv_all, layer, lw, h_acc):
    bsz, d_inner = u_z.shape
    r3 = lambda a: a.reshape(bsz, 1, a.shape[1])
    per_b = lambda a: pl.BlockSpec((1,) + a.shape[1:], lambda b: (b, 0, 0))
    per_lb = lambda a: pl.BlockSpec((1, 1) + a.shape[2:], lambda b: (layer, b, 0, 0))
    full = lambda a: pl.BlockSpec(a.shape, lambda b: (0, 0))
    uz3, ux3, ud3 = r3(u_z), r3(u_xbc), r3(u_dtx)
    cshape = conv_all.shape[1:]
    args = [uz3, ux3, ud3, ssm_all, conv_all, lw['conv_w'], lw['conv_b'], lw['dt_bias_x'], lw['a_log_x'],
            lw['d_skip'], lw['ssm_norm_w']]
    in_specs = [per_b(uz3), per_b(ux3), per_b(ud3), per_lb(ssm_all), per_lb(conv_all)]
    in_specs += [full(a) for a in args[5:]]
    aliases = {}
    if h_acc is not None:
        args.append(h_acc)
        in_specs.append(pl.BlockSpec(memory_space=pl.ANY))
        aliases = {len(args) - 1: 1}
    y, h, ct = pl.pallas_call(
        functools.partial(_ssd_step_kernel, h_acc is not None),
        grid=(bsz,),
        in_specs=in_specs,
        out_specs=[per_b(uz3), per_lb(ssm_all), pl.BlockSpec((1,) + cshape[1:], lambda b: (b, 0, 0))],
        out_shape=[jax.ShapeDtypeStruct((bsz, 1, d_inner), BF16), jax.ShapeDtypeStruct(ssm_all.shape, F32),
                   jax.ShapeDtypeStruct(cshape, F32)],
        scratch_shapes=[pltpu.VMEM((SSM_STATE, d_inner), F32)],
        input_output_aliases=aliases,
        compiler_params=_cparams("arbitrary"),
        name="ssd_step",
    )(*args)
    return y.reshape(bsz, d_inner), h, ct


def _rwkv_prep_kernel(seq_mode, l_real, ur_ref, prev_ref, mu_ref, w0_ref, a0_ref, lora_ref, gup_ref,
                      r_ref, w_ref, k_ref, v_ref, a_ref, g_ref, last_ref, carry_scr):
    t, cols = ur_ref.shape
    d = r_ref.shape[1]
    u = ur_ref[...]
    if seq_mode:
        c = pl.program_id(1)

        @pl.when(c == 0)
        def _():
            carry_scr[...] = jnp.broadcast_to(prev_ref[0], carry_scr.shape)

        rows = lax.broadcasted_iota(jnp.int32, (t, cols), 0)
        prev = jnp.where(rows == 0, jnp.broadcast_to(carry_scr[0:1, :], (t, cols)), pltpu.roll(u, 1, axis=0))
        lrow = (l_real - 1) % t
        carry_scr[...] = jnp.broadcast_to(u[lrow:lrow + 1, :], carry_scr.shape)
        last_ref[0] = u[lrow:lrow + 1, :]
    else:
        prev = prev_ref[...]
        last_ref[...] = u
    sh = u + (prev - u) * mu_ref[...]
    r_ref[...] = sh[:, 0:d]
    k_ref[...] = sh[:, d:2 * d]
    v_ref[...] = sh[:, 2 * d:3 * d]
    la = sh[:, 3 * d:3 * d + LANES]
    lane = lax.broadcasted_iota(jnp.int32, la.shape, 1)
    la = jnp.where(lane < LANES // 2, jnp.tanh(la), la).astype(BF16)
    pre = jnp.dot(la, lora_ref[...], preferred_element_type=F32)
    w_log = -_softplus(-(w0_ref[...] + pre[:, :d])) - 0.5
    w_ref[...] = jnp.exp(-jnp.exp(w_log))
    a_ref[...] = jax.nn.sigmoid(a0_ref[...] + pre[:, d:])
    gl = jax.nn.sigmoid(sh[:, 3 * d + LANES:]).astype(BF16)
    g_ref[...] = jnp.dot(gl, gup_ref[...], preferred_element_type=F32)


def _rwkv_prep(u_r, shift_prev, lw, bsz, seq_mode, l_real=None):
    m, cols = u_r.shape
    d = lw['w0'].shape[1]
    full = lambda a: pl.BlockSpec(a.shape, lambda *_: (0,) * a.ndim)
    outs = [jax.ShapeDtypeStruct((m, d), F32)] * 6
    args = (lw['shift_mu'], lw['w0'], lw['a0'], lw['lora_up'], lw['g_lora_up'])
    if seq_mode:
        seq = m // bsz
        t = _pick_tile(seq, 256)
        nblk = seq // t
        l_real = seq if l_real is None else l_real
        row = lambda cdim: pl.BlockSpec((t, cdim), lambda b, c: (b * nblk + c, 0))
        prev3 = shift_prev.reshape(bsz, 1, cols)
        per_b = pl.BlockSpec((1, 1, cols), lambda b, c: (b, 0, 0))
        res = pl.pallas_call(
            functools.partial(_rwkv_prep_kernel, True, l_real),
            grid=(bsz, nblk),
            in_specs=[row(cols), per_b] + [full(a) for a in args],
            out_specs=[row(d)] * 6 + [per_b],
            out_shape=outs + [jax.ShapeDtypeStruct((bsz, 1, cols), F32)],
            scratch_shapes=[pltpu.VMEM((8, cols), F32)],
            compiler_params=_cparams("parallel", "arbitrary"),
            name="rwkv_prep_seq",
        )(u_r, prev3, *args)
        return res[:6], res[6].reshape(bsz, cols)
    t = _pick_tile(m, 256)
    row = lambda cdim: pl.BlockSpec((t, cdim), lambda i: (i, 0))
    res = pl.pallas_call(
        functools.partial(_rwkv_prep_kernel, False, None),
        grid=(m // t,),
        in_specs=[row(cols), row(cols)] + [full(a) for a in args],
        out_specs=[row(d)] * 6 + [row(cols)],
        out_shape=outs + [jax.ShapeDtypeStruct((m, cols), F32)],
        scratch_shapes=[pltpu.VMEM((8, cols), F32)],
        compiler_params=_cparams("parallel"),
        name="rwkv_prep_step",
    )(u_r, shift_prev, *args)
    return res[:6], res[6]


def _rwkv_scan_kernel(tb, r_ref, w_ref, k_ref, v_ref, a_ref, s0_ref, kk_ref, ka_ref, rk_ref, lg_ref, lb_ref,
                      o_ref, sout_ref, s_scr, g_scr, sa_scr, nkk_scr, b_scr, km_scr, rt_scr, bon_scr):
    hd = HEAD_DIM
    blk = pl.program_id(1)
    pad_rows = o_ref.shape[1] - hd

    @pl.when(blk == 0)
    def _():
        s_scr[...] = s0_ref[...]

    def prep(t, slot):
        k = k_ref[t, 0:hd, :]
        a = a_ref[t, 0:hd, :]
        r = r_ref[t, 0:hd, :]
        kk = k * kk_ref[...]
        kk = kk * lax.rsqrt(jnp.maximum(jnp.sum(kk * kk, axis=0, keepdims=True), 1e-24))
        g_prev = g_scr[...]
        g = g_prev * w_ref[t, 0:hd, :]
        g_scr[...] = g
        ginv = 1.0 / g
        nkk_scr[slot] = -(kk * g_prev)
        b_scr[slot] = kk * a * ginv
        km = k * (1.0 + (a - 1.0) * ka_ref[...])
        km_scr[slot] = km * ginv
        rt_scr[slot] = r * g
        bon_scr[slot] = jnp.broadcast_to(jnp.sum(r * km * rk_ref[...], axis=0, keepdims=True), (8, LANES))

    def step(t, slot, has_next):
        nslot = 1 - slot
        if has_next:
            prep(t + 1, nslot)
        v = v_ref[t, 0:hd, :]
        sa = sa_scr[...]
        zero = jnp.zeros((hd, LANES), F32)

        def jbody(j, carry):
            o, sa_next = carry
            sj = s_scr[j] + sa * b_scr[slot, pl.ds(j, 1), :] + v * km_scr[slot, pl.ds(j, 1), :]
            s_scr[j] = sj
            o = o + sj * rt_scr[slot, pl.ds(j, 1), :]
            if has_next:
                sa_next = sa_next + sj * nkk_scr[nslot, pl.ds(j, 1), :]
            return o, sa_next

        o, sa_next = lax.fori_loop(0, hd, jbody, (zero, zero), unroll=SCAN_UNROLL)
        if has_next:
            sa_scr[...] = sa_next
        mu = jnp.mean(o, axis=0, keepdims=True)
        dlt = o - mu
        var = jnp.mean(dlt * dlt, axis=0, keepdims=True)
        on = dlt * lax.rsqrt(var + GN_EPS) * lg_ref[...] + lb_ref[...]
        o_ref[t, 0:hd, :] = on + bon_scr[slot, 0:1, :] * v
        if pad_rows:
            o_ref[t, hd:hd + pad_rows, :] = jnp.zeros((pad_rows, LANES), F32)

    g_scr[...] = jnp.ones((hd, LANES), F32)
    prep(0, 0)
    sa0 = s_scr[0] * nkk_scr[0, 0:1, :]
    for j in range(1, hd):
        sa0 = sa0 + s_scr[j] * nkk_scr[0, j:j + 1, :]
    sa_scr[...] = sa0

    if tb >= 2:
        def pair(p, carry):
            step(2 * p, 0, True)
            step(2 * p + 1, 1, True)
            return carry

        lax.fori_loop(0, tb // 2 - 1, pair, 0)
        step(tb - 2, 0, True)
        step(tb - 1, 1, False)
    else:
        step(0, 0, False)

    for j in range(hd):
        s_scr[j] = s_scr[j] * g_scr[j:j + 1, :]

    @pl.when(blk == pl.num_programs(1) - 1)
    def _():
        sout_ref[...] = s_scr[...]


def _rwkv_scan(r, w, k, v, a, s0, lw_c):
    t_total, rows, c = r.shape
    hd = HEAD_DIM
    tb = 32 if t_total % 32 == 0 else t_total
    assert tb == 1 or tb % 2 == 0
    nb = t_total // tb
    ncl = c // LANES
    seq = pl.BlockSpec((tb, rows, LANES), lambda cl, i: (i, 0, cl))
    st = pl.BlockSpec((hd, hd, LANES), lambda cl, i: (0, 0, cl))
    par = pl.BlockSpec((hd, LANES), lambda cl, i: (0, cl))
    return pl.pallas_call(
        functools.partial(_rwkv_scan_kernel, tb),
        grid=(ncl, nb),
        in_specs=[seq] * 5 + [st] + [par] * 5,
        out_specs=[seq, st],
        out_shape=[jax.ShapeDtypeStruct((t_total, rows, c), F32), jax.ShapeDtypeStruct((hd, hd, c), F32)],
        scratch_shapes=[pltpu.VMEM((hd, hd, LANES), F32)] + [pltpu.VMEM((hd, LANES), F32)] * 2
        + [pltpu.VMEM((2, hd, LANES), F32)] * 4 + [pltpu.VMEM((2, 8, LANES), F32)],
        compiler_params=_cparams("parallel", "arbitrary"),
        name="rwkv_scan",
    )(r, w, k, v, a, s0, lw_c['k_k'], lw_c['k_a'], lw_c['r_k'], lw_c['lnx_g'], lw_c['lnx_b'])


def _to_chain(x, bsz, heads):
    t = x.shape[0] // bsz
    return x.reshape(bsz, t, heads, HEAD_DIM).transpose(1, 3, 0, 2).reshape(t, HEAD_DIM, bsz * heads)


def _from_chain(x, bsz, heads):
    t = x.shape[0]
    return x.reshape(t, HEAD_DIM, bsz, heads).transpose(2, 0, 3, 1).reshape(bsz * t, heads * HEAD_DIM)


CHAIN_PITCH = HEAD_DIM + 8


def _to_chain_kernel(x_ref, o_ref, y_scr):
    nb = x_ref.shape[0]
    nk = x_ref.shape[2] // LANES
    for b in range(nb):
        for k in range(nk):
            tt = x_ref[b, :, k * LANES:(k + 1) * LANES].T
            m0 = b * 2 * nk + 2 * k
            y_scr[m0 * CHAIN_PITCH:m0 * CHAIN_PITCH + HEAD_DIM, :] = tt[:HEAD_DIM, :]
            y_scr[(m0 + 1) * CHAIN_PITCH:(m0 + 1) * CHAIN_PITCH + HEAD_DIM, :] = tt[HEAD_DIM:, :]
    for j in range(HEAD_DIM):
        g = y_scr[pl.ds(j, LANES, stride=CHAIN_PITCH), :]
        o_ref[pl.ds(j, LANES, stride=CHAIN_PITCH), :] = g.T
    for j in range(HEAD_DIM, CHAIN_PITCH):
        o_ref[pl.ds(j, LANES, stride=CHAIN_PITCH), :] = jnp.zeros((LANES, LANES), F32)


def _to_chain_pallas(x, bsz):
    m, d = x.shape
    t_total = m // bsz
    out = pl.pallas_call(
        _to_chain_kernel,
        grid=(t_total // LANES,),
        in_specs=[pl.BlockSpec((bsz, LANES, d), lambda i: (0, i, 0))],
        out_specs=pl.BlockSpec((LANES * CHAIN_PITCH, LANES), lambda i: (i, 0)),
        out_shape=jax.ShapeDtypeStruct((t_total * CHAIN_PITCH, LANES), F32),
        scratch_shapes=[pltpu.VMEM((LANES * CHAIN_PITCH, LANES), F32)],
        compiler_params=_cparams("parallel"),
        name="to_chain",
    )(x.reshape(bsz, t_total, d))
    return out.reshape(t_total, CHAIN_PITCH, LANES)


def _from_chain_kernel(x_ref, o_ref, y_scr):
    nb = o_ref.shape[0]
    nk = o_ref.shape[2] // LANES
    for i in range(HEAD_DIM):
        g = x_ref[pl.ds(i, LANES, stride=CHAIN_PITCH), :]
        y_scr[pl.ds(i, LANES, stride=CHAIN_PITCH), :] = g.T
    for b in range(nb):
        for k in range(nk):
            m0 = b * 2 * nk + 2 * k
            tt = jnp.concatenate([y_scr[m0 * CHAIN_PITCH:m0 * CHAIN_PITCH + HEAD_DIM, :],
                                  y_scr[(m0 + 1) * CHAIN_PITCH:(m0 + 1) * CHAIN_PITCH + HEAD_DIM, :]], axis=0)
            o_ref[b, :, k * LANES:(k + 1) * LANES] = tt.T


def _from_chain_pallas(x, bsz, d):
    t_total = x.shape[0]
    out = pl.pallas_call(
        _from_chain_kernel,
        grid=(t_total // LANES,),
        in_specs=[pl.BlockSpec((LANES * CHAIN_PITCH, LANES), lambda i: (i, 0))],
        out_specs=pl.BlockSpec((bsz, LANES, d), lambda i: (0, i, 0)),
        out_shape=jax.ShapeDtypeStruct((bsz, t_total, d), F32),
        scratch_shapes=[pltpu.VMEM((LANES * CHAIN_PITCH, LANES), F32)],
        compiler_params=_cparams("parallel"),
        name="from_chain",
    )(x.reshape(t_total * CHAIN_PITCH, LANES))
    return out.reshape(bsz * t_total, d)


def _chain_param(p, bsz, heads):
    return jnp.tile(p.reshape(heads, HEAD_DIM).T, (1, bsz))


def _pad_lanes(x, c_pad):
    c = x.shape[-1]
    if c == c_pad:
        return x
    return jnp.pad(x, [(0, 0)] * (x.ndim - 1) + [(0, c_pad - c)])


def _rwkv_core(rwkva, s0, lw, bsz):
    d = rwkva[0].shape[1]
    heads = d // HEAD_DIM
    c = bsz * heads
    c_pad = -(-c // LANES) * LANES
    in_kernel_relayout = c == LANES and (rwkva[0].shape[0] // bsz) % LANES == 0
    if in_kernel_relayout:
        chain = [_to_chain_pallas(x, bsz) for x in rwkva]
    else:
        chain = [_pad_lanes(_to_chain(x, bsz, heads), c_pad) for x in rwkva]
    s0c = _pad_lanes(s0.transpose(3, 2, 0, 1).reshape(HEAD_DIM, HEAD_DIM, c), c_pad)
    lw_c = {n: _pad_lanes(_chain_param(lw[n], bsz, heads), c_pad) for n in ('k_k', 'k_a', 'r_k', 'lnx_g', 'lnx_b')}
    o, s = _rwkv_scan(*chain, s0c, lw_c)
    if in_kernel_relayout:
        o = _from_chain_pallas(o, bsz, d)
    else:
        o = _from_chain(o[..., :c], bsz, heads)
    s = s[..., :c].reshape(HEAD_DIM, HEAD_DIM, bsz, heads).transpose(2, 3, 1, 0)
    return o, s


def _prep_layer_weights(w, l):
    d = w['w_in'].shape[1]
    d_inner = w['p_ssm'].shape[1]
    heads_ssm = w['dt_bias'].shape[1]
    conv_dim = w['conv_w'].shape[2]
    rcols = w['shift_mu'].shape[1]
    dl = w['w_lora_up'].shape[1]
    s0, s1, s2 = d_inner, d_inner + conv_dim, d_inner + conv_dim + heads_ssm
    s3 = s2 + rcols
    w_in = w['w_in'][l]
    pad_h = lambda v, fill: jnp.pad(v, (0, LANES - heads_ssm), constant_values=fill).reshape(1, LANES)
    row = lambda v: v.reshape(1, -1)
    lora = jnp.zeros((LANES, 2 * d), F32)
    lora = lora.at[:dl, :d].set(w['w_lora_up'][l]).at[dl:dl + w['a_lora_up'].shape[1], d:].set(w['a_lora_up'][l])
    return {
        'w_in_all': w['w_in'], 'layer': l, 'zxd_cols': ((0, s0), (s0, s1 - s0), (s1, LANES)),
        'w_ffn_in_all': w['w_ffn_in'],
        'w_dt_x': jnp.repeat(w_in[:, s1:s2], HEAD_DIM, axis=1).astype(BF16),
        'dt_bias_x': row(jnp.repeat(w['dt_bias'][l], HEAD_DIM)),
        'a_log_x': row(jnp.repeat(w['a_log'][l], HEAD_DIM)),
        'w_r': w_in[:, s2:s3], 'w_gate': w_in[:, s3:],
        'conv_w': w['conv_w'][l], 'conv_b': row(w['conv_b'][l]),
        'dt_bias': pad_h(w['dt_bias'][l], 0.0), 'a_log': pad_h(w['a_log'][l], 0.0),
        'd_skip': row(jnp.repeat(w['d_skip'][l], HEAD_DIM)),
        'ssm_norm_w': row(w['ssm_norm_w'][l]),
        'p_ssm': w['p_ssm'][l].astype(BF16),
        'shift_mu': row(w['shift_mu'][l]), 'w0': row(w['w0'][l]), 'a0': row(w['a0'][l]),
        'lora_up': lora.astype(BF16), 'g_lora_up': w['g_lora_up'][l].astype(BF16),
        'k_k': w['k_k'][l], 'k_a': w['k_a'][l], 'r_k': w['r_k'][l].reshape(-1),
        'lnx_g': w['lnx_g'][l], 'lnx_b': w['lnx_b'][l],
        'p_rwkv': w['p_rwkv'][l].astype(BF16), 'w_out': w['w_out'][l].astype(BF16),
        'ln1_g': row(w['ln1_g'][l]), 'ln1_b': row(w['ln1_b'][l]),
        'w_ffn_out': w['w_ffn_out'][l].astype(BF16),
        'ln2_g': row(w['ln2_g'][l]), 'ln2_b': row(w['ln2_b'][l]),
    }


def _layer_tail(alpha, x, xb, y_ssm, o_rwkv, g_rwkv, u_gate, lw):
    del xb
    x1, x1b = _merge_out_ln(alpha, y_ssm, o_rwkv, g_rwkv, u_gate, x, lw['p_ssm'], lw['p_rwkv'], lw['w_out'],
                            lw['ln1_g'], lw['ln1_b'])
    hmid = _ffn_in(x1b, lw['w_ffn_in_all'], lw['layer'])
    return _ffn_out_ln(alpha, hmid, lw['w_ffn_out'], x1, lw['ln2_g'], lw['ln2_b'])


def _in_proj(xb, lw, dt_expanded=False):
    z, xbc, dt = (_matmul(xb, lw['w_in_all'], layer=lw['layer'], col0=c0, n=n) for c0, n in lw['zxd_cols'])
    if dt_expanded:
        dt = _matmul(xb, lw['w_dt_x'])
    return z, xbc, dt, _matmul(xb, lw['w_r']), _matmul(xb, lw['w_gate'])


def _seq_layer(alpha, x, xb, ssm0, conv0, wkv0, shift0, lw, bsz, l_real):
    l_pad = x.shape[0] // bsz
    u_z, u_xbc, u_dt, u_r, u_gate = _in_proj(xb, lw)
    h0 = ssm0.reshape(bsz, -1, SSM_STATE)
    y_ssm, h_new, conv_new = _ssd_chunked(u_z, u_xbc, u_dt, h0, conv0, lw, bsz, l_real)
    (r, w, k, v, a, g), shift_new = _rwkv_prep(u_r, shift0, lw, bsz, True, l_real)
    if l_real < l_pad:
        cut = lambda t: t.reshape(bsz, l_pad, -1)[:, :l_real].reshape(bsz * l_real, -1)
        o, s_new = _rwkv_core([cut(t) for t in (r, w, k, v, a)], wkv0, lw, bsz)
        o = jnp.pad(o.reshape(bsz, l_real, -1), ((0, 0), (0, l_pad - l_real), (0, 0))).reshape(bsz * l_pad, -1)
    else:
        o, s_new = _rwkv_core([r, w, k, v, a], wkv0, lw, bsz)
    x2, x2b = _layer_tail(alpha, x, xb, y_ssm, o, g, u_gate, lw)
    return x2, x2b, h_new.reshape(ssm0.shape), conv_new, s_new, shift_new


def _step_layer(alpha, x, xb, ssm_all, conv_all, layer, h_acc, wkv0, shift0, lw):
    bsz = x.shape[0]
    u_z, u_xbc, u_dtx, u_r, u_gate = _in_proj(xb, lw, dt_expanded=True)
    y_ssm, h_acc, conv_new = _ssd_step(u_z, u_xbc, u_dtx, ssm_all, conv_all, layer, lw, h_acc)
    (r, w, k, v, a, g), shift_new = _rwkv_prep(u_r, shift0, lw, bsz, False)
    o, s_new = _rwkv_core([r, w, k, v, a], wkv0, lw, bsz)
    x2, x2b = _layer_tail(alpha, x, xb, y_ssm, o, g, u_gate, lw)
    return x2, x2b, h_acc, conv_new, s_new, shift_new


def kernel(x_prompt, x_sample, state_ssm, state_conv, state_wkv, state_shift, meta_tokens, w_in, conv_w, conv_b,
           dt_bias, a_log, d_skip, ssm_norm_w, p_ssm, shift_mu, w0, w_lora_up, a0, a_lora_up, g_lora_up, k_k,
           k_a, r_k, lnx_g, lnx_b, p_rwkv, w_out, ln1_g, ln1_b, w_ffn_in, w_ffn_out, ln2_g, ln2_b):
    weights = {
        'w_in': w_in, 'conv_w': conv_w, 'conv_b': conv_b, 'dt_bias': dt_bias, 'a_log': a_log,
        'd_skip': d_skip, 'ssm_norm_w': ssm_norm_w, 'p_ssm': p_ssm, 'shift_mu': shift_mu,
        'w0': w0, 'w_lora_up': w_lora_up, 'a0': a0, 'a_lora_up': a_lora_up, 'g_lora_up': g_lora_up,
        'k_k': k_k, 'k_a': k_a, 'r_k': r_k, 'lnx_g': lnx_g, 'lnx_b': lnx_b, 'p_rwkv': p_rwkv,
        'w_out': w_out, 'ln1_g': ln1_g, 'ln1_b': ln1_b, 'w_ffn_in': w_ffn_in,
        'w_ffn_out': w_ffn_out, 'ln2_g': ln2_g, 'ln2_b': ln2_b,
    }
    depth = w_in.shape[0]
    alpha = (2 * depth) ** 0.25
    bsz, seq, d = x_prompt.shape
    n_meta = meta_tokens.shape[0]
    layers = [_prep_layer_weights(weights, l) for l in range(depth)]

    xm = jnp.pad(meta_tokens.astype(F32), ((0, SSD_CHUNK - n_meta), (0, 0)))
    xp = x_prompt.reshape(bsz * seq, d)
    xs = x_sample.reshape(x_sample.shape[0], d)
    xmb, xpb, xsb = xm.astype(BF16), xp.astype(BF16), xs.astype(BF16)
    bcast = lambda t: jnp.broadcast_to(t, (bsz,) + t.shape[1:])
    ssm_p, conv_p, wkv_p, shift_p, conv_s, wkv_s, shift_s = [], [], [], [], [], [], []
    ssm_all = state_ssm.reshape(state_ssm.shape[:2] + (-1, SSM_STATE))
    ssm_s = None
    for l in range(depth):
        lw = layers[l]
        z = lambda a: jnp.zeros((1,) + a.shape[2:], F32)
        xm, xmb, hm, cm, sm, shm = _seq_layer(alpha, xm, xmb, z(state_ssm), z(state_conv), z(state_wkv),
                                              z(state_shift), lw, 1, n_meta)
        xp, xpb, h, c, s, sh = _seq_layer(alpha, xp, xpb, bcast(hm), bcast(cm), bcast(sm), bcast(shm), lw,
                                          bsz, seq)
        ssm_p.append(h), conv_p.append(c), wkv_p.append(s), shift_p.append(sh)
        xs, xsb, ssm_s, c, s, sh = _step_layer(alpha, xs, xsb, ssm_all, state_conv, l, ssm_s, state_wkv[l],
                                               state_shift[l], lw)
        conv_s.append(c), wkv_s.append(s), shift_s.append(sh)
    st = jnp.stack
    return (xp.reshape(bsz, seq, d), xs.reshape(x_sample.shape), st(ssm_p), st(conv_p), st(wkv_p), st(shift_p),
            ssm_s.reshape(state_ssm.shape), st(conv_s), st(wkv_s), st(shift_s))
```

```python
import functools
import math

import jax
import jax.numpy as jnp
from jax import lax
from jax.experimental import pallas as pl
from jax.experimental.pallas import tpu as pltpu

F32 = jnp.float32
BF16 = jnp.bfloat16

LANES = 128
N_META = 16
HEAD_DIM = 64
SSM_STATE = 128
SSM_GROUPS = 4
CONV_K = 4
SSD_CHUNK = 128
CONV_PITCH = 3
SCAN_UNROLL = 16
LN_EPS = 1e-5
RMS_EPS = 1e-5
GN_EPS = 64e-5
VMEM_LIMIT_BYTES = 56 * 1024 * 1024


def _cparams(*sem):
    return pltpu.CompilerParams(dimension_semantics=sem, vmem_limit_bytes=VMEM_LIMIT_BYTES)


def _pick_tile(n, cap):
    if n <= cap:
        return n
    best = LANES
    for t in range(LANES, cap + 1, LANES):
        if n % t == 0:
            best = t
    return best


def _silu(x):
    return x * jax.nn.sigmoid(x)


def _softplus(x):
    return jnp.maximum(x, 0.0) + jnp.log(1.0 + jnp.exp(-jnp.abs(x)))


def _layer_norm(y, g, b):
    mu = jnp.mean(y, axis=-1, keepdims=True)
    d = y - mu
    var = jnp.mean(d * d, axis=-1, keepdims=True)
    return d * lax.rsqrt(var + LN_EPS) * g + b


def _mm_kernel(x_ref, w_ref, o_ref, *wb_scr):
    if wb_scr:
        @pl.when(pl.program_id(1) == 0)
        def _():
            wb_scr[0][...] = w_ref[...].astype(BF16)

        w = wb_scr[0][...]
    else:
        w = w_ref[...]
    o_ref[...] = jnp.dot(x_ref[...], w, preferred_element_type=F32).astype(o_ref.dtype)


def _matmul(x, w, out_dtype=F32, layer=None, col0=0, n=None):
    m, k = x.shape
    n = w.shape[-1] if n is None else n
    tm = _pick_tile(m, 1024)
    tn = _pick_tile(n, 1792)
    assert col0 % LANES == 0 and n % LANES == 0
    while col0 % tn or n % tn:
        tn -= LANES
    jb = col0 // tn
    if layer is None:
        w_spec = pl.BlockSpec((k, tn), lambda j, i: (0, jb + j))
    else:
        w_spec = pl.BlockSpec((None, k, tn), lambda j, i: (layer, 0, jb + j))
    scratch = [pltpu.VMEM((k, tn), BF16)] if w.dtype == F32 else []
    return pl.pallas_call(
        _mm_kernel,
        grid=(n // tn, m // tm),
        in_specs=[pl.BlockSpec((tm, k), lambda j, i: (i, 0)), w_spec],
        out_specs=pl.BlockSpec((tm, tn), lambda j, i: (i, j)),
        out_shape=jax.ShapeDtypeStruct((m, n), out_dtype),
        scratch_shapes=scratch,
        compiler_params=_cparams("parallel", "arbitrary"),
        name="matmul",
    )(x, w)


def _merge_kernel(alpha, ys_ref, yr_ref, g_ref, ug_ref, x_ref, ps_ref, pr_ref, wo_ref, lg_ref, lb_ref,
                  o_ref, ob_ref):
    d = o_ref.shape[1]
    a = jnp.dot(ys_ref[...], ps_ref[...], preferred_element_type=F32)
    yr = (yr_ref[...] * g_ref[...]).astype(BF16)
    b = jnp.dot(yr, pr_ref[...], preferred_element_type=F32)
    gates = jax.nn.sigmoid(ug_ref[...])
    merged = gates[:, :d] * a + gates[:, d:] * b
    y = alpha * x_ref[...] + jnp.dot(merged.astype(BF16), wo_ref[...], preferred_element_type=F32)
    out = _layer_norm(y, lg_ref[...], lb_ref[...])
    o_ref[...] = out
    ob_ref[...] = out.astype(BF16)


def _merge_out_ln(alpha, y_ssm, y_rwkv, g_rwkv, u_gate, x, p_ssm, p_rwkv, w_out, ln_g, ln_b):
    m, d = x.shape
    tm = _pick_tile(m, 512)
    row = lambda c: pl.BlockSpec((tm, c), lambda i: (i, 0))
    full = lambda a: pl.BlockSpec(a.shape, lambda i: (0, 0))
    return pl.pallas_call(
        functools.partial(_merge_kernel, alpha),
        grid=(m // tm,),
        in_specs=[row(y_ssm.shape[1]), row(d), row(d), row(2 * d), row(d),
                  full(p_ssm), full(p_rwkv), full(w_out), full(ln_g), full(ln_b)],
        out_specs=[row(d), row(d)],
        out_shape=[jax.ShapeDtypeStruct((m, d), F32), jax.ShapeDtypeStruct((m, d), BF16)],
        compiler_params=_cparams("parallel"),
        name="merge_out_ln",
    )(y_ssm, y_rwkv, g_rwkv, u_gate, x, p_ssm, p_rwkv, w_out, ln_g, ln_b)


def _swiglu_kernel(x_ref, wg_ref, wu_ref, o_ref, wgb_scr, wub_scr):
    @pl.when(pl.program_id(1) == 0)
    def _():
        wgb_scr[...] = wg_ref[...].astype(BF16)
        wub_scr[...] = wu_ref[...].astype(BF16)

    x = x_ref[...]
    hg = jnp.dot(x, wgb_scr[...], preferred_element_type=F32)
    hu = jnp.dot(x, wub_scr[...], preferred_element_type=F32)
    o_ref[...] = (_silu(hg) * hu).astype(o_ref.dtype)


def _ffn_in(x, w_ffn_in_all, layer):
    m, k = x.shape
    dff = w_ffn_in_all.shape[2] // 2
    tm = _pick_tile(m, 512)
    tn = _pick_tile(dff, 1408)
    nj = dff // tn
    return pl.pallas_call(
        _swiglu_kernel,
        grid=(nj, m // tm),
        in_specs=[pl.BlockSpec((tm, k), lambda j, i: (i, 0)),
                  pl.BlockSpec((None, k, tn), lambda j, i: (layer, 0, j)),
                  pl.BlockSpec((None, k, tn), lambda j, i: (layer, 0, j + nj))],
        out_specs=pl.BlockSpec((tm, tn), lambda j, i: (i, j)),
        out_shape=jax.ShapeDtypeStruct((m, dff), BF16),
        scratch_shapes=[pltpu.VMEM((k, tn), BF16)] * 2,
        compiler_params=_cparams("parallel", "arbitrary"),
        name="ffn_in_swiglu",
    )(x, w_ffn_in_all, w_ffn_in_all)


def _ffn_out_kernel(alpha, h_ref, w_ref, x_ref, lg_ref, lb_ref, o_ref, ob_ref):
    y = alpha * x_ref[...] + jnp.dot(h_ref[...], w_ref[...], preferred_element_type=F32)
    out = _layer_norm(y, lg_ref[...], lb_ref[...])
    o_ref[...] = out
    ob_ref[...] = out.astype(BF16)


def _ffn_out_ln(alpha, h, w_ffn_out, x, ln_g, ln_b):
    m, d = x.shape
    k = h.shape[1]
    tm = _pick_tile(m, 512)
    row = lambda c: pl.BlockSpec((tm, c), lambda i: (i, 0))
    full = lambda a: pl.BlockSpec(a.shape, lambda i: (0, 0))
    return pl.pallas_call(
        functools.partial(_ffn_out_kernel, alpha),
        grid=(m // tm,),
        in_specs=[row(k), full(w_ffn_out), row(d), full(ln_g), full(ln_b)],
        out_specs=[row(d), row(d)],
        out_shape=[jax.ShapeDtypeStruct((m, d), F32), jax.ShapeDtypeStruct((m, d), BF16)],
        compiler_params=_cparams("parallel"),
        name="ffn_out_ln",
    )(h, w_ffn_out, x, ln_g, ln_b)


def _ssd_chunk_kernel(l_real, nchunks,
                      uzx_ref, udt_ref, h0_ref, cpre_ref, convw_ref, convb_ref, dtb_ref, alog_ref,
                      dskip_ref, normw_ref,
                      y_ref, hout_ref, ctail_ref,
                      ht_scr, cbuf_scr, xbc_scr, y_scr):
    q = SSD_CHUNK
    d_inner = y_ref.shape[1]
    gw = d_inner // SSM_GROUPS
    c = pl.program_id(1)

    @pl.when(c == 0)
    def _():
        for kb in range(d_inner // LANES):
            ht_scr[:, kb * LANES:(kb + 1) * LANES] = h0_ref[0, kb * LANES:(kb + 1) * LANES, :].T
        for ct in range(cbuf_scr.shape[0]):
            cbuf_scr[ct, pl.ds(5 * CONV_PITCH, CONV_K - 1, stride=CONV_PITCH), :] = \
                cpre_ref[0, :, ct * LANES:(ct + 1) * LANES]

    rows_at = lambda r0, n: pl.ds(r0 * CONV_PITCH, n, stride=CONV_PITCH)
    for ct in range(cbuf_scr.shape[0]):
        cols = slice(ct * LANES, (ct + 1) * LANES)
        cbuf_scr[ct, rows_at(8, q), :] = uzx_ref[:, d_inner + ct * LANES:d_inner + (ct + 1) * LANES]
        acc = convb_ref[:, cols] + cbuf_scr[ct, rows_at(5, q), :] * convw_ref[0:1, cols]
        for k in range(1, CONV_K):
            acc = acc + cbuf_scr[ct, rows_at(5 + k, q), :] * convw_ref[k:k + 1, cols]
        xbc_scr[:, cols] = _silu(acc)
        tail = cbuf_scr[ct, rows_at(l_real + 5, CONV_K - 1), :]
        cbuf_scr[ct, rows_at(5, CONV_K - 1), :] = tail

    rows = lax.broadcasted_iota(jnp.int32, (q, LANES), 0)
    dt = _softplus(udt_ref[...] + dtb_ref[...])
    if l_real < q:
        dt = jnp.where(rows < l_real, dt, 0.0)
    da = dt * (-jnp.exp(alog_ref[...]))
    acum = da
    s = 1
    while s < q:
        acum = acum + jnp.where(rows >= s, pltpu.roll(acum, s, axis=0), 0.0)
        s *= 2
    acum_t = acum.T
    dt_t = dt.T
    a_last = acum[q - 1:q, :]
    st_t = dt_t * jnp.exp(acum_t[:, q - 1:q] - acum_t)
    ii = lax.broadcasted_iota(jnp.int32, (q, q), 0)
    jj = lax.broadcasted_iota(jnp.int32, (q, q), 1)
    causal = ii >= jj
    low = lax.broadcasted_iota(jnp.int32, (q, LANES), 1) < HEAD_DIM

    for g in range(SSM_GROUPS):
        bm = xbc_scr[:, d_inner + g * SSM_STATE:d_inner + (g + 1) * SSM_STATE]
        cm = xbc_scr[:, d_inner + (SSM_GROUPS + g) * SSM_STATE:d_inner + (SSM_GROUPS + g + 1) * SSM_STATE]
        cmb = cm.astype(BF16)
        cb = lax.dot_general(cmb, bm.astype(BF16), (((1,), (1,)), ((), ())), preferred_element_type=F32)
        bm_t = bm.T
        for pr in range(gw // LANES):
            lanes = slice(g * gw + pr * LANES, g * gw + (pr + 1) * LANES)
            hd0 = (g * gw + pr * LANES) // HEAD_DIM
            x_pair = xbc_scr[:, lanes]
            xb = x_pair.astype(BF16)
            yo = jnp.dot(cmb, ht_scr[:, lanes].astype(BF16), preferred_element_type=F32)
            yd, st, ea, cd = [], [], [], []
            for hd in (hd0, hd0 + 1):
                a_col = jnp.broadcast_to(acum[:, hd:hd + 1], (q, q))
                lmat = jnp.exp(jnp.where(causal, a_col - acum_t[hd:hd + 1, :], -jnp.inf))
                wd = (cb * lmat * dt_t[hd:hd + 1, :]).astype(BF16)
                yd.append(jnp.dot(wd, xb, preferred_element_type=F32))
                ea.append(jnp.exp(a_col))
                st.append(jnp.dot((bm_t * st_t[hd:hd + 1, :]).astype(BF16), xb, preferred_element_type=F32))
                cd.append(jnp.broadcast_to(jnp.exp(a_last[:, hd:hd + 1]), (SSM_STATE, LANES)))
            y_pair = jnp.where(low, yd[0] + yo * ea[0], yd[1] + yo * ea[1]) + x_pair * dskip_ref[:, lanes]
            y_scr[:, lanes] = y_pair
            ht_scr[:, lanes] = ht_scr[:, lanes] * jnp.where(low, cd[0], cd[1]) + jnp.where(low, st[0], st[1])

    for g in range(SSM_GROUPS):
        cols = slice(g * gw, (g + 1) * gw)
        yg = y_scr[:, cols] * _silu(uzx_ref[:, cols])
        ms = jnp.mean(yg * yg, axis=-1, keepdims=True)
        y_ref[:, cols] = (yg * lax.rsqrt(ms + RMS_EPS) * normw_ref[:, cols]).astype(y_ref.dtype)

    @pl.when(c == nchunks - 1)
    def _():
        for kb in range(d_inner // LANES):
            hout_ref[0, kb * LANES:(kb + 1) * LANES, :] = ht_scr[:, kb * LANES:(kb + 1) * LANES].T
        for ct in range(cbuf_scr.shape[0]):
            ctail_ref[0, :, ct * LANES:(ct + 1) * LANES] = \
                cbuf_scr[ct, pl.ds(5 * CONV_PITCH, CONV_K - 1, stride=CONV_PITCH), :]


def _ssd_chunked(u_zx, u_dt, h0, conv_pre, lw, bsz, l_real):
    m = u_zx.shape[0]
    conv_dim = conv_pre.shape[2]
    d_inner = u_zx.shape[1] - conv_dim
    q = SSD_CHUNK
    nchunks = m // bsz // q
    row = lambda cdim: pl.BlockSpec((q, cdim), lambda b, c: (b * nchunks + c, 0))
    full = lambda a: pl.BlockSpec(a.shape, lambda b, c: (0, 0))
    per_b = lambda a: pl.BlockSpec((1,) + a.shape[1:], lambda b, c: (b, 0, 0))
    hshape = jax.ShapeDtypeStruct(h0.shape, F32)
    cshape = jax.ShapeDtypeStruct(conv_pre.shape, F32)
    return pl.pallas_call(
        functools.partial(_ssd_chunk_kernel, min(l_real, q), nchunks),
        grid=(bsz, nchunks),
        in_specs=[row(d_inner + conv_dim), row(LANES), per_b(h0), per_b(conv_pre),
                  full(lw['conv_w']), full(lw['conv_b']), full(lw['dt_bias']), full(lw['a_log']),
                  full(lw['d_skip']), full(lw['ssm_norm_w'])],
        out_specs=[row(d_inner), per_b(h0), per_b(conv_pre)],
        out_shape=[jax.ShapeDtypeStruct((m, d_inner), BF16), hshape, cshape],
        scratch_shapes=[pltpu.VMEM((SSM_STATE, d_inner), F32),
                        pltpu.VMEM((conv_dim // LANES, (q + 8) * CONV_PITCH, LANES), F32),
                        pltpu.VMEM((q, conv_dim), F32),
                        pltpu.VMEM((q, d_inner), F32)],
        compiler_params=_cparams("parallel", "arbitrary"),
        name="ssd_chunk",
    )(u_zx, u_dt, h0, conv_pre, lw['conv_w'], lw['conv_b'], lw['dt_bias'], lw['a_log'],
      lw['d_skip'], lw['ssm_norm_w'])


def _row_to_col(row):
    r_i = lax.broadcasted_iota(jnp.int32, (LANES, LANES), 0)
    c_i = lax.broadcasted_iota(jnp.int32, (LANES, LANES), 1)
    return jnp.sum(jnp.where(r_i == c_i, jnp.broadcast_to(row, (LANES, LANES)), 0.0), axis=1, keepdims=True)


def _ssd_step_kernel(aliased, uz_ref, uxbc_ref, udtx_ref, h0_ref, cpre_ref, convw_ref, convb_ref, dtbx_ref,
                     alogx_ref, dskip_ref, normw_ref, *rest):
    y_ref, hout_ref, ctail_ref, ht_scr = rest[1:] if aliased else rest
    d_inner = uz_ref.shape[2]
    gw = d_inner // SSM_GROUPS
    u = uxbc_ref[0]
    pre = cpre_ref[0, 0]
    acc = convb_ref[...] + u * convw_ref[CONV_K - 1:CONV_K, :]
    for k in range(CONV_K - 1):
        acc = acc + pre[k:k + 1, :] * convw_ref[k:k + 1, :]
    xbc = _silu(acc)
    ctail_ref[0, 0:CONV_K - 2, :] = pre[1:, :]
    ctail_ref[0, CONV_K - 2:CONV_K - 1, :] = u
    dt = _softplus(udtx_ref[0] + dtbx_ref[...])
    dec = jnp.exp(dt * (-jnp.exp(alogx_ref[...])))
    x_row = xbc[:, :d_inner]
    xdt = x_row * dt
    for kb in range(d_inner // LANES):
        ht_scr[:, kb * LANES:(kb + 1) * LANES] = h0_ref[0, 0, kb * LANES:(kb + 1) * LANES, :].T
    y_parts = []
    for g in range(SSM_GROUPS):
        cols = slice(g * gw, (g + 1) * gw)
        bm = xbc[:, d_inner + g * SSM_STATE:d_inner + (g + 1) * SSM_STATE]
        cm = xbc[:, d_inner + (SSM_GROUPS + g) * SSM_STATE:d_inner + (SSM_GROUPS + g + 1) * SSM_STATE]
        hn = ht_scr[:, cols] * dec[:, cols] + _row_to_col(bm) * xdt[:, cols]
        ht_scr[:, cols] = hn
        y_parts.append(jnp.sum(hn * _row_to_col(cm), axis=0, keepdims=True))
    y_row = jnp.concatenate(y_parts, axis=1) + x_row * dskip_ref[...]
    y_row = y_row * _silu(uz_ref[0])
    outs = []
    for g in range(SSM_GROUPS):
        yg = y_row[:, g * gw:(g + 1) * gw]
        ms = jnp.mean(yg * yg, axis=-1, keepdims=True)
        outs.append(yg * lax.rsqrt(ms + RMS_EPS))
    y_ref[0] = (jnp.concatenate(outs, axis=1) * normw_ref[...]).astype(y_ref.dtype)
    for kb in range(d_inner // LANES):
        hout_ref[0, 0, kb * LANES:(kb + 1) * LANES, :] = ht_scr[:, kb * LANES:(kb + 1) * LANES].T


def _ssd_step(u_z, u_xbc, u_dtx, ssm_all, conv_all, layer, lw, h_acc):
    bsz, d_inner = u_z.shape
    r3 = lambda a: a.reshape(bsz, 1, a.shape[1])
    per_b = lambda a: pl.BlockSpec((1,) + a.shape[1:], lambda b: (b, 0, 0))
    per_lb = lambda a: pl.BlockSpec((1, 1) + a.shape[2:], lambda b: (layer, b, 0, 0))
    full = lambda a: pl.BlockSpec(a.shape, lambda b: (0, 0))
    uz3, ux3, ud3 = r3(u_z), r3(u_xbc), r3(u_dtx)
    cshape = conv_all.shape[1:]
    args = [uz3, ux3, ud3, ssm_all, conv_all, lw['conv_w'], lw['conv_b'], lw['dt_bias_x'], lw['a_log_x'],
            lw['d_skip'], lw['ssm_norm_w']]
    in_specs = [per_b(uz3), per_b(ux3), per_b(ud3), per_lb(ssm_all), per_lb(conv_all)]
    in_specs += [full(a) for a in args[5:]]
    aliases = {}
    if h_acc is not None:
        args.append(h_acc)
        in_specs.append(pl.BlockSpec(memory_space=pl.ANY))
        aliases = {len(args) - 1: 1}
    y, h, ct = pl.pallas_call(
        functools.partial(_ssd_step_kernel, h_acc is not None),
        grid=(bsz,),
        in_specs=in_specs,
        out_specs=[per_b(uz3), per_lb(ssm_all), pl.BlockSpec((1,) + cshape[1:], lambda b: (b, 0, 0))],
        out_shape=[jax.ShapeDtypeStruct((bsz, 1, d_inner), BF16), jax.ShapeDtypeStruct(ssm_all.shape, F32),
                   jax.ShapeDtypeStruct(cshape, F32)],
        scratch_shapes=[pltpu.VMEM((SSM_STATE, d_inner), F32)],
        input_output_aliases=aliases,
        compiler_params=_cparams("arbitrary"),
        name="ssd_step",
    )(*args)
    return y.reshape(bsz, d_inner), h, ct


def _rwkv_prep_kernel(seq_mode, l_real, ur_ref, prev_ref, mu_ref, w0_ref, a0_ref, lora_ref, gup_ref,
                      r_ref, w_ref, k_ref, v_ref, a_ref, g_ref, last_ref, carry_scr):
    t, cols = ur_ref.shape
    d = r_ref.shape[1]
    u = ur_ref[...]
    if seq_mode:
        c = pl.program_id(1)

        @pl.when(c == 0)
        def _():
            carry_scr[...] = jnp.broadcast_to(prev_ref[0], carry_scr.shape)

        rows = lax.broadcasted_iota(jnp.int32, (t, cols), 0)
        prev = jnp.where(rows == 0, jnp.broadcast_to(carry_scr[0:1, :], (t, cols)), pltpu.roll(u, 1, axis=0))
        lrow = (l_real - 1) % t
        carry_scr[...] = jnp.broadcast_to(u[lrow:lrow + 1, :], carry_scr.shape)
        last_ref[0] = u[lrow:lrow + 1, :]
    else:
        prev = prev_ref[...]
        last_ref[...] = u
    sh = u + (prev - u) * mu_ref[...]
    r_ref[...] = sh[:, 0:d]
    k_ref[...] = sh[:, d:2 * d]
    v_ref[...] = sh[:, 2 * d:3 * d]
    la = sh[:, 3 * d:3 * d + LANES]
    lane = lax.broadcasted_iota(jnp.int32, la.shape, 1)
    la = jnp.where(lane < LANES // 2, jnp.tanh(la), la).astype(BF16)
    pre = jnp.dot(la, lora_ref[...], preferred_element_type=F32)
    w_log = -_softplus(-(w0_ref[...] + pre[:, :d])) - 0.5
    w_ref[...] = jnp.exp(-jnp.exp(w_log))
    a_ref[...] = jax.nn.sigmoid(a0_ref[...] + pre[:, d:])
    gl = jax.nn.sigmoid(sh[:, 3 * d + LANES:]).astype(BF16)
    g_ref[...] = jnp.dot(gl, gup_ref[...], preferred_element_type=F32)


def _rwkv_prep(u_r, shift_prev, lw, bsz, seq_mode, l_real=None):
    m, cols = u_r.shape
    d = lw['w0'].shape[1]
    full = lambda a: pl.BlockSpec(a.shape, lambda *_: (0,) * a.ndim)
    outs = [jax.ShapeDtypeStruct((m, d), F32)] * 6
    args = (lw['shift_mu'], lw['w0'], lw['a0'], lw['lora_up'], lw['g_lora_up'])
    if seq_mode:
        seq = m // bsz
        t = _pick_tile(seq, 256)
        nblk = seq // t
        l_real = seq if l_real is None else l_real
        row = lambda cdim: pl.BlockSpec((t, cdim), lambda b, c: (b * nblk + c, 0))
        prev3 = shift_prev.reshape(bsz, 1, cols)
        per_b = pl.BlockSpec((1, 1, cols), lambda b, c: (b, 0, 0))
        res = pl.pallas_call(
            functools.partial(_rwkv_prep_kernel, True, l_real),
            grid=(bsz, nblk),
            in_specs=[row(cols), per_b] + [full(a) for a in args],
            out_specs=[row(d)] * 6 + [per_b],
            out_shape=outs + [jax.ShapeDtypeStruct((bsz, 1, cols), F32)],
            scratch_shapes=[pltpu.VMEM((8, cols), F32)],
            compiler_params=_cparams("parallel", "arbitrary"),
            name="rwkv_prep_seq",
        )(u_r, prev3, *args)
        return res[:6], res[6].reshape(bsz, cols)
    t = _pick_tile(m, 256)
    row = lambda cdim: pl.BlockSpec((t, cdim), lambda i: (i, 0))
    res = pl.pallas_call(
        functools.partial(_rwkv_prep_kernel, False, None),
        grid=(m // t,),
        in_specs=[row(cols), row(cols)] + [full(a) for a in args],
        out_specs=[row(d)] * 6 + [row(cols)],
        out_shape=outs + [jax.ShapeDtypeStruct((m, cols), F32)],
        scratch_shapes=[pltpu.VMEM((8, cols), F32)],
        compiler_params=_cparams("parallel"),
        name="rwkv_prep_step",
    )(u_r, shift_prev, *args)
    return res[:6], res[6]


def _rwkv_scan_kernel(tb, r_ref, w_ref, k_ref, v_ref, a_ref, s0_ref, kk_ref, ka_ref, rk_ref, lg_ref, lb_ref,
                      o_ref, sout_ref, s_scr, g_scr, sa_scr, nkk_scr, b_scr, km_scr, rt_scr, bon_scr):
    hd = HEAD_DIM
    blk = pl.program_id(1)
    pad_rows = o_ref.shape[1] - hd

    @pl.when(blk == 0)
    def _():
        s_scr[...] = s0_ref[...]

    def prep(t, slot):
        k = k_ref[t, 0:hd, :]
        a = a_ref[t, 0:hd, :]
        r = r_ref[t, 0:hd, :]
        kk = k * kk_ref[...]
        kk = kk * lax.rsqrt(jnp.maximum(jnp.sum(kk * kk, axis=0, keepdims=True), 1e-24))
        g_prev = g_scr[...]
        g = g_prev * w_ref[t, 0:hd, :]
        g_scr[...] = g
        ginv = 1.0 / g
        nkk_scr[slot] = -(kk * g_prev)
        b_scr[slot] = kk * a * ginv
        km = k * (1.0 + (a - 1.0) * ka_ref[...])
        km_scr[slot] = km * ginv
        rt_scr[slot] = r * g
        bon_scr[slot] = jnp.broadcast_to(jnp.sum(r * km * rk_ref[...], axis=0, keepdims=True), (8, LANES))

    def step(t, slot, has_next):
        nslot = 1 - slot
        if has_next:
            prep(t + 1, nslot)
        v = v_ref[t, 0:hd, :]
        sa = sa_scr[...]
        zero = jnp.zeros((hd, LANES), F32)

        def jbody(j, carry):
            o, sa_next = carry
            sj = s_scr[j] + sa * b_scr[slot, pl.ds(j, 1), :] + v * km_scr[slot, pl.ds(j, 1), :]
            s_scr[j] = sj
            o = o + sj * rt_scr[slot, pl.ds(j, 1), :]
            if has_next:
                sa_next = sa_next + sj * nkk_scr[nslot, pl.ds(j, 1), :]
            return o, sa_next

        o, sa_next = lax.fori_loop(0, hd, jbody, (zero, zero), unroll=SCAN_UNROLL)
        if has_next:
            sa_scr[...] = sa_next
        mu = jnp.mean(o, axis=0, keepdims=True)
        dlt = o - mu
        var = jnp.mean(dlt * dlt, axis=0, keepdims=True)
        on = dlt * lax.rsqrt(var + GN_EPS) * lg_ref[...] + lb_ref[...]
        o_ref[t, 0:hd, :] = on + bon_scr[slot, 0:1, :] * v
        if pad_rows:
            o_ref[t, hd:hd + pad_rows, :] = jnp.zeros((pad_rows, LANES), F32)

    g_scr[...] = jnp.ones((hd, LANES), F32)
    prep(0, 0)
    sa0 = s_scr[0] * nkk_scr[0, 0:1, :]
    for j in range(1, hd):
        sa0 = sa0 + s_scr[j] * nkk_scr[0, j:j + 1, :]
    sa_scr[...] = sa0

    if tb >= 2:
        def pair(p, carry):
            step(2 * p, 0, True)
            step(2 * p + 1, 1, True)
            return carry

        lax.fori_loop(0, tb // 2 - 1, pair, 0)
        step(tb - 2, 0, True)
        step(tb - 1, 1, False)
    else:
        step(0, 0, False)

    for j in range(hd):
        s_scr[j] = s_scr[j] * g_scr[j:j + 1, :]

    @pl.when(blk == pl.num_programs(1) - 1)
    def _():
        sout_ref[...] = s_scr[...]


def _rwkv_scan(r, w, k, v, a, s0, lw_c):
    t_total, rows, c = r.shape
    hd = HEAD_DIM
    tb = 32 if t_total % 32 == 0 else t_total
    assert tb == 1 or tb % 2 == 0
    nb = t_total // tb
    ncl = c // LANES
    seq = pl.BlockSpec((tb, rows, LANES), lambda cl, i: (i, 0, cl))
    st = pl.BlockSpec((hd, hd, LANES), lambda cl, i: (0, 0, cl))
    par = pl.BlockSpec((hd, LANES), lambda cl, i: (0, cl))
    return pl.pallas_call(
        functools.partial(_rwkv_scan_kernel, tb),
        grid=(ncl, nb),
        in_specs=[seq] * 5 + [st] + [par] * 5,
        out_specs=[seq, st],
        out_shape=[jax.ShapeDtypeStruct((t_total, rows, c), F32), jax.ShapeDtypeStruct((hd, hd, c), F32)],
        scratch_shapes=[pltpu.VMEM((hd, hd, LANES), F32)] + [pltpu.VMEM((hd, LANES), F32)] * 2
        + [pltpu.VMEM((2, hd, LANES), F32)] * 4 + [pltpu.VMEM((2, 8, LANES), F32)],
        compiler_params=_cparams("parallel", "arbitrary"),
        name="rwkv_scan",
    )(r, w, k, v, a, s0, lw_c['k_k'], lw_c['k_a'], lw_c['r_k'], lw_c['lnx_g'], lw_c['lnx_b'])


def _to_chain(x, bsz, heads):
    t = x.shape[0] // bsz
    return x.reshape(bsz, t, heads, HEAD_DIM).transpose(1, 3, 0, 2).reshape(t, HEAD_DIM, bsz * heads)


def _from_chain(x, bsz, heads):
    t = x.shape[0]
    return x.reshape(t, HEAD_DIM, bsz, heads).transpose(2, 0, 3, 1).reshape(bsz * t, heads * HEAD_DIM)


CHAIN_PITCH = HEAD_DIM + 8


def _chain_from_u_kernel(kind, u_ref, prev_ref, mu_ref, *rest):
    if kind == 'lerp':
        o_ref, y_scr, carry_scr = rest
    else:
        p0_ref, lora_ref, o_ref, y_scr, carry_scr = rest
    nb, t, wdt = u_ref.shape

    @pl.when(pl.program_id(0) == 0)
    def _():
        carry_scr[...] = prev_ref[:, 0, :]

    rows = lax.broadcasted_iota(jnp.int32, (t, wdt), 0)
    for b in range(nb):
        u = u_ref[b]
        prev = jnp.where(rows == 0, jnp.broadcast_to(carry_scr[b:b + 1, :], (t, wdt)), pltpu.roll(u, 1, axis=0))
        carry_scr[b:b + 1, :] = u[t - 1:t, :]
        sh = u + (prev - u) * mu_ref[...]
        if kind == 'lerp':
            x = sh
        else:
            lane = lax.broadcasted_iota(jnp.int32, sh.shape, 1)
            la = jnp.where(lane < LANES // 2, jnp.tanh(sh), sh).astype(BF16)
            pre = p0_ref[...] + jnp.dot(la, lora_ref[...], preferred_element_type=F32)
            if kind == 'decay':
                x = jnp.exp(-jnp.exp(-_softplus(-pre) - 0.5))
            else:
                x = jax.nn.sigmoid(pre)
        nk = x.shape[1] // LANES
        for k in range(nk):
            tt = x[:, k * LANES:(k + 1) * LANES].T
            m0 = b * 2 * nk + 2 * k
            y_scr[m0 * CHAIN_PITCH:m0 * CHAIN_PITCH + HEAD_DIM, :] = tt[:HEAD_DIM, :]
            y_scr[(m0 + 1) * CHAIN_PITCH:(m0 + 1) * CHAIN_PITCH + HEAD_DIM, :] = tt[HEAD_DIM:, :]
    for j in range(HEAD_DIM):
        g = y_scr[pl.ds(j, LANES, stride=CHAIN_PITCH), :]
        o_ref[pl.ds(j, LANES, stride=CHAIN_PITCH), :] = g.T
    for j in range(HEAD_DIM, CHAIN_PITCH):
        o_ref[pl.ds(j, LANES, stride=CHAIN_PITCH), :] = jnp.zeros((LANES, LANES), F32)


def _chain_from_u(kind, u_r, shift_prev, lw, bsz):
    m, cols = u_r.shape
    t_total = m // bsz
    d = lw['w0'].shape[1]
    if kind in ('r', 'k', 'v'):
        width, cb, kern_kind, extra = d, 'rkv'.index(kind), 'lerp', []
    else:
        width, cb, kern_kind = LANES, 3 * d // LANES, kind
        p0 = lw['w0'] if kind == 'decay' else lw['a0']
        half = 0 if kind == 'decay' else 1
        extra = [(p0, pl.BlockSpec((1, d), lambda i: (0, 0))),
                 (lw['lora_up'], pl.BlockSpec((LANES, d), lambda i: (0, half)))]
    args = [u_r.reshape(bsz, t_total, cols), shift_prev.reshape(bsz, 1, cols), lw['shift_mu']] + [a for a, _ in extra]
    in_specs = [pl.BlockSpec((bsz, LANES, width), lambda i: (0, i, cb)),
                pl.BlockSpec((bsz, 1, width), lambda i: (0, 0, cb)),
                pl.BlockSpec((1, width), lambda i: (0, cb))] + [s for _, s in extra]
    out = pl.pallas_call(
        functools.partial(_chain_from_u_kernel, kern_kind),
        grid=(t_total // LANES,),
        in_specs=in_specs,
        out_specs=pl.BlockSpec((LANES * CHAIN_PITCH, LANES), lambda i: (i, 0)),
        out_shape=jax.ShapeDtypeStruct((t_total * CHAIN_PITCH, LANES), F32),
        scratch_shapes=[pltpu.VMEM((LANES * CHAIN_PITCH, LANES), F32), pltpu.VMEM((bsz, width), F32)],
        compiler_params=_cparams("arbitrary"),
        name="chain_" + kern_kind,
    )(*args)
    return out.reshape(t_total, CHAIN_PITCH, LANES)


def _gate_kernel(u_ref, prev_ref, mu_ref, gup_ref, g_ref, carry_scr):
    t, wdt = u_ref.shape

    @pl.when(pl.program_id(1) == 0)
    def _():
        carry_scr[...] = jnp.broadcast_to(prev_ref[0], carry_scr.shape)

    u = u_ref[...]
    rows = lax.broadcasted_iota(jnp.int32, (t, wdt), 0)
    prev = jnp.where(rows == 0, jnp.broadcast_to(carry_scr[0:1, :], (t, wdt)), pltpu.roll(u, 1, axis=0))
    carry_scr[...] = jnp.broadcast_to(u[t - 1:t, :], carry_scr.shape)
    sh = u + (prev - u) * mu_ref[...]
    gl = jax.nn.sigmoid(sh[:, wdt - LANES:]).astype(BF16)
    g_ref[...] = jnp.dot(gl, gup_ref[...], preferred_element_type=F32)


def _rwkv_gate(u_r, shift_prev, lw, bsz):
    m, cols = u_r.shape
    d = lw['w0'].shape[1]
    seq = m // bsz
    t = _pick_tile(seq, 512)
    nblk = seq // t
    wdt = 2 * LANES
    cb = cols // wdt - 1
    assert cols % wdt == 0
    return pl.pallas_call(
        _gate_kernel,
        grid=(bsz, nblk),
        in_specs=[pl.BlockSpec((t, wdt), lambda b, c: (b * nblk + c, cb)),
                  pl.BlockSpec((1, 1, wdt), lambda b, c: (b, 0, cb)),
                  pl.BlockSpec((1, wdt), lambda b, c: (0, cb)),
                  pl.BlockSpec(lw['g_lora_up'].shape, lambda b, c: (0, 0))],
        out_specs=pl.BlockSpec((t, d), lambda b, c: (b * nblk + c, 0)),
        out_shape=jax.ShapeDtypeStruct((m, d), F32),
        scratch_shapes=[pltpu.VMEM((8, wdt), F32)],
        compiler_params=_cparams("parallel", "arbitrary"),
        name="rwkv_gate",
    )(u_r, shift_prev.reshape(bsz, 1, cols), lw['shift_mu'], lw['g_lora_up'])


def _from_chain_kernel(x_ref, o_ref, y_scr):
    nb = o_ref.shape[0]
    nk = o_ref.shape[2] // LANES
    for i in range(HEAD_DIM):
        g = x_ref[pl.ds(i, LANES, stride=CHAIN_PITCH), :]
        y_scr[pl.ds(i, LANES, stride=CHAIN_PITCH), :] = g.T
    for b in range(nb):
        for k in range(nk):
            m0 = b * 2 * nk + 2 * k
            tt = jnp.concatenate([y_scr[m0 * CHAIN_PITCH:m0 * CHAIN_PITCH + HEAD_DIM, :],
                                  y_scr[(m0 + 1) * CHAIN_PITCH:(m0 + 1) * CHAIN_PITCH + HEAD_DIM, :]], axis=0)
            o_ref[b, :, k * LANES:(k + 1) * LANES] = tt.T


def _from_chain_pallas(x, bsz, d):
    t_total = x.shape[0]
    out = pl.pallas_call(
        _from_chain_kernel,
        grid=(t_total // LANES,),
        in_specs=[pl.BlockSpec((LANES * CHAIN_PITCH, LANES), lambda i: (i, 0))],
        out_specs=pl.BlockSpec((bsz, LANES, d), lambda i: (0, i, 0)),
        out_shape=jax.ShapeDtypeStruct((bsz, t_total, d), F32),
        scratch_shapes=[pltpu.VMEM((LANES * CHAIN_PITCH, LANES), F32)],
        compiler_params=_cparams("parallel"),
        name="from_chain",
    )(x.reshape(t_total * CHAIN_PITCH, LANES))
    return out.reshape(bsz * t_total, d)


def _chain_param(p, bsz, heads):
    return jnp.tile(p.reshape(heads, HEAD_DIM).T, (1, bsz))


def _pad_lanes(x, c_pad):
    c = x.shape[-1]
    if c == c_pad:
        return x
    return jnp.pad(x, [(0, 0)] * (x.ndim - 1) + [(0, c_pad - c)])


def _rwkv_core(rwkva, s0, lw, bsz, chain_ready=False):
    d = lw['w0'].shape[1]
    heads = d // HEAD_DIM
    c = bsz * heads
    c_pad = -(-c // LANES) * LANES
    if chain_ready:
        chain = rwkva
    else:
        chain = [_pad_lanes(_to_chain(x, bsz, heads), c_pad) for x in rwkva]
    s0c = _pad_lanes(s0.transpose(3, 2, 0, 1).reshape(HEAD_DIM, HEAD_DIM, c), c_pad)
    lw_c = {n: _pad_lanes(_chain_param(lw[n], bsz, heads), c_pad) for n in ('k_k', 'k_a', 'r_k', 'lnx_g', 'lnx_b')}
    o, s = _rwkv_scan(*chain, s0c, lw_c)
    if chain_ready:
        o = _from_chain_pallas(o, bsz, d)
    else:
        o = _from_chain(o[..., :c], bsz, heads)
    s = s[..., :c].reshape(HEAD_DIM, HEAD_DIM, bsz, heads).transpose(2, 3, 1, 0)
    return o, s


def _prep_layer_weights(w, l):
    d = w['w_in'].shape[1]
    d_inner = w['p_ssm'].shape[1]
    heads_ssm = w['dt_bias'].shape[1]
    conv_dim = w['conv_w'].shape[2]
    rcols = w['shift_mu'].shape[1]
    dl = w['w_lora_up'].shape[1]
    s0, s1, s2 = d_inner, d_inner + conv_dim, d_inner + conv_dim + heads_ssm
    s3 = s2 + rcols
    w_in = w['w_in'][l]
    pad_h = lambda v, fill: jnp.pad(v, (0, LANES - heads_ssm), constant_values=fill).reshape(1, LANES)
    row = lambda v: v.reshape(1, -1)
    lora = jnp.zeros((LANES, 2 * d), F32)
    lora = lora.at[:dl, :d].set(w['w_lora_up'][l]).at[dl:dl + w['a_lora_up'].shape[1], d:].set(w['a_lora_up'][l])
    return {
        'w_in_all': w['w_in'], 'layer': l, 'zxd_cols': ((0, s1), (s1, LANES)), 'd_inner': d_inner,
        'w_ffn_in_all': w['w_ffn_in'],
        'w_dt_x': jnp.repeat(w_in[:, s1:s2], HEAD_DIM, axis=1).astype(BF16),
        'dt_bias_x': row(jnp.repeat(w['dt_bias'][l], HEAD_DIM)),
        'a_log_x': row(jnp.repeat(w['a_log'][l], HEAD_DIM)),
        'w_r': w_in[:, s2:s3], 'w_gate': w_in[:, s3:],
        'conv_w': w['conv_w'][l], 'conv_b': row(w['conv_b'][l]),
        'dt_bias': pad_h(w['dt_bias'][l], 0.0), 'a_log': pad_h(w['a_log'][l], 0.0),
        'd_skip': row(jnp.repeat(w['d_skip'][l], HEAD_DIM)),
        'ssm_norm_w': row(w['ssm_norm_w'][l]),
        'p_ssm': w['p_ssm'][l].astype(BF16),
        'shift_mu': row(w['shift_mu'][l]), 'w0': row(w['w0'][l]), 'a0': row(w['a0'][l]),
        'lora_up': lora.astype(BF16), 'g_lora_up': w['g_lora_up'][l].astype(BF16),
        'k_k': w['k_k'][l], 'k_a': w['k_a'][l], 'r_k': w['r_k'][l].reshape(-1),
        'lnx_g': w['lnx_g'][l], 'lnx_b': w['lnx_b'][l],
        'p_rwkv': w['p_rwkv'][l].astype(BF16), 'w_out': w['w_out'][l].astype(BF16),
        'ln1_g': row(w['ln1_g'][l]), 'ln1_b': row(w['ln1_b'][l]),
        'w_ffn_out': w['w_ffn_out'][l].astype(BF16),
        'ln2_g': row(w['ln2_g'][l]), 'ln2_b': row(w['ln2_b'][l]),
    }


def _layer_tail(alpha, x, xb, y_ssm, o_rwkv, g_rwkv, u_gate, lw):
    del xb
    x1, x1b = _merge_out_ln(alpha, y_ssm, o_rwkv, g_rwkv, u_gate, x, lw['p_ssm'], lw['p_rwkv'], lw['w_out'],
                            lw['ln1_g'], lw['ln1_b'])
    hmid = _ffn_in(x1b, lw['w_ffn_in_all'], lw['layer'])
    return _ffn_out_ln(alpha, hmid, lw['w_ffn_out'], x1, lw['ln2_g'], lw['ln2_b'])


def _in_proj(xb, lw, dt_expanded=False):
    (c0, n0), (c1, n1) = lw['zxd_cols']
    zx = _matmul(xb, lw['w_in_all'], layer=lw['layer'], col0=c0, n=n0)
    if dt_expanded:
        dt = _matmul(xb, lw['w_dt_x'])
    else:
        dt = _matmul(xb, lw['w_in_all'], layer=lw['layer'], col0=c1, n=n1)
    return zx, dt, _matmul(xb, lw['w_r']), _matmul(xb, lw['w_gate'])


def _seq_layer(alpha, x, xb, ssm0, conv0, wkv0, shift0, lw, bsz, l_real):
    l_pad = x.shape[0] // bsz
    u_zx, u_dt, u_r, u_gate = _in_proj(xb, lw)
    h0 = ssm0.reshape(bsz, -1, SSM_STATE)
    y_ssm, h_new, conv_new = _ssd_chunked(u_zx, u_dt, h0, conv0, lw, bsz, l_real)
    heads_r = lw['w0'].shape[1] // HEAD_DIM
    if bsz * heads_r == LANES and l_real == l_pad:
        chain = [_chain_from_u(kind, u_r, shift0, lw, bsz) for kind in ('r', 'decay', 'k', 'v', 'rate')]
        g = _rwkv_gate(u_r, shift0, lw, bsz)
        shift_new = u_r.reshape(bsz, l_pad, -1)[:, -1]
        o, s_new = _rwkv_core(chain, wkv0, lw, bsz, chain_ready=True)
        x2, x2b = _layer_tail(alpha, x, xb, y_ssm, o, g, u_gate, lw)
        return x2, x2b, h_new.reshape(ssm0.shape), conv_new, s_new, shift_new
    (r, w, k, v, a, g), shift_new = _rwkv_prep(u_r, shift0, lw, bsz, True, l_real)
    if l_real < l_pad:
        cut = lambda t: t.reshape(bsz, l_pad, -1)[:, :l_real].reshape(bsz * l_real, -1)
        o, s_new = _rwkv_core([cut(t) for t in (r, w, k, v, a)], wkv0, lw, bsz)
        o = jnp.pad(o.reshape(bsz, l_real, -1), ((0, 0), (0, l_pad - l_real), (0, 0))).reshape(bsz * l_pad, -1)
    else:
        o, s_new = _rwkv_core([r, w, k, v, a], wkv0, lw, bsz)
    x2, x2b = _layer_tail(alpha, x, xb, y_ssm, o, g, u_gate, lw)
    return x2, x2b, h_new.reshape(ssm0.shape), conv_new, s_new, shift_new


def _step_layer(alpha, x, xb, ssm_all, conv_all, layer, h_acc, wkv0, shift0, lw):
    bsz = x.shape[0]
    u_zx, u_dtx, u_r, u_gate = _in_proj(xb, lw, dt_expanded=True)
    u_z, u_xbc = u_zx[:, :lw['d_inner']], u_zx[:, lw['d_inner']:]
    y_ssm, h_acc, conv_new = _ssd_step(u_z, u_xbc, u_dtx, ssm_all, conv_all, layer, lw, h_acc)
    (r, w, k, v, a, g), shift_new = _rwkv_prep(u_r, shift0, lw, bsz, False)
    o, s_new = _rwkv_core([r, w, k, v, a], wkv0, lw, bsz)
    x2, x2b = _layer_tail(alpha, x, xb, y_ssm, o, g, u_gate, lw)
    return x2, x2b, h_acc, conv_new, s_new, shift_new


def kernel(x_prompt, x_sample, state_ssm, state_conv, state_wkv, state_shift, meta_tokens, w_in, conv_w, conv_b,
           dt_bias, a_log, d_skip, ssm_norm_w, p_ssm, shift_mu, w0, w_lora_up, a0, a_lora_up, g_lora_up, k_k,
           k_a, r_k, lnx_g, lnx_b, p_rwkv, w_out, ln1_g, ln1_b, w_ffn_in, w_ffn_out, ln2_g, ln2_b):
    weights = {
        'w_in': w_in, 'conv_w': conv_w, 'conv_b': conv_b, 'dt_bias': dt_bias, 'a_log': a_log,
        'd_skip': d_skip, 'ssm_norm_w': ssm_norm_w, 'p_ssm': p_ssm, 'shift_mu': shift_mu,
        'w0': w0, 'w_lora_up': w_lora_up, 'a0': a0, 'a_lora_up': a_lora_up, 'g_lora_up': g_lora_up,
        'k_k': k_k, 'k_a': k_a, 'r_k': r_k, 'lnx_g': lnx_g, 'lnx_b': lnx_b, 'p_rwkv': p_rwkv,
        'w_out': w_out, 'ln1_g': ln1_g, 'ln1_b': ln1_b, 'w_ffn_in': w_ffn_in,
        'w_ffn_out': w_ffn_out, 'ln2_g': ln2_g, 'ln2_b': ln2_b,
    }
    depth = w_in.shape[0]
    alpha = (2 * depth) ** 0.25
    bsz, seq, d = x_prompt.shape
    n_meta = meta_tokens.shape[0]
    layers = [_prep_layer_weights(weights, l) for l in range(depth)]

    xm = jnp.pad(meta_tokens.astype(F32), ((0, SSD_CHUNK - n_meta), (0, 0)))
    xp = x_prompt.reshape(bsz * seq, d)
    xs = x_sample.reshape(x_sample.shape[0], d)
    xmb, xpb, xsb = xm.astype(BF16), xp.astype(BF16), xs.astype(BF16)
    bcast = lambda t: jnp.broadcast_to(t, (bsz,) + t.shape[1:])
    ssm_p, conv_p, wkv_p, shift_p, conv_s, wkv_s, shift_s = [], [], [], [], [], [], []
    ssm_all = state_ssm.reshape(state_ssm.shape[:2] + (-1, SSM_STATE))
    ssm_s = None
    for l in range(depth):
        lw = layers[l]
        z = lambda a: jnp.zeros((1,) + a.shape[2:], F32)
        xm, xmb, hm, cm, sm, shm = _seq_layer(alpha, xm, xmb, z(state_ssm), z(state_conv), z(state_wkv),
                                              z(state_shift), lw, 1, n_meta)
        xp, xpb, h, c, s, sh = _seq_layer(alpha, xp, xpb, bcast(hm), bcast(cm), bcast(sm), bcast(shm), lw,
                                          bsz, seq)
        ssm_p.append(h), conv_p.append(c), wkv_p.append(s), shift_p.append(sh)
        xs, xsb, ssm_s, c, s, sh = _step_layer(alpha, xs, xsb, ssm_all, state_conv, l, ssm_s, state_wkv[l],
                                               state_shift[l], lw)
        conv_s.append(c), wkv_s.append(s), shift_s.append(sh)
    st = jnp.stack
    return (xp.reshape(bsz, seq, d), xs.reshape(x_sample.shape), st(ssm_p), st(conv_p), st(wkv_p), st(shift_p),
            ssm_s.reshape(state_ssm.shape), st(conv_s), st(wkv_s), st(shift_s))
```

```python
import functools
import math

import jax
import jax.numpy as jnp
from jax import lax
from jax.experimental import pallas as pl
from jax.experimental.pallas import tpu as pltpu

F32 = jnp.float32
BF16 = jnp.bfloat16

LANES = 128
N_META = 16
HEAD_DIM = 64
SSM_STATE = 128
SSM_GROUPS = 4
CONV_K = 4
SSD_CHUNK = 128
CONV_PITCH = 3
SCAN_UNROLL = 32
LOG2E = 1.4426950408889634
LN_EPS = 1e-5
RMS_EPS = 1e-5
GN_EPS = 64e-5
VMEM_LIMIT_BYTES = 56 * 1024 * 1024


def _cparams(*sem):
    return pltpu.CompilerParams(dimension_semantics=sem, vmem_limit_bytes=VMEM_LIMIT_BYTES)


def _pick_tile(n, cap):
    if n <= cap:
        return n
    best = LANES
    for t in range(LANES, cap + 1, LANES):
        if n % t == 0:
            best = t
    return best


def _silu(x):
    return x * jax.nn.sigmoid(x)


def _softplus(x):
    return jnp.maximum(x, 0.0) + jnp.log(1.0 + jnp.exp(-jnp.abs(x)))


def _layer_norm(y, g, b):
    mu = jnp.mean(y, axis=-1, keepdims=True)
    d = y - mu
    var = jnp.mean(d * d, axis=-1, keepdims=True)
    return d * lax.rsqrt(var + LN_EPS) * g + b


def _mm_kernel(x_ref, w_ref, o_ref):
    o_ref[...] = jnp.dot(x_ref[...], w_ref[...], preferred_element_type=F32).astype(o_ref.dtype)


def _matmul(x, w, out_dtype=F32):
    m, k = x.shape
    n = w.shape[1]
    tm = _pick_tile(m, 1024)
    tn = _pick_tile(n, 1792)
    return pl.pallas_call(
        _mm_kernel,
        grid=(n // tn, m // tm),
        in_specs=[pl.BlockSpec((tm, k), lambda j, i: (i, 0)),
                  pl.BlockSpec((k, tn), lambda j, i: (0, j))],
        out_specs=pl.BlockSpec((tm, tn), lambda j, i: (i, j)),
        out_shape=jax.ShapeDtypeStruct((m, n), out_dtype),
        compiler_params=_cparams("parallel", "parallel"),
        name="matmul",
    )(x, w)


def _merge_kernel(alpha, ys_ref, yr_ref, g_ref, ug_ref, x_ref, ps_ref, pr_ref, wo_ref, lg_ref, lb_ref,
                  o_ref, ob_ref):
    d = o_ref.shape[1]
    a = jnp.dot(ys_ref[...], ps_ref[...], preferred_element_type=F32)
    yr = (yr_ref[...] * g_ref[...]).astype(BF16)
    b = jnp.dot(yr, pr_ref[...], preferred_element_type=F32)
    gates = jax.nn.sigmoid(ug_ref[...])
    merged = gates[:, :d] * a + gates[:, d:] * b
    y = alpha * x_ref[...] + jnp.dot(merged.astype(BF16), wo_ref[...], preferred_element_type=F32)
    out = _layer_norm(y, lg_ref[...], lb_ref[...])
    o_ref[...] = out
    ob_ref[...] = out.astype(BF16)


def _merge_out_ln(alpha, y_ssm, y_rwkv, g_rwkv, u_gate, x, p_ssm, p_rwkv, w_out, ln_g, ln_b):
    m, d = x.shape
    tm = _pick_tile(m, 512)
    row = lambda c: pl.BlockSpec((tm, c), lambda i: (i, 0))
    full = lambda a: pl.BlockSpec(a.shape, lambda i: (0, 0))
    return pl.pallas_call(
        functools.partial(_merge_kernel, alpha),
        grid=(m // tm,),
        in_specs=[row(y_ssm.shape[1]), row(d), row(d), row(2 * d), row(d),
                  full(p_ssm), full(p_rwkv), full(w_out), full(ln_g), full(ln_b)],
        out_specs=[row(d), row(d)],
        out_shape=[jax.ShapeDtypeStruct((m, d), F32), jax.ShapeDtypeStruct((m, d), BF16)],
        compiler_params=_cparams("parallel"),
        name="merge_out_ln",
    )(y_ssm, y_rwkv, g_rwkv, u_gate, x, p_ssm, p_rwkv, w_out, ln_g, ln_b)


def _swiglu_kernel(x_ref, wg_ref, wu_ref, o_ref, wgb_scr, wub_scr):
    @pl.when(pl.program_id(1) == 0)
    def _():
        wgb_scr[...] = wg_ref[...].astype(BF16)
        wub_scr[...] = wu_ref[...].astype(BF16)

    x = x_ref[...]
    hg = jnp.dot(x, wgb_scr[...], preferred_element_type=F32)
    hu = jnp.dot(x, wub_scr[...], preferred_element_type=F32)
    o_ref[...] = (_silu(hg) * hu).astype(o_ref.dtype)


def _ffn_in(x, w_ffn_in_all, layer):
    m, k = x.shape
    dff = w_ffn_in_all.shape[2] // 2
    tm = _pick_tile(m, 512)
    tn = _pick_tile(dff, 1408)
    nj = dff // tn
    return pl.pallas_call(
        _swiglu_kernel,
        grid=(nj, m // tm),
        in_specs=[pl.BlockSpec((tm, k), lambda j, i: (i, 0)),
                  pl.BlockSpec((None, k, tn), lambda j, i: (layer, 0, j)),
                  pl.BlockSpec((None, k, tn), lambda j, i: (layer, 0, j + nj))],
        out_specs=pl.BlockSpec((tm, tn), lambda j, i: (i, j)),
        out_shape=jax.ShapeDtypeStruct((m, dff), BF16),
        scratch_shapes=[pltpu.VMEM((k, tn), BF16)] * 2,
        compiler_params=_cparams("parallel", "arbitrary"),
        name="ffn_in_swiglu",
    )(x, w_ffn_in_all, w_ffn_in_all)


def _ffn_out_kernel(alpha, h_ref, w_ref, x_ref, lg_ref, lb_ref, o_ref, ob_ref):
    y = alpha * x_ref[...] + jnp.dot(h_ref[...], w_ref[...], preferred_element_type=F32)
    out = _layer_norm(y, lg_ref[...], lb_ref[...])
    o_ref[...] = out
    ob_ref[...] = out.astype(BF16)


def _ffn_out_ln(alpha, h, w_ffn_out, x, ln_g, ln_b):
    m, d = x.shape
    k = h.shape[1]
    tm = _pick_tile(m, 512)
    row = lambda c: pl.BlockSpec((tm, c), lambda i: (i, 0))
    full = lambda a: pl.BlockSpec(a.shape, lambda i: (0, 0))
    return pl.pallas_call(
        functools.partial(_ffn_out_kernel, alpha),
        grid=(m // tm,),
        in_specs=[row(k), full(w_ffn_out), row(d), full(ln_g), full(ln_b)],
        out_specs=[row(d), row(d)],
        out_shape=[jax.ShapeDtypeStruct((m, d), F32), jax.ShapeDtypeStruct((m, d), BF16)],
        compiler_params=_cparams("parallel"),
        name="ffn_out_ln",
    )(h, w_ffn_out, x, ln_g, ln_b)


def _ssd_chunk_kernel(l_real, nchunks,
                      uzx_ref, udt_ref, h0_ref, cpre_ref, convw_ref, convb_ref, dtb_ref, alog_ref,
                      dskip_ref, normw_ref,
                      y_ref, hout_ref, ctail_ref,
                      ht_scr, cbuf_scr, xbc_scr, y_scr):
    q = SSD_CHUNK
    d_inner = y_ref.shape[1]
    gw = d_inner // SSM_GROUPS
    c = pl.program_id(1)

    @pl.when(c == 0)
    def _():
        for kb in range(d_inner // LANES):
            ht_scr[:, kb * LANES:(kb + 1) * LANES] = h0_ref[0, kb * LANES:(kb + 1) * LANES, :].T
        for ct in range(cbuf_scr.shape[0]):
            cbuf_scr[ct, pl.ds(5 * CONV_PITCH, CONV_K - 1, stride=CONV_PITCH), :] = \
                cpre_ref[0, :, ct * LANES:(ct + 1) * LANES]

    rows_at = lambda r0, n: pl.ds(r0 * CONV_PITCH, n, stride=CONV_PITCH)
    for ct in range(cbuf_scr.shape[0]):
        cols = slice(ct * LANES, (ct + 1) * LANES)
        cbuf_scr[ct, rows_at(8, q), :] = uzx_ref[:, d_inner + ct * LANES:d_inner + (ct + 1) * LANES]
        acc = convb_ref[:, cols] + cbuf_scr[ct, rows_at(5, q), :] * convw_ref[0:1, cols]
        for k in range(1, CONV_K):
            acc = acc + cbuf_scr[ct, rows_at(5 + k, q), :] * convw_ref[k:k + 1, cols]
        xbc_scr[:, cols] = _silu(acc)
        tail = cbuf_scr[ct, rows_at(l_real + 5, CONV_K - 1), :]
        cbuf_scr[ct, rows_at(5, CONV_K - 1), :] = tail

    rows = lax.broadcasted_iota(jnp.int32, (q, LANES), 0)
    dt = _softplus(udt_ref[...] + dtb_ref[...])
    if l_real < q:
        dt = jnp.where(rows < l_real, dt, 0.0)
    da = dt * (-jnp.exp(alog_ref[...]))
    acum = da
    s = 1
    while s < q:
        acum = acum + jnp.where(rows >= s, pltpu.roll(acum, s, axis=0), 0.0)
        s *= 2
    acum = acum * LOG2E
    acum_t = acum.T
    dt_t = dt.T
    a_last = acum[q - 1:q, :]
    st_t = dt_t * jnp.exp2(acum_t[:, q - 1:q] - acum_t)
    ii = lax.broadcasted_iota(jnp.int32, (q, q), 0)
    jj = lax.broadcasted_iota(jnp.int32, (q, q), 1)
    causal = ii >= jj
    low = lax.broadcasted_iota(jnp.int32, (q, LANES), 1) < HEAD_DIM

    for g in range(SSM_GROUPS):
        bm = xbc_scr[:, d_inner + g * SSM_STATE:d_inner + (g + 1) * SSM_STATE]
        cm = xbc_scr[:, d_inner + (SSM_GROUPS + g) * SSM_STATE:d_inner + (SSM_GROUPS + g + 1) * SSM_STATE]
        cmb = cm.astype(BF16)
        cb = lax.dot_general(cmb, bm.astype(BF16), (((1,), (1,)), ((), ())), preferred_element_type=F32)
        bm_t = bm.T
        for pr in range(gw // LANES):
            lanes = slice(g * gw + pr * LANES, g * gw + (pr + 1) * LANES)
            hd0 = (g * gw + pr * LANES) // HEAD_DIM
            x_pair = xbc_scr[:, lanes]
            xb = x_pair.astype(BF16)
            yo = jnp.dot(cmb, ht_scr[:, lanes].astype(BF16), preferred_element_type=F32)
            yd, st, acol, alast = [], [], [], []
            for hd in (hd0, hd0 + 1):
                a_col = jnp.broadcast_to(acum[:, hd:hd + 1], (q, q))
                lmat = jnp.exp2(jnp.where(causal, a_col - acum_t[hd:hd + 1, :], -jnp.inf))
                wd = (cb * lmat * dt_t[hd:hd + 1, :]).astype(BF16)
                yd.append(jnp.dot(wd, xb, preferred_element_type=F32))
                acol.append(a_col)
                st.append(jnp.dot((bm_t * st_t[hd:hd + 1, :]).astype(BF16), xb, preferred_element_type=F32))
                alast.append(jnp.broadcast_to(a_last[:, hd:hd + 1], (SSM_STATE, LANES)))
            ea = jnp.exp2(jnp.where(low, acol[0], acol[1]))
            y_scr[:, lanes] = jnp.where(low, yd[0], yd[1]) + yo * ea + x_pair * dskip_ref[:, lanes]
            cd = jnp.exp2(jnp.where(low, alast[0], alast[1]))
            ht_scr[:, lanes] = ht_scr[:, lanes] * cd + jnp.where(low, st[0], st[1])

    for g in range(SSM_GROUPS):
        cols = slice(g * gw, (g + 1) * gw)
        yg = y_scr[:, cols] * _silu(uzx_ref[:, cols])
        ms = jnp.mean(yg * yg, axis=-1, keepdims=True)
        y_ref[:, cols] = (yg * lax.rsqrt(ms + RMS_EPS) * normw_ref[:, cols]).astype(y_ref.dtype)

    @pl.when(c == nchunks - 1)
    def _():
        for kb in range(d_inner // LANES):
            hout_ref[0, kb * LANES:(kb + 1) * LANES, :] = ht_scr[:, kb * LANES:(kb + 1) * LANES].T
        for ct in range(cbuf_scr.shape[0]):
            ctail_ref[0, :, ct * LANES:(ct + 1) * LANES] = \
                cbuf_scr[ct, pl.ds(5 * CONV_PITCH, CONV_K - 1, stride=CONV_PITCH), :]


def _ssd_chunked(u_zx, u_dt, h0, conv_pre, lw, bsz, l_real):
    m = u_zx.shape[0]
    conv_dim = conv_pre.shape[2]
    d_inner = u_zx.shape[1] - conv_dim
    q = SSD_CHUNK
    nchunks = m // bsz // q
    row = lambda cdim: pl.BlockSpec((q, cdim), lambda b, c: (b * nchunks + c, 0))
    full = lambda a: pl.BlockSpec(a.shape, lambda b, c: (0, 0))
    per_b = lambda a: pl.BlockSpec((1,) + a.shape[1:], lambda b, c: (b, 0, 0))
    hshape = jax.ShapeDtypeStruct(h0.shape, F32)
    cshape = jax.ShapeDtypeStruct(conv_pre.shape, F32)
    return pl.pallas_call(
        functools.partial(_ssd_chunk_kernel, min(l_real, q), nchunks),
        grid=(bsz, nchunks),
        in_specs=[row(d_inner + conv_dim), row(LANES), per_b(h0), per_b(conv_pre),
                  full(lw['conv_w']), full(lw['conv_b']), full(lw['dt_bias']), full(lw['a_log']),
                  full(lw['d_skip']), full(lw['ssm_norm_w'])],
        out_specs=[row(d_inner), per_b(h0), per_b(conv_pre)],
        out_shape=[jax.ShapeDtypeStruct((m, d_inner), BF16), hshape, cshape],
        scratch_shapes=[pltpu.VMEM((SSM_STATE, d_inner), F32),
                        pltpu.VMEM((conv_dim // LANES, (q + 8) * CONV_PITCH, LANES), F32),
                        pltpu.VMEM((q, conv_dim), F32),
                        pltpu.VMEM((q, d_inner), F32)],
        compiler_params=_cparams("parallel", "arbitrary"),
        name="ssd_chunk",
    )(u_zx, u_dt, h0, conv_pre, lw['conv_w'], lw['conv_b'], lw['dt_bias'], lw['a_log'],
      lw['d_skip'], lw['ssm_norm_w'])


def _row_to_col(row):
    r_i = lax.broadcasted_iota(jnp.int32, (LANES, LANES), 0)
    c_i = lax.broadcasted_iota(jnp.int32, (LANES, LANES), 1)
    return jnp.sum(jnp.where(r_i == c_i, jnp.broadcast_to(row, (LANES, LANES)), 0.0), axis=1, keepdims=True)


def _ssd_step_kernel(aliased, layer, uz_ref, uxbc_ref, udtx_ref, h0_ref, cpre_ref, convw_ref, convb_ref, dtbx_ref,
                     alogx_ref, dskip_ref, normw_ref, *rest):
    y_ref, hout_ref, ctail_ref, ht_scr = rest[1:] if aliased else rest
    d_inner = uz_ref.shape[2]
    gw = d_inner // SSM_GROUPS
    u = uxbc_ref[0]
    pre = cpre_ref[0, 0]
    acc = convb_ref[...] + u * convw_ref[CONV_K - 1:CONV_K, :]
    for k in range(CONV_K - 1):
        acc = acc + pre[k:k + 1, :] * convw_ref[k:k + 1, :]
    xbc = _silu(acc)
    ctail_ref[0, 0:CONV_K - 2, :] = pre[1:, :]
    ctail_ref[0, CONV_K - 2:CONV_K - 1, :] = u
    dt = _softplus(udtx_ref[0] + dtbx_ref[...])
    dec = jnp.exp(dt * (-jnp.exp(alogx_ref[...])))
    x_row = xbc[:, :d_inner]
    xdt = x_row * dt
    for kb in range(d_inner // LANES):
        ht_scr[:, kb * LANES:(kb + 1) * LANES] = h0_ref[0, 0, kb * LANES:(kb + 1) * LANES, :].T
    y_parts = []
    for g in range(SSM_GROUPS):
        cols = slice(g * gw, (g + 1) * gw)
        bm = xbc[:, d_inner + g * SSM_STATE:d_inner + (g + 1) * SSM_STATE]
        cm = xbc[:, d_inner + (SSM_GROUPS + g) * SSM_STATE:d_inner + (SSM_GROUPS + g + 1) * SSM_STATE]
        hn = ht_scr[:, cols] * dec[:, cols] + _row_to_col(bm) * xdt[:, cols]
        ht_scr[:, cols] = hn
        y_parts.append(jnp.sum(hn * _row_to_col(cm), axis=0, keepdims=True))
    y_row = jnp.concatenate(y_parts, axis=1) + x_row * dskip_ref[...]
    y_row = y_row * _silu(uz_ref[0])
    outs = []
    for g in range(SSM_GROUPS):
        yg = y_row[:, g * gw:(g + 1) * gw]
        ms = jnp.mean(yg * yg, axis=-1, keepdims=True)
        outs.append(yg * lax.rsqrt(ms + RMS_EPS))
    y_ref[0] = (jnp.concatenate(outs, axis=1) * normw_ref[...]).astype(y_ref.dtype)
    slot = 0 if aliased else layer
    for kb in range(d_inner // LANES):
        hout_ref[slot, 0, kb * LANES:(kb + 1) * LANES, :] = ht_scr[:, kb * LANES:(kb + 1) * LANES].T
    if not aliased:
        for other in range(hout_ref.shape[0]):
            if other != layer:
                hout_ref[other, 0] = jnp.zeros(hout_ref.shape[2:], F32)


def _ssd_step(u_z, u_xbc, u_dtx, ssm_all, conv_all, layer, lw, h_acc):
    bsz, d_inner = u_z.shape
    r3 = lambda a: a.reshape(bsz, 1, a.shape[1])
    per_b = lambda a: pl.BlockSpec((1,) + a.shape[1:], lambda b: (b, 0, 0))
    per_lb = lambda a: pl.BlockSpec((1, 1) + a.shape[2:], lambda b: (layer, b, 0, 0))
    full = lambda a: pl.BlockSpec(a.shape, lambda b: (0, 0))
    uz3, ux3, ud3 = r3(u_z), r3(u_xbc), r3(u_dtx)
    cshape = conv_all.shape[1:]
    args = [uz3, ux3, ud3, ssm_all, conv_all, lw['conv_w'], lw['conv_b'], lw['dt_bias_x'], lw['a_log_x'],
            lw['d_skip'], lw['ssm_norm_w']]
    in_specs = [per_b(uz3), per_b(ux3), per_b(ud3), per_lb(ssm_all), per_lb(conv_all)]
    in_specs += [full(a) for a in args[5:]]
    aliases = {}
    h_spec = pl.BlockSpec((ssm_all.shape[0], 1) + ssm_all.shape[2:], lambda b: (0, b, 0, 0))
    if h_acc is not None:
        args.append(h_acc)
        in_specs.append(pl.BlockSpec(memory_space=pl.ANY))
        aliases = {len(args) - 1: 1}
        h_spec = per_lb(ssm_all)
    y, h, ct = pl.pallas_call(
        functools.partial(_ssd_step_kernel, h_acc is not None, layer),
        grid=(bsz,),
        in_specs=in_specs,
        out_specs=[per_b(uz3), h_spec, pl.BlockSpec((1,) + cshape[1:], lambda b: (b, 0, 0))],
        out_shape=[jax.ShapeDtypeStruct((bsz, 1, d_inner), BF16), jax.ShapeDtypeStruct(ssm_all.shape, F32),
                   jax.ShapeDtypeStruct(cshape, F32)],
        scratch_shapes=[pltpu.VMEM((SSM_STATE, d_inner), F32)],
        input_output_aliases=aliases,
        compiler_params=_cparams("arbitrary"),
        name="ssd_step",
    )(*args)
    return y.reshape(bsz, d_inner), h, ct


def _rwkv_prep_kernel(seq_mode, l_real, ur_ref, prev_ref, mu_ref, w0_ref, a0_ref, lora_ref, gup_ref,
                      r_ref, w_ref, k_ref, v_ref, a_ref, g_ref, last_ref, carry_scr):
    t, cols = ur_ref.shape
    d = r_ref.shape[1]
    u = ur_ref[...]
    if seq_mode:
        c = pl.program_id(1)

        @pl.when(c == 0)
        def _():
            carry_scr[...] = jnp.broadcast_to(prev_ref[0], carry_scr.shape)

        rows = lax.broadcasted_iota(jnp.int32, (t, cols), 0)
        prev = jnp.where(rows == 0, jnp.broadcast_to(carry_scr[0:1, :], (t, cols)), pltpu.roll(u, 1, axis=0))
        lrow = (l_real - 1) % t
        carry_scr[...] = jnp.broadcast_to(u[lrow:lrow + 1, :], carry_scr.shape)
        last_ref[0] = u[lrow:lrow + 1, :]
    else:
        prev = prev_ref[...]
        last_ref[...] = u
    sh = u + (prev - u) * mu_ref[...]
    r_ref[...] = sh[:, 0:d]
    k_ref[...] = sh[:, d:2 * d]
    v_ref[...] = sh[:, 2 * d:3 * d]
    la = sh[:, 3 * d:3 * d + LANES]
    lane = lax.broadcasted_iota(jnp.int32, la.shape, 1)
    la = jnp.where(lane < LANES // 2, jnp.tanh(la), la).astype(BF16)
    pre = jnp.dot(la, lora_ref[...], preferred_element_type=F32)
    w_log = -_softplus(-(w0_ref[...] + pre[:, :d])) - 0.5
    w_ref[...] = jnp.exp(-jnp.exp(w_log))
    a_ref[...] = jax.nn.sigmoid(a0_ref[...] + pre[:, d:])
    gl = jax.nn.sigmoid(sh[:, 3 * d + LANES:]).astype(BF16)
    g_ref[...] = jnp.dot(gl, gup_ref[...], preferred_element_type=F32)


def _rwkv_prep(u_r, shift_prev, lw, bsz, seq_mode, l_real=None):
    m, cols = u_r.shape
    d = lw['w0'].shape[1]
    full = lambda a: pl.BlockSpec(a.shape, lambda *_: (0,) * a.ndim)
    outs = [jax.ShapeDtypeStruct((m, d), F32)] * 6
    args = (lw['shift_mu'], lw['w0'], lw['a0'], lw['lora_up'], lw['g_lora_up'])
    if seq_mode:
        seq = m // bsz
        t = _pick_tile(seq, 256)
        nblk = seq // t
        l_real = seq if l_real is None else l_real
        row = lambda cdim: pl.BlockSpec((t, cdim), lambda b, c: (b * nblk + c, 0))
        prev3 = shift_prev.reshape(bsz, 1, cols)
        per_b = pl.BlockSpec((1, 1, cols), lambda b, c: (b, 0, 0))
        res = pl.pallas_call(
            functools.partial(_rwkv_prep_kernel, True, l_real),
            grid=(bsz, nblk),
            in_specs=[row(cols), per_b] + [full(a) for a in args],
            out_specs=[row(d)] * 6 + [per_b],
            out_shape=outs + [jax.ShapeDtypeStruct((bsz, 1, cols), F32)],
            scratch_shapes=[pltpu.VMEM((8, cols), F32)],
            compiler_params=_cparams("parallel", "arbitrary"),
            name="rwkv_prep_seq",
        )(u_r, prev3, *args)
        return res[:6], res[6].reshape(bsz, cols)
    t = _pick_tile(m, 256)
    row = lambda cdim: pl.BlockSpec((t, cdim), lambda i: (i, 0))
    res = pl.pallas_call(
        functools.partial(_rwkv_prep_kernel, False, None),
        grid=(m // t,),
        in_specs=[row(cols), row(cols)] + [full(a) for a in args],
        out_specs=[row(d)] * 6 + [row(cols)],
        out_shape=outs + [jax.ShapeDtypeStruct((m, cols), F32)],
        scratch_shapes=[pltpu.VMEM((8, cols), F32)],
        compiler_params=_cparams("parallel"),
        name="rwkv_prep_step",
    )(u_r, shift_prev, *args)
    return res[:6], res[6]


def _rwkv_scan_kernel(tb, r_ref, w_ref, k_ref, v_ref, a_ref, s0_ref, kk_ref, ka_ref, rk_ref, lg_ref, lb_ref,
                      o_ref, sout_ref, s_scr, g_scr, sa_scr, nkk_scr, b_scr, km_scr, rt_scr, bon_scr):
    hd = HEAD_DIM
    blk = pl.program_id(1)
    pad_rows = o_ref.shape[1] - hd

    @pl.when(blk == 0)
    def _():
        s_scr[...] = s0_ref[...]

    def prep(t, slot):
        k = k_ref[t, 0:hd, :]
        a = a_ref[t, 0:hd, :]
        r = r_ref[t, 0:hd, :]
        kk = k * kk_ref[...]
        kk = kk * lax.rsqrt(jnp.maximum(jnp.sum(kk * kk, axis=0, keepdims=True), 1e-24))
        g_prev = g_scr[...]
        g = g_prev * w_ref[t, 0:hd, :]
        g_scr[...] = g
        ginv = 1.0 / g
        nkk_scr[slot] = -(kk * g_prev)
        b_scr[slot] = kk * a * ginv
        km = k * (1.0 + (a - 1.0) * ka_ref[...])
        km_scr[slot] = km * ginv
        rt_scr[slot] = r * g
        bon_scr[slot] = jnp.broadcast_to(jnp.sum(r * km * rk_ref[...], axis=0, keepdims=True), (8, LANES))

    def step(t, slot, has_next):
        nslot = 1 - slot
        if has_next:
            prep(t + 1, nslot)
        v = v_ref[t, 0:hd, :]
        sa = sa_scr[...]
        zero = jnp.zeros((hd, LANES), F32)

        def jbody(j, carry):
            o, sa_next = carry
            sj = s_scr[j] + sa * b_scr[slot, pl.ds(j, 1), :] + v * km_scr[slot, pl.ds(j, 1), :]
            s_scr[j] = sj
            o = o + sj * rt_scr[slot, pl.ds(j, 1), :]
            if has_next:
                sa_next = sa_next + sj * nkk_scr[nslot, pl.ds(j, 1), :]
            return o, sa_next

        o, sa_next = lax.fori_loop(0, hd, jbody, (zero, zero), unroll=SCAN_UNROLL)
        if has_next:
            sa_scr[...] = sa_next
        mu = jnp.mean(o, axis=0, keepdims=True)
        dlt = o - mu
        var = jnp.mean(dlt * dlt, axis=0, keepdims=True)
        on = dlt * lax.rsqrt(var + GN_EPS) * lg_ref[...] + lb_ref[...]
        o_ref[t, 0:hd, :] = on + bon_scr[slot, 0:1, :] * v
        if pad_rows:
            o_ref[t, hd:hd + pad_rows, :] = jnp.zeros((pad_rows, LANES), F32)

    g_scr[...] = jnp.ones((hd, LANES), F32)
    prep(0, 0)
    sa0 = s_scr[0] * nkk_scr[0, 0:1, :]
    for j in range(1, hd):
        sa0 = sa0 + s_scr[j] * nkk_scr[0, j:j + 1, :]
    sa_scr[...] = sa0

    if tb >= 2:
        def pair(p, carry):
            step(2 * p, 0, True)
            step(2 * p + 1, 1, True)
            return carry

        lax.fori_loop(0, tb // 2 - 1, pair, 0)
        step(tb - 2, 0, True)
        step(tb - 1, 1, False)
    else:
        step(0, 0, False)

    for j in range(hd):
        s_scr[j] = s_scr[j] * g_scr[j:j + 1, :]

    @pl.when(blk == pl.num_programs(1) - 1)
    def _():
        sout_ref[...] = s_scr[...]


def _rwkv_scan(r, w, k, v, a, s0, lw_c):
    t_total, rows, c = r.shape
    hd = HEAD_DIM
    tb = 32 if t_total % 32 == 0 else t_total
    assert tb == 1 or tb % 2 == 0
    nb = t_total // tb
    ncl = c // LANES
    seq = pl.BlockSpec((tb, rows, LANES), lambda cl, i: (i, 0, cl))
    st = pl.BlockSpec((hd, hd, LANES), lambda cl, i: (0, 0, cl))
    par = pl.BlockSpec((hd, LANES), lambda cl, i: (0, cl))
    return pl.pallas_call(
        functools.partial(_rwkv_scan_kernel, tb),
        grid=(ncl, nb),
        in_specs=[seq] * 5 + [st] + [par] * 5,
        out_specs=[seq, st],
        out_shape=[jax.ShapeDtypeStruct((t_total, rows, c), F32), jax.ShapeDtypeStruct((hd, hd, c), F32)],
        scratch_shapes=[pltpu.VMEM((hd, hd, LANES), F32)] + [pltpu.VMEM((hd, LANES), F32)] * 2
        + [pltpu.VMEM((2, hd, LANES), F32)] * 4 + [pltpu.VMEM((2, 8, LANES), F32)],
        compiler_params=_cparams("parallel", "arbitrary"),
        name="rwkv_scan",
    )(r, w, k, v, a, s0, lw_c['k_k'], lw_c['k_a'], lw_c['r_k'], lw_c['lnx_g'], lw_c['lnx_b'])


def _to_chain(x, bsz, heads):
    t = x.shape[0] // bsz
    return x.reshape(bsz, t, heads, HEAD_DIM).transpose(1, 3, 0, 2).reshape(t, HEAD_DIM, bsz * heads)


def _from_chain(x, bsz, heads):
    t = x.shape[0]
    return x.reshape(t, HEAD_DIM, bsz, heads).transpose(2, 0, 3, 1).reshape(bsz * t, heads * HEAD_DIM)


CHAIN_PITCH = HEAD_DIM + 8


def _chain_from_u_kernel(kind, u_ref, prev_ref, mu_ref, *rest):
    if kind == 'lerp':
        o_ref, y_scr, carry_scr = rest
    else:
        p0_ref, lora_ref, o_ref, y_scr, carry_scr = rest
    nb, t, wdt = u_ref.shape

    @pl.when(pl.program_id(0) == 0)
    def _():
        carry_scr[...] = prev_ref[:, 0, :]

    rows = lax.broadcasted_iota(jnp.int32, (t, wdt), 0)
    for b in range(nb):
        u = u_ref[b]
        prev = jnp.where(rows == 0, jnp.broadcast_to(carry_scr[b:b + 1, :], (t, wdt)), pltpu.roll(u, 1, axis=0))
        carry_scr[b:b + 1, :] = u[t - 1:t, :]
        sh = u + (prev - u) * mu_ref[...]
        if kind == 'lerp':
            x = sh
        else:
            lane = lax.broadcasted_iota(jnp.int32, sh.shape, 1)
            la = jnp.where(lane < LANES // 2, jnp.tanh(sh), sh).astype(BF16)
            pre = p0_ref[...] + jnp.dot(la, lora_ref[...], preferred_element_type=F32)
            if kind == 'decay':
                x = jnp.exp(-jnp.exp(-_softplus(-pre) - 0.5))
            else:
                x = jax.nn.sigmoid(pre)
        nk = x.shape[1] // LANES
        for k in range(nk):
            tt = x[:, k * LANES:(k + 1) * LANES].T
            m0 = b * 2 * nk + 2 * k
            y_scr[m0 * CHAIN_PITCH:m0 * CHAIN_PITCH + HEAD_DIM, :] = tt[:HEAD_DIM, :]
            y_scr[(m0 + 1) * CHAIN_PITCH:(m0 + 1) * CHAIN_PITCH + HEAD_DIM, :] = tt[HEAD_DIM:, :]
    for j in range(HEAD_DIM):
        g = y_scr[pl.ds(j, LANES, stride=CHAIN_PITCH), :]
        o_ref[pl.ds(j, LANES, stride=CHAIN_PITCH), :] = g.T
    for j in range(HEAD_DIM, CHAIN_PITCH):
        o_ref[pl.ds(j, LANES, stride=CHAIN_PITCH), :] = jnp.zeros((LANES, LANES), F32)


def _chain_from_u(kind, u_r, shift_prev, lw, bsz):
    m, cols = u_r.shape
    t_total = m // bsz
    d = lw['w0'].shape[1]
    if kind in ('r', 'k', 'v'):
        width, cb, kern_kind, extra = d, 'rkv'.index(kind), 'lerp', []
    else:
        width, cb, kern_kind = LANES, 3 * d // LANES, kind
        p0 = lw['w0'] if kind == 'decay' else lw['a0']
        half = 0 if kind == 'decay' else 1
        extra = [(p0, pl.BlockSpec((1, d), lambda i: (0, 0))),
                 (lw['lora_up'], pl.BlockSpec((LANES, d), lambda i: (0, half)))]
    args = [u_r.reshape(bsz, t_total, cols), shift_prev.reshape(bsz, 1, cols), lw['shift_mu']] + [a for a, _ in extra]
    in_specs = [pl.BlockSpec((bsz, LANES, width), lambda i: (0, i, cb)),
                pl.BlockSpec((bsz, 1, width), lambda i: (0, 0, cb)),
                pl.BlockSpec((1, width), lambda i: (0, cb))] + [s for _, s in extra]
    out = pl.pallas_call(
        functools.partial(_chain_from_u_kernel, kern_kind),
        grid=(t_total // LANES,),
        in_specs=in_specs,
        out_specs=pl.BlockSpec((LANES * CHAIN_PITCH, LANES), lambda i: (i, 0)),
        out_shape=jax.ShapeDtypeStruct((t_total * CHAIN_PITCH, LANES), F32),
        scratch_shapes=[pltpu.VMEM((LANES * CHAIN_PITCH, LANES), F32), pltpu.VMEM((bsz, width), F32)],
        compiler_params=_cparams("arbitrary"),
        name="chain_" + kern_kind,
    )(*args)
    return out.reshape(t_total, CHAIN_PITCH, LANES)


def _gate_kernel(u_ref, prev_ref, mu_ref, gup_ref, g_ref, carry_scr):
    t, wdt = u_ref.shape

    @pl.when(pl.program_id(1) == 0)
    def _():
        carry_scr[...] = jnp.broadcast_to(prev_ref[0], carry_scr.shape)

    u = u_ref[...]
    rows = lax.broadcasted_iota(jnp.int32, (t, wdt), 0)
    prev = jnp.where(rows == 0, jnp.broadcast_to(carry_scr[0:1, :], (t, wdt)), pltpu.roll(u, 1, axis=0))
    carry_scr[...] = jnp.broadcast_to(u[t - 1:t, :], carry_scr.shape)
    sh = u + (prev - u) * mu_ref[...]
    gl = jax.nn.sigmoid(sh[:, wdt - LANES:]).astype(BF16)
    g_ref[...] = jnp.dot(gl, gup_ref[...], preferred_element_type=F32)


def _rwkv_gate(u_r, shift_prev, lw, bsz):
    m, cols = u_r.shape
    d = lw['w0'].shape[1]
    seq = m // bsz
    t = _pick_tile(seq, 512)
    nblk = seq // t
    wdt = 2 * LANES
    cb = cols // wdt - 1
    assert cols % wdt == 0
    return pl.pallas_call(
        _gate_kernel,
        grid=(bsz, nblk),
        in_specs=[pl.BlockSpec((t, wdt), lambda b, c: (b * nblk + c, cb)),
                  pl.BlockSpec((1, 1, wdt), lambda b, c: (b, 0, cb)),
                  pl.BlockSpec((1, wdt), lambda b, c: (0, cb)),
                  pl.BlockSpec(lw['g_lora_up'].shape, lambda b, c: (0, 0))],
        out_specs=pl.BlockSpec((t, d), lambda b, c: (b * nblk + c, 0)),
        out_shape=jax.ShapeDtypeStruct((m, d), F32),
        scratch_shapes=[pltpu.VMEM((8, wdt), F32)],
        compiler_params=_cparams("parallel", "arbitrary"),
        name="rwkv_gate",
    )(u_r, shift_prev.reshape(bsz, 1, cols), lw['shift_mu'], lw['g_lora_up'])


def _from_chain_kernel(x_ref, o_ref, y_scr):
    nb = o_ref.shape[0]
    nk = o_ref.shape[2] // LANES
    for i in range(HEAD_DIM):
        g = x_ref[pl.ds(i, LANES, stride=CHAIN_PITCH), :]
        y_scr[pl.ds(i, LANES, stride=CHAIN_PITCH), :] = g.T
    for b in range(nb):
        for k in range(nk):
            m0 = b * 2 * nk + 2 * k
            tt = jnp.concatenate([y_scr[m0 * CHAIN_PITCH:m0 * CHAIN_PITCH + HEAD_DIM, :],
                                  y_scr[(m0 + 1) * CHAIN_PITCH:(m0 + 1) * CHAIN_PITCH + HEAD_DIM, :]], axis=0)
            o_ref[b, :, k * LANES:(k + 1) * LANES] = tt.T


def _from_chain_pallas(x, bsz, d):
    t_total = x.shape[0]
    out = pl.pallas_call(
        _from_chain_kernel,
        grid=(t_total // LANES,),
        in_specs=[pl.BlockSpec((LANES * CHAIN_PITCH, LANES), lambda i: (i, 0))],
        out_specs=pl.BlockSpec((bsz, LANES, d), lambda i: (0, i, 0)),
        out_shape=jax.ShapeDtypeStruct((bsz, t_total, d), F32),
        scratch_shapes=[pltpu.VMEM((LANES * CHAIN_PITCH, LANES), F32)],
        compiler_params=_cparams("parallel"),
        name="from_chain",
    )(x.reshape(t_total * CHAIN_PITCH, LANES))
    return out.reshape(bsz * t_total, d)


def _chain_param(p, bsz, heads):
    return jnp.tile(p.reshape(heads, HEAD_DIM).T, (1, bsz))


def _pad_lanes(x, c_pad):
    c = x.shape[-1]
    if c == c_pad:
        return x
    return jnp.pad(x, [(0, 0)] * (x.ndim - 1) + [(0, c_pad - c)])


def _rwkv_core(rwkva, s0, lw, bsz, chain_ready=False):
    d = lw['w0'].shape[1]
    heads = d // HEAD_DIM
    c = bsz * heads
    c_pad = -(-c // LANES) * LANES
    if chain_ready:
        chain = rwkva
    else:
        chain = [_pad_lanes(_to_chain(x, bsz, heads), c_pad) for x in rwkva]
    s0c = _pad_lanes(s0.transpose(3, 2, 0, 1).reshape(HEAD_DIM, HEAD_DIM, c), c_pad)
    lw_c = {n: _pad_lanes(_chain_param(lw[n], bsz, heads), c_pad) for n in ('k_k', 'k_a', 'r_k', 'lnx_g', 'lnx_b')}
    o, s = _rwkv_scan(*chain, s0c, lw_c)
    if chain_ready:
        o = _from_chain_pallas(o, bsz, d)
    else:
        o = _from_chain(o[..., :c], bsz, heads)
    s = s[..., :c].reshape(HEAD_DIM, HEAD_DIM, bsz, heads).transpose(2, 3, 1, 0)
    return o, s


def _prep_layer_weights(w, l):
    d = w['w_in'].shape[1]
    d_inner = w['p_ssm'].shape[1]
    heads_ssm = w['dt_bias'].shape[1]
    conv_dim = w['conv_w'].shape[2]
    rcols = w['shift_mu'].shape[1]
    dl = w['w_lora_up'].shape[1]
    s0, s1, s2 = d_inner, d_inner + conv_dim, d_inner + conv_dim + heads_ssm
    s3 = s2 + rcols
    w_in = w['w_in'][l]
    pad_h = lambda v, fill: jnp.pad(v, (0, LANES - heads_ssm), constant_values=fill).reshape(1, LANES)
    row = lambda v: v.reshape(1, -1)
    lora = jnp.zeros((LANES, 2 * d), F32)
    lora = lora.at[:dl, :d].set(w['w_lora_up'][l]).at[dl:dl + w['a_lora_up'].shape[1], d:].set(w['a_lora_up'][l])
    return {
        'w_zx': w_in[:, :s1].astype(BF16),
        'w_dt': jnp.pad(w_in[:, s1:s2], ((0, 0), (0, LANES - heads_ssm))).astype(BF16),
        'layer': l, 'd_inner': d_inner, 'w_ffn_in_all': w['w_ffn_in'],
        'w_dt_x': jnp.repeat(w_in[:, s1:s2], HEAD_DIM, axis=1).astype(BF16),
        'dt_bias_x': row(jnp.repeat(w['dt_bias'][l], HEAD_DIM)),
        'a_log_x': row(jnp.repeat(w['a_log'][l], HEAD_DIM)),
        'w_r': w_in[:, s2:s3].astype(BF16), 'w_gate': w_in[:, s3:].astype(BF16),
        'conv_w': w['conv_w'][l], 'conv_b': row(w['conv_b'][l]),
        'dt_bias': pad_h(w['dt_bias'][l], 0.0), 'a_log': pad_h(w['a_log'][l], 0.0),
        'd_skip': row(jnp.repeat(w['d_skip'][l], HEAD_DIM)),
        'ssm_norm_w': row(w['ssm_norm_w'][l]),
        'p_ssm': w['p_ssm'][l].astype(BF16),
        'shift_mu': row(w['shift_mu'][l]), 'w0': row(w['w0'][l]), 'a0': row(w['a0'][l]),
        'lora_up': lora.astype(BF16), 'g_lora_up': w['g_lora_up'][l].astype(BF16),
        'k_k': w['k_k'][l], 'k_a': w['k_a'][l], 'r_k': w['r_k'][l].reshape(-1),
        'lnx_g': w['lnx_g'][l], 'lnx_b': w['lnx_b'][l],
        'p_rwkv': w['p_rwkv'][l].astype(BF16), 'w_out': w['w_out'][l].astype(BF16),
        'ln1_g': row(w['ln1_g'][l]), 'ln1_b': row(w['ln1_b'][l]),
        'w_ffn_out': w['w_ffn_out'][l].astype(BF16),
        'ln2_g': row(w['ln2_g'][l]), 'ln2_b': row(w['ln2_b'][l]),
    }


def _layer_tail(alpha, x, xb, y_ssm, o_rwkv, g_rwkv, u_gate, lw):
    del xb
    x1, x1b = _merge_out_ln(alpha, y_ssm, o_rwkv, g_rwkv, u_gate, x, lw['p_ssm'], lw['p_rwkv'], lw['w_out'],
                            lw['ln1_g'], lw['ln1_b'])
    hmid = _ffn_in(x1b, lw['w_ffn_in_all'], lw['layer'])
    return _ffn_out_ln(alpha, hmid, lw['w_ffn_out'], x1, lw['ln2_g'], lw['ln2_b'])


def _in_proj(xb, lw, dt_expanded=False):
    dt = _matmul(xb, lw['w_dt_x' if dt_expanded else 'w_dt'])
    return _matmul(xb, lw['w_zx']), dt, _matmul(xb, lw['w_r']), _matmul(xb, lw['w_gate'])


def _seq_layer(alpha, x, xb, ssm0, conv0, wkv0, shift0, lw, bsz, l_real):
    l_pad = x.shape[0] // bsz
    u_zx, u_dt, u_r, u_gate = _in_proj(xb, lw)
    h0 = ssm0.reshape(bsz, -1, SSM_STATE)
    y_ssm, h_new, conv_new = _ssd_chunked(u_zx, u_dt, h0, conv0, lw, bsz, l_real)
    heads_r = lw['w0'].shape[1] // HEAD_DIM
    if bsz * heads_r == LANES and l_real == l_pad:
        chain = [_chain_from_u(kind, u_r, shift0, lw, bsz) for kind in ('r', 'decay', 'k', 'v', 'rate')]
        g = _rwkv_gate(u_r, shift0, lw, bsz)
        shift_new = u_r.reshape(bsz, l_pad, -1)[:, -1]
        o, s_new = _rwkv_core(chain, wkv0, lw, bsz, chain_ready=True)
        x2, x2b = _layer_tail(alpha, x, xb, y_ssm, o, g, u_gate, lw)
        return x2, x2b, h_new.reshape(ssm0.shape), conv_new, s_new, shift_new
    (r, w, k, v, a, g), shift_new = _rwkv_prep(u_r, shift0, lw, bsz, True, l_real)
    if l_real < l_pad:
        cut = lambda t: t.reshape(bsz, l_pad, -1)[:, :l_real].reshape(bsz * l_real, -1)
        o, s_new = _rwkv_core([cut(t) for t in (r, w, k, v, a)], wkv0, lw, bsz)
        o = jnp.pad(o.reshape(bsz, l_real, -1), ((0, 0), (0, l_pad - l_real), (0, 0))).reshape(bsz * l_pad, -1)
    else:
        o, s_new = _rwkv_core([r, w, k, v, a], wkv0, lw, bsz)
    x2, x2b = _layer_tail(alpha, x, xb, y_ssm, o, g, u_gate, lw)
    return x2, x2b, h_new.reshape(ssm0.shape), conv_new, s_new, shift_new


def _step_layer(alpha, x, xb, ssm_all, conv_all, layer, h_acc, wkv0, shift0, lw):
    bsz = x.shape[0]
    u_zx, u_dtx, u_r, u_gate = _in_proj(xb, lw, dt_expanded=True)
    u_z, u_xbc = u_zx[:, :lw['d_inner']], u_zx[:, lw['d_inner']:]
    y_ssm, h_acc, conv_new = _ssd_step(u_z, u_xbc, u_dtx, ssm_all, conv_all, layer, lw, h_acc)
    (r, w, k, v, a, g), shift_new = _rwkv_prep(u_r, shift0, lw, bsz, False)
    o, s_new = _rwkv_core([r, w, k, v, a], wkv0, lw, bsz)
    x2, x2b = _layer_tail(alpha, x, xb, y_ssm, o, g, u_gate, lw)
    return x2, x2b, h_acc, conv_new, s_new, shift_new


def kernel(x_prompt, x_sample, state_ssm, state_conv, state_wkv, state_shift, meta_tokens, w_in, conv_w, conv_b,
           dt_bias, a_log, d_skip, ssm_norm_w, p_ssm, shift_mu, w0, w_lora_up, a0, a_lora_up, g_lora_up, k_k,
           k_a, r_k, lnx_g, lnx_b, p_rwkv, w_out, ln1_g, ln1_b, w_ffn_in, w_ffn_out, ln2_g, ln2_b):
    weights = {
        'w_in': w_in, 'conv_w': conv_w, 'conv_b': conv_b, 'dt_bias': dt_bias, 'a_log': a_log,
        'd_skip': d_skip, 'ssm_norm_w': ssm_norm_w, 'p_ssm': p_ssm, 'shift_mu': shift_mu,
        'w0': w0, 'w_lora_up': w_lora_up, 'a0': a0, 'a_lora_up': a_lora_up, 'g_lora_up': g_lora_up,
        'k_k': k_k, 'k_a': k_a, 'r_k': r_k, 'lnx_g': lnx_g, 'lnx_b': lnx_b, 'p_rwkv': p_rwkv,
        'w_out': w_out, 'ln1_g': ln1_g, 'ln1_b': ln1_b, 'w_ffn_in': w_ffn_in,
        'w_ffn_out': w_ffn_out, 'ln2_g': ln2_g, 'ln2_b': ln2_b,
    }
    depth = w_in.shape[0]
    alpha = (2 * depth) ** 0.25
    bsz, seq, d = x_prompt.shape
    n_meta = meta_tokens.shape[0]
    layers = [_prep_layer_weights(weights, l) for l in range(depth)]

    xm = jnp.pad(meta_tokens.astype(F32), ((0, SSD_CHUNK - n_meta), (0, 0)))
    xp = x_prompt.reshape(bsz * seq, d)
    xs = x_sample.reshape(x_sample.shape[0], d)
    xmb, xpb, xsb = xm.astype(BF16), xp.astype(BF16), xs.astype(BF16)
    bcast = lambda t: jnp.broadcast_to(t, (bsz,) + t.shape[1:])
    ssm_p, conv_p, wkv_p, shift_p, conv_s, wkv_s, shift_s = [], [], [], [], [], [], []
    ssm_all = state_ssm.reshape(state_ssm.shape[:2] + (-1, SSM_STATE))
    ssm_s = None
    for l in range(depth):
        lw = layers[l]
        z = lambda a: jnp.zeros((1,) + a.shape[2:], F32)
        xm, xmb, hm, cm, sm, shm = _seq_layer(alpha, xm, xmb, z(state_ssm), z(state_conv), z(state_wkv),
                                              z(state_shift), lw, 1, n_meta)
        xp, xpb, h, c, s, sh = _seq_layer(alpha, xp, xpb, bcast(hm), bcast(cm), bcast(sm), bcast(shm), lw,
                                          bsz, seq)
        ssm_p.append(h), conv_p.append(c), wkv_p.append(s), shift_p.append(sh)
        xs, xsb, ssm_s, c, s, sh = _step_layer(alpha, xs, xsb, ssm_all, state_conv, l, ssm_s, state_wkv[l],
                                               state_shift[l], lw)
        conv_s.append(c), wkv_s.append(s), shift_s.append(sh)
    st = jnp.stack
    return (xp.reshape(bsz, seq, d), xs.reshape(x_sample.shape), st(ssm_p), st(conv_p), st(wkv_p), st(shift_p),
            ssm_s.reshape(state_ssm.shape), st(conv_s), st(wkv_s), st(shift_s))
```

```python
import functools
import math

import jax
import jax.numpy as jnp
from jax import lax
from jax.experimental import pallas as pl
from jax.experimental.pallas import tpu as pltpu

F32 = jnp.float32
BF16 = jnp.bfloat16

LANES = 128
N_META = 16
HEAD_DIM = 64
SSM_STATE = 128
SSM_GROUPS = 4
CONV_K = 4
SSD_CHUNK = 128
CONV_PITCH = 3
SCAN_UNROLL = 32
MM_TN_CAP = 14 * LANES
LOG2E = 1.4426950408889634
LN_EPS = 1e-5
RMS_EPS = 1e-5
GN_EPS = 64e-5
VMEM_LIMIT_BYTES = 56 * 1024 * 1024


def _cparams(*sem):
    return pltpu.CompilerParams(dimension_semantics=sem, vmem_limit_bytes=VMEM_LIMIT_BYTES)


def _pick_tile(n, cap):
    if n <= cap:
        return n
    best = LANES
    for t in range(LANES, cap + 1, LANES):
        if n % t == 0:
            best = t
    return best


def _silu(x):
    return x * jax.nn.sigmoid(x)


def _softplus(x):
    return jnp.maximum(x, 0.0) + jnp.log(1.0 + jnp.exp(-jnp.abs(x)))


def _layer_norm(y, g, b):
    mu = jnp.mean(y, axis=-1, keepdims=True)
    d = y - mu
    var = jnp.mean(d * d, axis=-1, keepdims=True)
    return d * lax.rsqrt(var + LN_EPS) * g + b


def _cast_kernel(x_ref, o_ref):
    o_ref[...] = x_ref[...].astype(o_ref.dtype)


def _to_bf16(x):
    m, d = x.shape
    tm = _pick_tile(m, 1024)
    return pl.pallas_call(
        _cast_kernel,
        grid=(m // tm,),
        in_specs=[pl.BlockSpec((tm, d), lambda i: (i, 0))],
        out_specs=pl.BlockSpec((tm, d), lambda i: (i, 0)),
        out_shape=jax.ShapeDtypeStruct((m, d), BF16),
        compiler_params=_cparams("parallel"),
        name="to_bf16",
    )(x)


def _mm_kernel(x_ref, w_ref, o_ref):
    o_ref[...] = jnp.dot(x_ref[...], w_ref[...], preferred_element_type=F32).astype(o_ref.dtype)


def _matmul(x, w, out_dtype=F32):
    m, k = x.shape
    n = w.shape[1]
    tm = _pick_tile(m, 1024)
    tn = _pick_tile(n, MM_TN_CAP)
    return pl.pallas_call(
        _mm_kernel,
        grid=(n // tn, m // tm),
        in_specs=[pl.BlockSpec((tm, k), lambda j, i: (i, 0)),
                  pl.BlockSpec((k, tn), lambda j, i: (0, j))],
        out_specs=pl.BlockSpec((tm, tn), lambda j, i: (i, j)),
        out_shape=jax.ShapeDtypeStruct((m, n), out_dtype),
        compiler_params=_cparams("parallel", "parallel"),
        name="matmul",
    )(x, w)


def _merge_kernel(alpha, ys_ref, yr_ref, g_ref, ug_ref, x_ref, ps_ref, pr_ref, wo_ref, lg_ref, lb_ref,
                  o_ref, ob_ref):
    d = o_ref.shape[1]
    a = jnp.dot(ys_ref[...], ps_ref[...], preferred_element_type=F32)
    yr = (yr_ref[...] * g_ref[...]).astype(BF16)
    b = jnp.dot(yr, pr_ref[...], preferred_element_type=F32)
    gates = jax.nn.sigmoid(ug_ref[...])
    merged = gates[:, :d] * a + gates[:, d:] * b
    y = alpha * x_ref[...] + jnp.dot(merged.astype(BF16), wo_ref[...], preferred_element_type=F32)
    out = _layer_norm(y, lg_ref[...], lb_ref[...])
    o_ref[...] = out
    ob_ref[...] = out.astype(BF16)


def _merge_out_ln(alpha, y_ssm, y_rwkv, g_rwkv, u_gate, x, p_ssm, p_rwkv, w_out, ln_g, ln_b):
    m, d = x.shape
    tm = _pick_tile(m, 512)
    row = lambda c: pl.BlockSpec((tm, c), lambda i: (i, 0))
    full = lambda a: pl.BlockSpec(a.shape, lambda i: (0, 0))
    return pl.pallas_call(
        functools.partial(_merge_kernel, alpha),
        grid=(m // tm,),
        in_specs=[row(y_ssm.shape[1]), row(d), row(d), row(2 * d), row(d),
                  full(p_ssm), full(p_rwkv), full(w_out), full(ln_g), full(ln_b)],
        out_specs=[row(d), row(d)],
        out_shape=[jax.ShapeDtypeStruct((m, d), F32), jax.ShapeDtypeStruct((m, d), BF16)],
        compiler_params=_cparams("parallel"),
        name="merge_out_ln",
    )(y_ssm, y_rwkv, g_rwkv, u_gate, x, p_ssm, p_rwkv, w_out, ln_g, ln_b)


def _swiglu_kernel(x_ref, wg_ref, wu_ref, o_ref, wgb_scr, wub_scr):
    @pl.when(pl.program_id(1) == 0)
    def _():
        wgb_scr[...] = wg_ref[...].astype(BF16)
        wub_scr[...] = wu_ref[...].astype(BF16)

    x = x_ref[...]
    hg = jnp.dot(x, wgb_scr[...], preferred_element_type=F32)
    hu = jnp.dot(x, wub_scr[...], preferred_element_type=F32)
    o_ref[...] = (_silu(hg) * hu).astype(o_ref.dtype)


def _ffn_in(x, w_ffn_in_all, layer):
    m, k = x.shape
    dff = w_ffn_in_all.shape[2] // 2
    tm = _pick_tile(m, 1024)
    tn = _pick_tile(dff, 1408)
    nj = dff // tn
    return pl.pallas_call(
        _swiglu_kernel,
        grid=(nj, m // tm),
        in_specs=[pl.BlockSpec((tm, k), lambda j, i: (i, 0)),
                  pl.BlockSpec((None, k, tn), lambda j, i: (layer, 0, j)),
                  pl.BlockSpec((None, k, tn), lambda j, i: (layer, 0, j + nj))],
        out_specs=pl.BlockSpec((tm, tn), lambda j, i: (i, j)),
        out_shape=jax.ShapeDtypeStruct((m, dff), BF16),
        scratch_shapes=[pltpu.VMEM((k, tn), BF16)] * 2,
        compiler_params=_cparams("parallel", "arbitrary"),
        name="ffn_in_swiglu",
    )(x, w_ffn_in_all, w_ffn_in_all)


def _ffn_out_kernel(alpha, h_ref, w_ref, x_ref, lg_ref, lb_ref, o_ref, ob_ref):
    y = alpha * x_ref[...] + jnp.dot(h_ref[...], w_ref[...], preferred_element_type=F32)
    out = _layer_norm(y, lg_ref[...], lb_ref[...])
    o_ref[...] = out
    ob_ref[...] = out.astype(BF16)


def _ffn_out_ln(alpha, h, w_ffn_out, x, ln_g, ln_b):
    m, d = x.shape
    k = h.shape[1]
    tm = _pick_tile(m, 512)
    row = lambda c: pl.BlockSpec((tm, c), lambda i: (i, 0))
    full = lambda a: pl.BlockSpec(a.shape, lambda i: (0, 0))
    return pl.pallas_call(
        functools.partial(_ffn_out_kernel, alpha),
        grid=(m // tm,),
        in_specs=[row(k), full(w_ffn_out), row(d), full(ln_g), full(ln_b)],
        out_specs=[row(d), row(d)],
        out_shape=[jax.ShapeDtypeStruct((m, d), F32), jax.ShapeDtypeStruct((m, d), BF16)],
        compiler_params=_cparams("parallel"),
        name="ffn_out_ln",
    )(h, w_ffn_out, x, ln_g, ln_b)


def _ssd_chunk_kernel(l_real, nchunks,
                      uzx_ref, udt_ref, h0_ref, cpre_ref, convw_ref, convb_ref, dtb_ref, alog_ref,
                      dskip_ref, normw_ref,
                      y_ref, hout_ref, ctail_ref,
                      ht_scr, cbuf_scr, xbc_scr, y_scr):
    q = SSD_CHUNK
    d_inner = y_ref.shape[1]
    gw = d_inner // SSM_GROUPS
    c = pl.program_id(1)

    @pl.when(c == 0)
    def _():
        for kb in range(d_inner // LANES):
            ht_scr[:, kb * LANES:(kb + 1) * LANES] = h0_ref[0, kb * LANES:(kb + 1) * LANES, :].T
        for ct in range(cbuf_scr.shape[0]):
            cbuf_scr[ct, pl.ds(5 * CONV_PITCH, CONV_K - 1, stride=CONV_PITCH), :] = \
                cpre_ref[0, :, ct * LANES:(ct + 1) * LANES]

    rows_at = lambda r0, n: pl.ds(r0 * CONV_PITCH, n, stride=CONV_PITCH)
    for ct in range(cbuf_scr.shape[0]):
        cols = slice(ct * LANES, (ct + 1) * LANES)
        cbuf_scr[ct, rows_at(8, q), :] = uzx_ref[:, d_inner + ct * LANES:d_inner + (ct + 1) * LANES]
        acc = convb_ref[:, cols] + cbuf_scr[ct, rows_at(5, q), :] * convw_ref[0:1, cols]
        for k in range(1, CONV_K):
            acc = acc + cbuf_scr[ct, rows_at(5 + k, q), :] * convw_ref[k:k + 1, cols]
        xbc_scr[:, cols] = _silu(acc)
        tail = cbuf_scr[ct, rows_at(l_real + 5, CONV_K - 1), :]
        cbuf_scr[ct, rows_at(5, CONV_K - 1), :] = tail

    rows = lax.broadcasted_iota(jnp.int32, (q, LANES), 0)
    dt = _softplus(udt_ref[...] + dtb_ref[...])
    if l_real < q:
        dt = jnp.where(rows < l_real, dt, 0.0)
    da = dt * (-jnp.exp(alog_ref[...]))
    acum = da
    s = 1
    while s < q:
        acum = acum + jnp.where(rows >= s, pltpu.roll(acum, s, axis=0), 0.0)
        s *= 2
    acum = acum * LOG2E
    acum_t = acum.T
    dt_t = dt.T
    a_last = acum[q - 1:q, :]
    st_t = dt_t * jnp.exp2(acum_t[:, q - 1:q] - acum_t)
    ii = lax.broadcasted_iota(jnp.int32, (q, q), 0)
    jj = lax.broadcasted_iota(jnp.int32, (q, q), 1)
    causal = ii >= jj
    low = lax.broadcasted_iota(jnp.int32, (q, LANES), 1) < HEAD_DIM

    for g in range(SSM_GROUPS):
        bm = xbc_scr[:, d_inner + g * SSM_STATE:d_inner + (g + 1) * SSM_STATE]
        cm = xbc_scr[:, d_inner + (SSM_GROUPS + g) * SSM_STATE:d_inner + (SSM_GROUPS + g + 1) * SSM_STATE]
        cmb = cm.astype(BF16)
        cb = lax.dot_general(cmb, bm.astype(BF16), (((1,), (1,)), ((), ())), preferred_element_type=F32)
        bm_t = bm.T
        for pr in range(gw // LANES):
            lanes = slice(g * gw + pr * LANES, g * gw + (pr + 1) * LANES)
            hd0 = (g * gw + pr * LANES) // HEAD_DIM
            x_pair = xbc_scr[:, lanes]
            xb = x_pair.astype(BF16)
            yo = jnp.dot(cmb, ht_scr[:, lanes].astype(BF16), preferred_element_type=F32)
            yd, st, acol, alast = [], [], [], []
            for hd in (hd0, hd0 + 1):
                a_col = jnp.broadcast_to(acum[:, hd:hd + 1], (q, q))
                lmat = jnp.exp2(jnp.where(causal, a_col - acum_t[hd:hd + 1, :], -jnp.inf))
                wd = (cb * lmat * dt_t[hd:hd + 1, :]).astype(BF16)
                yd.append(jnp.dot(wd, xb, preferred_element_type=F32))
                acol.append(a_col)
                st.append(jnp.dot((bm_t * st_t[hd:hd + 1, :]).astype(BF16), xb, preferred_element_type=F32))
                alast.append(jnp.broadcast_to(a_last[:, hd:hd + 1], (SSM_STATE, LANES)))
            ea = jnp.exp2(jnp.where(low, acol[0], acol[1]))
            y_scr[:, lanes] = jnp.where(low, yd[0], yd[1]) + yo * ea + x_pair * dskip_ref[:, lanes]
            cd = jnp.exp2(jnp.where(low, alast[0], alast[1]))
            ht_scr[:, lanes] = ht_scr[:, lanes] * cd + jnp.where(low, st[0], st[1])

    for g in range(SSM_GROUPS):
        cols = slice(g * gw, (g + 1) * gw)
        yg = y_scr[:, cols] * _silu(uzx_ref[:, cols])
        ms = jnp.mean(yg * yg, axis=-1, keepdims=True)
        y_ref[:, cols] = (yg * lax.rsqrt(ms + RMS_EPS) * normw_ref[:, cols]).astype(y_ref.dtype)

    @pl.when(c == nchunks - 1)
    def _():
        for kb in range(d_inner // LANES):
            hout_ref[0, kb * LANES:(kb + 1) * LANES, :] = ht_scr[:, kb * LANES:(kb + 1) * LANES].T
        for ct in range(cbuf_scr.shape[0]):
            ctail_ref[0, :, ct * LANES:(ct + 1) * LANES] = \
                cbuf_scr[ct, pl.ds(5 * CONV_PITCH, CONV_K - 1, stride=CONV_PITCH), :]


def _ssd_chunked(u_zxd, h0, conv_pre, lw, bsz, l_real):
    m = u_zxd.shape[0]
    conv_dim = conv_pre.shape[2]
    d_inner = lw['d_inner']
    q = SSD_CHUNK
    nchunks = m // bsz // q
    row = lambda cdim, cblk=0: pl.BlockSpec((q, cdim), lambda b, c: (b * nchunks + c, cblk))
    full = lambda a: pl.BlockSpec(a.shape, lambda b, c: (0, 0))
    per_b = lambda a: pl.BlockSpec((1,) + a.shape[1:], lambda b, c: (b, 0, 0))
    hshape = jax.ShapeDtypeStruct(h0.shape, F32)
    cshape = jax.ShapeDtypeStruct(conv_pre.shape, F32)
    return pl.pallas_call(
        functools.partial(_ssd_chunk_kernel, min(l_real, q), nchunks),
        grid=(bsz, nchunks),
        in_specs=[row(d_inner + conv_dim), row(LANES, (d_inner + conv_dim) // LANES), per_b(h0), per_b(conv_pre),
                  full(lw['conv_w']), full(lw['conv_b']), full(lw['dt_bias']), full(lw['a_log']),
                  full(lw['d_skip']), full(lw['ssm_norm_w'])],
        out_specs=[row(d_inner), per_b(h0), per_b(conv_pre)],
        out_shape=[jax.ShapeDtypeStruct((m, d_inner), BF16), hshape, cshape],
        scratch_shapes=[pltpu.VMEM((SSM_STATE, d_inner), F32),
                        pltpu.VMEM((conv_dim // LANES, (q + 8) * CONV_PITCH, LANES), F32),
                        pltpu.VMEM((q, conv_dim), F32),
                        pltpu.VMEM((q, d_inner), F32)],
        compiler_params=_cparams("parallel", "arbitrary"),
        name="ssd_chunk",
    )(u_zxd, u_zxd, h0, conv_pre, lw['conv_w'], lw['conv_b'], lw['dt_bias'], lw['a_log'],
      lw['d_skip'], lw['ssm_norm_w'])


def _row_to_col(row):
    r_i = lax.broadcasted_iota(jnp.int32, (LANES, LANES), 0)
    c_i = lax.broadcasted_iota(jnp.int32, (LANES, LANES), 1)
    return jnp.sum(jnp.where(r_i == c_i, jnp.broadcast_to(row, (LANES, LANES)), 0.0), axis=1, keepdims=True)


def _ssd_step_kernel(aliased, layer, uz_ref, uxbc_ref, udtx_ref, h0_ref, cpre_ref, convw_ref, convb_ref, dtbx_ref,
                     alogx_ref, dskip_ref, normw_ref, *rest):
    y_ref, hout_ref, ctail_ref, ht_scr = rest[1:] if aliased else rest
    d_inner = uz_ref.shape[2]
    gw = d_inner // SSM_GROUPS
    u = uxbc_ref[0]
    pre = cpre_ref[0, 0]
    acc = convb_ref[...] + u * convw_ref[CONV_K - 1:CONV_K, :]
    for k in range(CONV_K - 1):
        acc = acc + pre[k:k + 1, :] * convw_ref[k:k + 1, :]
    xbc = _silu(acc)
    ctail_ref[0, 0:CONV_K - 2, :] = pre[1:, :]
    ctail_ref[0, CONV_K - 2:CONV_K - 1, :] = u
    dt = _softplus(udtx_ref[0] + dtbx_ref[...])
    dec = jnp.exp(dt * (-jnp.exp(alogx_ref[...])))
    x_row = xbc[:, :d_inner]
    xdt = x_row * dt
    for kb in range(d_inner // LANES):
        ht_scr[:, kb * LANES:(kb + 1) * LANES] = h0_ref[0, 0, kb * LANES:(kb + 1) * LANES, :].T
    y_parts = []
    for g in range(SSM_GROUPS):
        cols = slice(g * gw, (g + 1) * gw)
        bm = xbc[:, d_inner + g * SSM_STATE:d_inner + (g + 1) * SSM_STATE]
        cm = xbc[:, d_inner + (SSM_GROUPS + g) * SSM_STATE:d_inner + (SSM_GROUPS + g + 1) * SSM_STATE]
        hn = ht_scr[:, cols] * dec[:, cols] + _row_to_col(bm) * xdt[:, cols]
        ht_scr[:, cols] = hn
        y_parts.append(jnp.sum(hn * _row_to_col(cm), axis=0, keepdims=True))
    y_row = jnp.concatenate(y_parts, axis=1) + x_row * dskip_ref[...]
    y_row = y_row * _silu(uz_ref[0])
    outs = []
    for g in range(SSM_GROUPS):
        yg = y_row[:, g * gw:(g + 1) * gw]
        ms = jnp.mean(yg * yg, axis=-1, keepdims=True)
        outs.append(yg * lax.rsqrt(ms + RMS_EPS))
    y_ref[0] = (jnp.concatenate(outs, axis=1) * normw_ref[...]).astype(y_ref.dtype)
    slot = 0 if aliased else layer
    r_i = lax.broadcasted_iota(jnp.int32, (LANES, 3 * LANES), 0)
    c_i = lax.broadcasted_iota(jnp.int32, (LANES, 3 * LANES), 1)
    eye3 = jnp.where((c_i == r_i) | (c_i == r_i + LANES) | (c_i == r_i + 2 * LANES), 1.0, 0.0).astype(BF16)
    for kb in range(d_inner // LANES):
        a = ht_scr[:, kb * LANES:(kb + 1) * LANES]
        hi = a.astype(BF16)
        rest1 = a - hi.astype(F32)
        mid = rest1.astype(BF16)
        lo = (rest1 - mid.astype(F32)).astype(BF16)
        pieces = jnp.concatenate([hi, mid, lo], axis=1)
        hout_ref[slot, 0, kb * LANES:(kb + 1) * LANES, :] = lax.dot_general(
            eye3, pieces, (((1,), (1,)), ((), ())), preferred_element_type=F32)
    if not aliased:
        for other in range(hout_ref.shape[0]):
            if other != layer:
                hout_ref[other, 0] = jnp.zeros(hout_ref.shape[2:], F32)


def _ssd_step(u_z, u_xbc, u_dtx, ssm_all, conv_all, layer, lw, h_acc):
    bsz, d_inner = u_z.shape
    r3 = lambda a: a.reshape(bsz, 1, a.shape[1])
    per_b = lambda a: pl.BlockSpec((1,) + a.shape[1:], lambda b: (b, 0, 0))
    per_lb = lambda a: pl.BlockSpec((1, 1) + a.shape[2:], lambda b: (layer, b, 0, 0))
    full = lambda a: pl.BlockSpec(a.shape, lambda b: (0, 0))
    uz3, ux3, ud3 = r3(u_z), r3(u_xbc), r3(u_dtx)
    cshape = conv_all.shape[1:]
    args = [uz3, ux3, ud3, ssm_all, conv_all, lw['conv_w'], lw['conv_b'], lw['dt_bias_x'], lw['a_log_x'],
            lw['d_skip'], lw['ssm_norm_w']]
    in_specs = [per_b(uz3), per_b(ux3), per_b(ud3), per_lb(ssm_all), per_lb(conv_all)]
    in_specs += [full(a) for a in args[5:]]
    aliases = {}
    h_spec = pl.BlockSpec((ssm_all.shape[0], 1) + ssm_all.shape[2:], lambda b: (0, b, 0, 0))
    if h_acc is not None:
        args.append(h_acc)
        in_specs.append(pl.BlockSpec(memory_space=pl.ANY))
        aliases = {len(args) - 1: 1}
        h_spec = per_lb(ssm_all)
    y, h, ct = pl.pallas_call(
        functools.partial(_ssd_step_kernel, h_acc is not None, layer),
        grid=(bsz,),
        in_specs=in_specs,
        out_specs=[per_b(uz3), h_spec, pl.BlockSpec((1,) + cshape[1:], lambda b: (b, 0, 0))],
        out_shape=[jax.ShapeDtypeStruct((bsz, 1, d_inner), BF16), jax.ShapeDtypeStruct(ssm_all.shape, F32),
                   jax.ShapeDtypeStruct(cshape, F32)],
        scratch_shapes=[pltpu.VMEM((SSM_STATE, d_inner), F32)],
        input_output_aliases=aliases,
        compiler_params=_cparams("arbitrary"),
        name="ssd_step",
    )(*args)
    return y.reshape(bsz, d_inner), h, ct


def _rwkv_prep_kernel(seq_mode, l_real, ur_ref, prev_ref, mu_ref, w0_ref, a0_ref, lora_ref, gup_ref,
                      r_ref, w_ref, k_ref, v_ref, a_ref, g_ref, last_ref, carry_scr):
    t, cols = ur_ref.shape
    d = r_ref.shape[1]
    u = ur_ref[...]
    if seq_mode:
        c = pl.program_id(1)

        @pl.when(c == 0)
        def _():
            carry_scr[...] = jnp.broadcast_to(prev_ref[0], carry_scr.shape)

        rows = lax.broadcasted_iota(jnp.int32, (t, cols), 0)
        prev = jnp.where(rows == 0, jnp.broadcast_to(carry_scr[0:1, :], (t, cols)), pltpu.roll(u, 1, axis=0))
        lrow = (l_real - 1) % t
        carry_scr[...] = jnp.broadcast_to(u[lrow:lrow + 1, :], carry_scr.shape)
        last_ref[0] = u[lrow:lrow + 1, :]
    else:
        prev = prev_ref[...]
        last_ref[...] = u
    sh = u + (prev - u) * mu_ref[...]
    r_ref[...] = sh[:, 0:d]
    k_ref[...] = sh[:, d:2 * d]
    v_ref[...] = sh[:, 2 * d:3 * d]
    la = sh[:, 3 * d:3 * d + LANES]
    lane = lax.broadcasted_iota(jnp.int32, la.shape, 1)
    la = jnp.where(lane < LANES // 2, jnp.tanh(la), la).astype(BF16)
    pre = jnp.dot(la, lora_ref[...], preferred_element_type=F32)
    w_log = -_softplus(-(w0_ref[...] + pre[:, :d])) - 0.5
    w_ref[...] = jnp.exp(-jnp.exp(w_log))
    a_ref[...] = jax.nn.sigmoid(a0_ref[...] + pre[:, d:])
    gl = jax.nn.sigmoid(sh[:, 3 * d + LANES:]).astype(BF16)
    g_ref[...] = jnp.dot(gl, gup_ref[...], preferred_element_type=F32)


def _rwkv_prep(u_r, shift_prev, lw, bsz, seq_mode, l_real=None):
    m, cols = u_r.shape
    d = lw['w0'].shape[1]
    full = lambda a: pl.BlockSpec(a.shape, lambda *_: (0,) * a.ndim)
    outs = [jax.ShapeDtypeStruct((m, d), F32)] * 6
    args = (lw['shift_mu'], lw['w0'], lw['a0'], lw['lora_up'], lw['g_lora_up'])
    if seq_mode:
        seq = m // bsz
        t = _pick_tile(seq, 256)
        nblk = seq // t
        l_real = seq if l_real is None else l_real
        row = lambda cdim: pl.BlockSpec((t, cdim), lambda b, c: (b * nblk + c, 0))
        prev3 = shift_prev.reshape(bsz, 1, cols)
        per_b = pl.BlockSpec((1, 1, cols), lambda b, c: (b, 0, 0))
        res = pl.pallas_call(
            functools.partial(_rwkv_prep_kernel, True, l_real),
            grid=(bsz, nblk),
            in_specs=[row(cols), per_b] + [full(a) for a in args],
            out_specs=[row(d)] * 6 + [per_b],
            out_shape=outs + [jax.ShapeDtypeStruct((bsz, 1, cols), F32)],
            scratch_shapes=[pltpu.VMEM((8, cols), F32)],
            compiler_params=_cparams("parallel", "arbitrary"),
            name="rwkv_prep_seq",
        )(u_r, prev3, *args)
        return res[:6], res[6].reshape(bsz, cols)
    t = _pick_tile(m, 256)
    row = lambda cdim: pl.BlockSpec((t, cdim), lambda i: (i, 0))
    res = pl.pallas_call(
        functools.partial(_rwkv_prep_kernel, False, None),
        grid=(m // t,),
        in_specs=[row(cols), row(cols)] + [full(a) for a in args],
        out_specs=[row(d)] * 6 + [row(cols)],
        out_shape=outs + [jax.ShapeDtypeStruct((m, cols), F32)],
        scratch_shapes=[pltpu.VMEM((8, cols), F32)],
        compiler_params=_cparams("parallel"),
        name="rwkv_prep_step",
    )(u_r, shift_prev, *args)
    return res[:6], res[6]


def _rwkv_scan_kernel(tb, r_ref, w_ref, k_ref, v_ref, a_ref, s0_ref, kk_ref, ka_ref, rk_ref, lg_ref, lb_ref,
                      o_ref, sout_ref, s_scr, g_scr, sa_scr, nkk_scr, b_scr, km_scr, rt_scr, bon_scr):
    hd = HEAD_DIM
    blk = pl.program_id(1)
    pad_rows = o_ref.shape[1] - hd

    @pl.when(blk == 0)
    def _():
        s_scr[...] = s0_ref[...]

    def prep(t, slot):
        k = k_ref[t, 0:hd, :]
        a = a_ref[t, 0:hd, :]
        r = r_ref[t, 0:hd, :]
        kk = k * kk_ref[...]
        kk = kk * lax.rsqrt(jnp.maximum(jnp.sum(kk * kk, axis=0, keepdims=True), 1e-24))
        g_prev = g_scr[...]
        g = g_prev * w_ref[t, 0:hd, :]
        g_scr[...] = g
        ginv = 1.0 / g
        nkk_scr[slot] = -(kk * g_prev)
        b_scr[slot] = kk * a * ginv
        km = k * (1.0 + (a - 1.0) * ka_ref[...])
        km_scr[slot] = km * ginv
        rt_scr[slot] = r * g
        bon_scr[slot] = jnp.broadcast_to(jnp.sum(r * km * rk_ref[...], axis=0, keepdims=True), (8, LANES))

    def step(t, slot, has_next):
        nslot = 1 - slot
        if has_next:
            prep(t + 1, nslot)
        v = v_ref[t, 0:hd, :]
        sa = sa_scr[...]
        zero = jnp.zeros((hd, LANES), F32)

        def jbody(j, carry):
            o, sa_next = carry
            sj = s_scr[j] + sa * b_scr[slot, pl.ds(j, 1), :] + v * km_scr[slot, pl.ds(j, 1), :]
            s_scr[j] = sj
            o = o + sj * rt_scr[slot, pl.ds(j, 1), :]
            if has_next:
                sa_next = sa_next + sj * nkk_scr[nslot, pl.ds(j, 1), :]
            return o, sa_next

        o, sa_next = lax.fori_loop(0, hd, jbody, (zero, zero), unroll=SCAN_UNROLL)
        if has_next:
            sa_scr[...] = sa_next
        mu = jnp.mean(o, axis=0, keepdims=True)
        dlt = o - mu
        var = jnp.mean(dlt * dlt, axis=0, keepdims=True)
        on = dlt * lax.rsqrt(var + GN_EPS) * lg_ref[...] + lb_ref[...]
        o_ref[t, 0:hd, :] = on + bon_scr[slot, 0:1, :] * v
        if pad_rows:
            o_ref[t, hd:hd + pad_rows, :] = jnp.zeros((pad_rows, LANES), F32)

    g_scr[...] = jnp.ones((hd, LANES), F32)
    prep(0, 0)
    sa0 = s_scr[0] * nkk_scr[0, 0:1, :]
    for j in range(1, hd):
        sa0 = sa0 + s_scr[j] * nkk_scr[0, j:j + 1, :]
    sa_scr[...] = sa0

    if tb >= 2:
        def pair(p, carry):
            step(2 * p, 0, True)
            step(2 * p + 1, 1, True)
            return carry

        lax.fori_loop(0, tb // 2 - 1, pair, 0)
        step(tb - 2, 0, True)
        step(tb - 1, 1, False)
    else:
        step(0, 0, False)

    for j in range(hd):
        s_scr[j] = s_scr[j] * g_scr[j:j + 1, :]

    @pl.when(blk == pl.num_programs(1) - 1)
    def _():
        sout_ref[...] = s_scr[...]


def _rwkv_scan(r, w, k, v, a, s0, lw_c):
    t_total, rows, c = r.shape
    hd = HEAD_DIM
    tb = 32 if t_total % 32 == 0 else t_total
    assert tb == 1 or tb % 2 == 0
    nb = t_total // tb
    ncl = c // LANES
    seq = pl.BlockSpec((tb, rows, LANES), lambda cl, i: (i, 0, cl))
    st = pl.BlockSpec((hd, hd, LANES), lambda cl, i: (0, 0, cl))
    par = pl.BlockSpec((hd, LANES), lambda cl, i: (0, cl))
    return pl.pallas_call(
        functools.partial(_rwkv_scan_kernel, tb),
        grid=(ncl, nb),
        in_specs=[seq] * 5 + [st] + [par] * 5,
        out_specs=[seq, st],
        out_shape=[jax.ShapeDtypeStruct((t_total, rows, c), F32), jax.ShapeDtypeStruct((hd, hd, c), F32)],
        scratch_shapes=[pltpu.VMEM((hd, hd, LANES), F32)] + [pltpu.VMEM((hd, LANES), F32)] * 2
        + [pltpu.VMEM((2, hd, LANES), F32)] * 4 + [pltpu.VMEM((2, 8, LANES), F32)],
        compiler_params=_cparams("parallel", "arbitrary"),
        name="rwkv_scan",
    )(r, w, k, v, a, s0, lw_c['k_k'], lw_c['k_a'], lw_c['r_k'], lw_c['lnx_g'], lw_c['lnx_b'])


def _to_chain(x, bsz, heads):
    t = x.shape[0] // bsz
    return x.reshape(bsz, t, heads, HEAD_DIM).transpose(1, 3, 0, 2).reshape(t, HEAD_DIM, bsz * heads)


def _from_chain(x, bsz, heads):
    t = x.shape[0]
    return x.reshape(t, HEAD_DIM, bsz, heads).transpose(2, 0, 3, 1).reshape(bsz * t, heads * HEAD_DIM)


CHAIN_PITCH = HEAD_DIM + 8


def _chain_from_u_kernel(kind, u_ref, prev_ref, mu_ref, *rest):
    if kind == 'lerp':
        o_ref, y_scr, carry_scr = rest
    else:
        p0_ref, lora_ref, o_ref, y_scr, carry_scr = rest
    nb, t, wdt = u_ref.shape

    @pl.when(pl.program_id(0) == 0)
    def _():
        carry_scr[...] = prev_ref[:, 0, :]

    rows = lax.broadcasted_iota(jnp.int32, (t, wdt), 0)
    for b in range(nb):
        u = u_ref[b]
        prev = jnp.where(rows == 0, jnp.broadcast_to(carry_scr[b:b + 1, :], (t, wdt)), pltpu.roll(u, 1, axis=0))
        carry_scr[b:b + 1, :] = u[t - 1:t, :]
        sh = u + (prev - u) * mu_ref[...]
        if kind == 'lerp':
            x = sh
        else:
            lane = lax.broadcasted_iota(jnp.int32, sh.shape, 1)
            la = jnp.where(lane < LANES // 2, jnp.tanh(sh), sh).astype(BF16)
            pre = p0_ref[...] + jnp.dot(la, lora_ref[...], preferred_element_type=F32)
            if kind == 'decay':
                x = jnp.exp(-jnp.exp(-_softplus(-pre) - 0.5))
            else:
                x = jax.nn.sigmoid(pre)
        nk = x.shape[1] // LANES
        for k in range(nk):
            tt = x[:, k * LANES:(k + 1) * LANES].T
            m0 = b * 2 * nk + 2 * k
            y_scr[m0 * CHAIN_PITCH:m0 * CHAIN_PITCH + HEAD_DIM, :] = tt[:HEAD_DIM, :]
            y_scr[(m0 + 1) * CHAIN_PITCH:(m0 + 1) * CHAIN_PITCH + HEAD_DIM, :] = tt[HEAD_DIM:, :]
    for j in range(HEAD_DIM):
        g = y_scr[pl.ds(j, LANES, stride=CHAIN_PITCH), :]
        o_ref[pl.ds(j, LANES, stride=CHAIN_PITCH), :] = g.T
    for j in range(HEAD_DIM, CHAIN_PITCH):
        o_ref[pl.ds(j, LANES, stride=CHAIN_PITCH), :] = jnp.zeros((LANES, LANES), F32)


def _chain_from_u(kind, u_r, shift_prev, lw, bsz):
    m, cols = u_r.shape
    t_total = m // bsz
    d = lw['w0'].shape[1]
    if kind in ('r', 'k', 'v'):
        width, cb, kern_kind, extra = d, 'rkv'.index(kind), 'lerp', []
    else:
        width, cb, kern_kind = LANES, 3 * d // LANES, kind
        p0 = lw['w0'] if kind == 'decay' else lw['a0']
        half = 0 if kind == 'decay' else 1
        extra = [(p0, pl.BlockSpec((1, d), lambda i: (0, 0))),
                 (lw['lora_up'], pl.BlockSpec((LANES, d), lambda i: (0, half)))]
    args = [u_r.reshape(bsz, t_total, cols), shift_prev.reshape(bsz, 1, cols), lw['shift_mu']] + [a for a, _ in extra]
    in_specs = [pl.BlockSpec((bsz, LANES, width), lambda i: (0, i, cb)),
                pl.BlockSpec((bsz, 1, width), lambda i: (0, 0, cb)),
                pl.BlockSpec((1, width), lambda i: (0, cb))] + [s for _, s in extra]
    out = pl.pallas_call(
        functools.partial(_chain_from_u_kernel, kern_kind),
        grid=(t_total // LANES,),
        in_specs=in_specs,
        out_specs=pl.BlockSpec((LANES * CHAIN_PITCH, LANES), lambda i: (i, 0)),
        out_shape=jax.ShapeDtypeStruct((t_total * CHAIN_PITCH, LANES), F32),
        scratch_shapes=[pltpu.VMEM((LANES * CHAIN_PITCH, LANES), F32), pltpu.VMEM((bsz, width), F32)],
        compiler_params=_cparams("arbitrary"),
        name="chain_" + kern_kind,
    )(*args)
    return out.reshape(t_total, CHAIN_PITCH, LANES)


def _gate_kernel(u_ref, prev_ref, mu_ref, gup_ref, g_ref, carry_scr):
    t, wdt = u_ref.shape

    @pl.when(pl.program_id(1) == 0)
    def _():
        carry_scr[...] = jnp.broadcast_to(prev_ref[0], carry_scr.shape)

    u = u_ref[...]
    rows = lax.broadcasted_iota(jnp.int32, (t, wdt), 0)
    prev = jnp.where(rows == 0, jnp.broadcast_to(carry_scr[0:1, :], (t, wdt)), pltpu.roll(u, 1, axis=0))
    carry_scr[...] = jnp.broadcast_to(u[t - 1:t, :], carry_scr.shape)
    sh = u + (prev - u) * mu_ref[...]
    gl = jax.nn.sigmoid(sh[:, wdt - LANES:]).astype(BF16)
    g_ref[...] = jnp.dot(gl, gup_ref[...], preferred_element_type=F32)


def _rwkv_gate(u_r, shift_prev, lw, bsz):
    m, cols = u_r.shape
    d = lw['w0'].shape[1]
    seq = m // bsz
    t = _pick_tile(seq, 512)
    nblk = seq // t
    wdt = 2 * LANES
    cb = cols // wdt - 1
    assert cols % wdt == 0
    return pl.pallas_call(
        _gate_kernel,
        grid=(bsz, nblk),
        in_specs=[pl.BlockSpec((t, wdt), lambda b, c: (b * nblk + c, cb)),
                  pl.BlockSpec((1, 1, wdt), lambda b, c: (b, 0, cb)),
                  pl.BlockSpec((1, wdt), lambda b, c: (0, cb)),
                  pl.BlockSpec(lw['g_lora_up'].shape, lambda b, c: (0, 0))],
        out_specs=pl.BlockSpec((t, d), lambda b, c: (b * nblk + c, 0)),
        out_shape=jax.ShapeDtypeStruct((m, d), F32),
        scratch_shapes=[pltpu.VMEM((8, wdt), F32)],
        compiler_params=_cparams("parallel", "arbitrary"),
        name="rwkv_gate",
    )(u_r, shift_prev.reshape(bsz, 1, cols), lw['shift_mu'], lw['g_lora_up'])


def _from_chain_kernel(x_ref, o_ref, y_scr):
    nb = o_ref.shape[0]
    nk = o_ref.shape[2] // LANES
    for i in range(HEAD_DIM):
        g = x_ref[pl.ds(i, LANES, stride=CHAIN_PITCH), :]
        y_scr[pl.ds(i, LANES, stride=CHAIN_PITCH), :] = g.T
    for b in range(nb):
        for k in range(nk):
            m0 = b * 2 * nk + 2 * k
            tt = jnp.concatenate([y_scr[m0 * CHAIN_PITCH:m0 * CHAIN_PITCH + HEAD_DIM, :],
                                  y_scr[(m0 + 1) * CHAIN_PITCH:(m0 + 1) * CHAIN_PITCH + HEAD_DIM, :]], axis=0)
            o_ref[b, :, k * LANES:(k + 1) * LANES] = tt.T


def _from_chain_pallas(x, bsz, d):
    t_total = x.shape[0]
    out = pl.pallas_call(
        _from_chain_kernel,
        grid=(t_total // LANES,),
        in_specs=[pl.BlockSpec((LANES * CHAIN_PITCH, LANES), lambda i: (i, 0))],
        out_specs=pl.BlockSpec((bsz, LANES, d), lambda i: (0, i, 0)),
        out_shape=jax.ShapeDtypeStruct((bsz, t_total, d), F32),
        scratch_shapes=[pltpu.VMEM((LANES * CHAIN_PITCH, LANES), F32)],
        compiler_params=_cparams("parallel"),
        name="from_chain",
    )(x.reshape(t_total * CHAIN_PITCH, LANES))
    return out.reshape(bsz * t_total, d)


def _chain_param(p, bsz, heads):
    return jnp.tile(p.reshape(heads, HEAD_DIM).T, (1, bsz))


def _pad_lanes(x, c_pad):
    c = x.shape[-1]
    if c == c_pad:
        return x
    return jnp.pad(x, [(0, 0)] * (x.ndim - 1) + [(0, c_pad - c)])


def _rwkv_core(rwkva, s0, lw, bsz, chain_ready=False):
    d = lw['w0'].shape[1]
    heads = d // HEAD_DIM
    c = bsz * heads
    c_pad = -(-c // LANES) * LANES
    if chain_ready:
        chain = rwkva
    else:
        chain = [_pad_lanes(_to_chain(x, bsz, heads), c_pad) for x in rwkva]
    s0c = _pad_lanes(s0.transpose(3, 2, 0, 1).reshape(HEAD_DIM, HEAD_DIM, c), c_pad)
    lw_c = {n: _pad_lanes(_chain_param(lw[n], bsz, heads), c_pad) for n in ('k_k', 'k_a', 'r_k', 'lnx_g', 'lnx_b')}
    o, s = _rwkv_scan(*chain, s0c, lw_c)
    if chain_ready:
        o = _from_chain_pallas(o, bsz, d)
    else:
        o = _from_chain(o[..., :c], bsz, heads)
    s = s[..., :c].reshape(HEAD_DIM, HEAD_DIM, bsz, heads).transpose(2, 3, 1, 0)
    return o, s


def _prep_layer_weights(w, l):
    d = w['w_in'].shape[1]
    d_inner = w['p_ssm'].shape[1]
    heads_ssm = w['dt_bias'].shape[1]
    conv_dim = w['conv_w'].shape[2]
    rcols = w['shift_mu'].shape[1]
    dl = w['w_lora_up'].shape[1]
    s0, s1, s2 = d_inner, d_inner + conv_dim, d_inner + conv_dim + heads_ssm
    s3 = s2 + rcols
    w_in = w['w_in'][l]
    pad_h = lambda v, fill: jnp.pad(v, (0, LANES - heads_ssm), constant_values=fill).reshape(1, LANES)
    row = lambda v: v.reshape(1, -1)
    lora = jnp.zeros((LANES, 2 * d), F32)
    lora = lora.at[:dl, :d].set(w['w_lora_up'][l]).at[dl:dl + w['a_lora_up'].shape[1], d:].set(w['a_lora_up'][l])
    return {
        'w_zxd': jnp.pad(w_in[:, :s2], ((0, 0), (0, -s2 % MM_TN_CAP))).astype(BF16),
        'layer': l, 'd_inner': d_inner, 'w_ffn_in_all': w['w_ffn_in'],
        'w_dt_x': jnp.repeat(w_in[:, s1:s2], HEAD_DIM, axis=1).astype(BF16),
        'dt_bias_x': row(jnp.repeat(w['dt_bias'][l], HEAD_DIM)),
        'a_log_x': row(jnp.repeat(w['a_log'][l], HEAD_DIM)),
        'w_r': w_in[:, s2:s3].astype(BF16), 'w_gate': w_in[:, s3:].astype(BF16),
        'conv_w': w['conv_w'][l], 'conv_b': row(w['conv_b'][l]),
        'dt_bias': pad_h(w['dt_bias'][l], 0.0), 'a_log': pad_h(w['a_log'][l], 0.0),
        'd_skip': row(jnp.repeat(w['d_skip'][l], HEAD_DIM)),
        'ssm_norm_w': row(w['ssm_norm_w'][l]),
        'p_ssm': w['p_ssm'][l].astype(BF16),
        'shift_mu': row(w['shift_mu'][l]), 'w0': row(w['w0'][l]), 'a0': row(w['a0'][l]),
        'lora_up': lora.astype(BF16), 'g_lora_up': w['g_lora_up'][l].astype(BF16),
        'k_k': w['k_k'][l], 'k_a': w['k_a'][l], 'r_k': w['r_k'][l].reshape(-1),
        'lnx_g': w['lnx_g'][l], 'lnx_b': w['lnx_b'][l],
        'p_rwkv': w['p_rwkv'][l].astype(BF16), 'w_out': w['w_out'][l].astype(BF16),
        'ln1_g': row(w['ln1_g'][l]), 'ln1_b': row(w['ln1_b'][l]),
        'w_ffn_out': w['w_ffn_out'][l].astype(BF16),
        'ln2_g': row(w['ln2_g'][l]), 'ln2_b': row(w['ln2_b'][l]),
    }


def _layer_tail(alpha, x, xb, y_ssm, o_rwkv, g_rwkv, u_gate, lw):
    del xb
    x1, x1b = _merge_out_ln(alpha, y_ssm, o_rwkv, g_rwkv, u_gate, x, lw['p_ssm'], lw['p_rwkv'], lw['w_out'],
                            lw['ln1_g'], lw['ln1_b'])
    hmid = _ffn_in(x1b, lw['w_ffn_in_all'], lw['layer'])
    return _ffn_out_ln(alpha, hmid, lw['w_ffn_out'], x1, lw['ln2_g'], lw['ln2_b'])


def _in_proj(xb, lw, dt_expanded=False):
    dtx = _matmul(xb, lw['w_dt_x']) if dt_expanded else None
    return _matmul(xb, lw['w_zxd']), dtx, _matmul(xb, lw['w_r']), _matmul(xb, lw['w_gate'])


def _seq_layer(alpha, x, xb, ssm0, conv0, wkv0, shift0, lw, bsz, l_real):
    l_pad = x.shape[0] // bsz
    u_zxd, _, u_r, u_gate = _in_proj(xb, lw)
    h0 = ssm0.reshape(bsz, -1, SSM_STATE)
    y_ssm, h_new, conv_new = _ssd_chunked(u_zxd, h0, conv0, lw, bsz, l_real)
    heads_r = lw['w0'].shape[1] // HEAD_DIM
    if bsz * heads_r == LANES and l_real == l_pad:
        chain = [_chain_from_u(kind, u_r, shift0, lw, bsz) for kind in ('r', 'decay', 'k', 'v', 'rate')]
        g = _rwkv_gate(u_r, shift0, lw, bsz)
        shift_new = u_r.reshape(bsz, l_pad, -1)[:, -1]
        o, s_new = _rwkv_core(chain, wkv0, lw, bsz, chain_ready=True)
        x2, x2b = _layer_tail(alpha, x, xb, y_ssm, o, g, u_gate, lw)
        return x2, x2b, h_new.reshape(ssm0.shape), conv_new, s_new, shift_new
    (r, w, k, v, a, g), shift_new = _rwkv_prep(u_r, shift0, lw, bsz, True, l_real)
    if l_real < l_pad:
        cut = lambda t: t.reshape(bsz, l_pad, -1)[:, :l_real].reshape(bsz * l_real, -1)
        o, s_new = _rwkv_core([cut(t) for t in (r, w, k, v, a)], wkv0, lw, bsz)
        o = jnp.pad(o.reshape(bsz, l_real, -1), ((0, 0), (0, l_pad - l_real), (0, 0))).reshape(bsz * l_pad, -1)
    else:
        o, s_new = _rwkv_core([r, w, k, v, a], wkv0, lw, bsz)
    x2, x2b = _layer_tail(alpha, x, xb, y_ssm, o, g, u_gate, lw)
    return x2, x2b, h_new.reshape(ssm0.shape), conv_new, s_new, shift_new


def _step_layer(alpha, x, xb, ssm_all, conv_all, layer, h_acc, wkv0, shift0, lw):
    bsz = x.shape[0]
    u_zxd, u_dtx, u_r, u_gate = _in_proj(xb, lw, dt_expanded=True)
    u_z, u_xbc = u_zxd[:, :lw['d_inner']], u_zxd[:, lw['d_inner']:lw['d_inner'] + conv_all.shape[3]]
    y_ssm, h_acc, conv_new = _ssd_step(u_z, u_xbc, u_dtx, ssm_all, conv_all, layer, lw, h_acc)
    (r, w, k, v, a, g), shift_new = _rwkv_prep(u_r, shift0, lw, bsz, False)
    o, s_new = _rwkv_core([r, w, k, v, a], wkv0, lw, bsz)
    x2, x2b = _layer_tail(alpha, x, xb, y_ssm, o, g, u_gate, lw)
    return x2, x2b, h_acc, conv_new, s_new, shift_new


def kernel(x_prompt, x_sample, state_ssm, state_conv, state_wkv, state_shift, meta_tokens, w_in, conv_w, conv_b,
           dt_bias, a_log, d_skip, ssm_norm_w, p_ssm, shift_mu, w0, w_lora_up, a0, a_lora_up, g_lora_up, k_k,
           k_a, r_k, lnx_g, lnx_b, p_rwkv, w_out, ln1_g, ln1_b, w_ffn_in, w_ffn_out, ln2_g, ln2_b):
    weights = {
        'w_in': w_in, 'conv_w': conv_w, 'conv_b': conv_b, 'dt_bias': dt_bias, 'a_log': a_log,
        'd_skip': d_skip, 'ssm_norm_w': ssm_norm_w, 'p_ssm': p_ssm, 'shift_mu': shift_mu,
        'w0': w0, 'w_lora_up': w_lora_up, 'a0': a0, 'a_lora_up': a_lora_up, 'g_lora_up': g_lora_up,
        'k_k': k_k, 'k_a': k_a, 'r_k': r_k, 'lnx_g': lnx_g, 'lnx_b': lnx_b, 'p_rwkv': p_rwkv,
        'w_out': w_out, 'ln1_g': ln1_g, 'ln1_b': ln1_b, 'w_ffn_in': w_ffn_in,
        'w_ffn_out': w_ffn_out, 'ln2_g': ln2_g, 'ln2_b': ln2_b,
    }
    depth = w_in.shape[0]
    alpha = (2 * depth) ** 0.25
    bsz, seq, d = x_prompt.shape
    n_meta = meta_tokens.shape[0]
    layers = [_prep_layer_weights(weights, l) for l in range(depth)]

    xm = jnp.pad(meta_tokens.astype(F32), ((0, SSD_CHUNK - n_meta), (0, 0)))
    xp = x_prompt.reshape(bsz * seq, d)
    xs = x_sample.reshape(x_sample.shape[0], d)
    xmb, xpb, xsb = xm.astype(BF16), _to_bf16(xp), xs.astype(BF16)
    bcast = lambda t: jnp.broadcast_to(t, (bsz,) + t.shape[1:])
    ssm_p, conv_p, wkv_p, shift_p, conv_s, wkv_s, shift_s = [], [], [], [], [], [], []
    ssm_all = state_ssm.reshape(state_ssm.shape[:2] + (-1, SSM_STATE))
    ssm_s = None
    for l in range(depth):
        lw = layers[l]
        z = lambda a: jnp.zeros((1,) + a.shape[2:], F32)
        xm, xmb, hm, cm, sm, shm = _seq_layer(alpha, xm, xmb, z(state_ssm), z(state_conv), z(state_wkv),
                                              z(state_shift), lw, 1, n_meta)
        xp, xpb, h, c, s, sh = _seq_layer(alpha, xp, xpb, bcast(hm), bcast(cm), bcast(sm), bcast(shm), lw,
                                          bsz, seq)
        ssm_p.append(h), conv_p.append(c), wkv_p.append(s), shift_p.append(sh)
        xs, xsb, ssm_s, c, s, sh = _step_layer(alpha, xs, xsb, ssm_all, state_conv, l, ssm_s, state_wkv[l],
                                               state_shift[l], lw)
        conv_s.append(c), wkv_s.append(s), shift_s.append(sh)
    st = jnp.stack
    return (xp.reshape(bsz, seq, d), xs.reshape(x_sample.shape), st(ssm_p), st(conv_p), st(wkv_p), st(shift_p),
            ssm_s.reshape(state_ssm.shape), st(conv_s), st(wkv_s), st(shift_s))
```

```python
import functools
import math

import jax
import jax.numpy as jnp
from jax import lax
from jax.experimental import pallas as pl
from jax.experimental.pallas import tpu as pltpu

F32 = jnp.float32
BF16 = jnp.bfloat16

LANES = 128
N_META = 16
HEAD_DIM = 64
SSM_STATE = 128
SSM_GROUPS = 4
CONV_K = 4
SSD_CHUNK = 128
CONV_PITCH = 3
SCAN_UNROLL = 32
SSD_STEP_SEQS = 4
MM_TN_CAP = 14 * LANES
LOG2E = 1.4426950408889634
LN_EPS = 1e-5
RMS_EPS = 1e-5
GN_EPS = 64e-5
VMEM_LIMIT_BYTES = 56 * 1024 * 1024


def _cparams(*sem):
    return pltpu.CompilerParams(dimension_semantics=sem, vmem_limit_bytes=VMEM_LIMIT_BYTES)


def _pick_tile(n, cap):
    if n <= cap:
        return n
    best = LANES
    for t in range(LANES, cap + 1, LANES):
        if n % t == 0:
            best = t
    return best


def _silu(x):
    return x * jax.nn.sigmoid(x)


def _softplus(x):
    return jnp.maximum(x, 0.0) + jnp.log(1.0 + jnp.exp(-jnp.abs(x)))


def _layer_norm(y, g, b):
    mu = jnp.mean(y, axis=-1, keepdims=True)
    d = y - mu
    var = jnp.mean(d * d, axis=-1, keepdims=True)
    return d * lax.rsqrt(var + LN_EPS) * g + b


def _cast_kernel(x_ref, o_ref):
    o_ref[...] = x_ref[...].astype(o_ref.dtype)


def _to_bf16(x):
    m, d = x.shape
    tm = _pick_tile(m, 1024)
    return pl.pallas_call(
        _cast_kernel,
        grid=(m // tm,),
        in_specs=[pl.BlockSpec((tm, d), lambda i: (i, 0))],
        out_specs=pl.BlockSpec((tm, d), lambda i: (i, 0)),
        out_shape=jax.ShapeDtypeStruct((m, d), BF16),
        compiler_params=_cparams("parallel"),
        name="to_bf16",
    )(x)


def _mm_kernel(x_ref, w_ref, o_ref):
    o_ref[...] = jnp.dot(x_ref[...], w_ref[...], preferred_element_type=F32).astype(o_ref.dtype)


def _matmul(x, w, out_dtype=F32):
    m, k = x.shape
    n = w.shape[1]
    tm = _pick_tile(m, 1024)
    tn = _pick_tile(n, MM_TN_CAP)
    return pl.pallas_call(
        _mm_kernel,
        grid=(n // tn, m // tm),
        in_specs=[pl.BlockSpec((tm, k), lambda j, i: (i, 0)),
                  pl.BlockSpec((k, tn), lambda j, i: (0, j))],
        out_specs=pl.BlockSpec((tm, tn), lambda j, i: (i, j)),
        out_shape=jax.ShapeDtypeStruct((m, n), out_dtype),
        compiler_params=_cparams("parallel", "parallel"),
        name="matmul",
    )(x, w)


def _merge_kernel(alpha, ys_ref, yr_ref, g_ref, ug_ref, x_ref, ps_ref, pr_ref, wo_ref, lg_ref, lb_ref,
                  o_ref, ob_ref):
    d = o_ref.shape[1]
    a = jnp.dot(ys_ref[...], ps_ref[...], preferred_element_type=F32)
    yr = (yr_ref[...] * g_ref[...]).astype(BF16)
    b = jnp.dot(yr, pr_ref[...], preferred_element_type=F32)
    gates = jax.nn.sigmoid(ug_ref[...])
    merged = gates[:, :d] * a + gates[:, d:] * b
    y = alpha * x_ref[...] + jnp.dot(merged.astype(BF16), wo_ref[...], preferred_element_type=F32)
    out = _layer_norm(y, lg_ref[...], lb_ref[...])
    o_ref[...] = out
    ob_ref[...] = out.astype(BF16)


def _merge_out_ln(alpha, y_ssm, y_rwkv, g_rwkv, u_gate, x, p_ssm, p_rwkv, w_out, ln_g, ln_b):
    m, d = x.shape
    tm = _pick_tile(m, 512)
    row = lambda c: pl.BlockSpec((tm, c), lambda i: (i, 0))
    full = lambda a: pl.BlockSpec(a.shape, lambda i: (0, 0))
    return pl.pallas_call(
        functools.partial(_merge_kernel, alpha),
        grid=(m // tm,),
        in_specs=[row(y_ssm.shape[1]), row(d), row(d), row(2 * d), row(d),
                  full(p_ssm), full(p_rwkv), full(w_out), full(ln_g), full(ln_b)],
        out_specs=[row(d), row(d)],
        out_shape=[jax.ShapeDtypeStruct((m, d), F32), jax.ShapeDtypeStruct((m, d), BF16)],
        compiler_params=_cparams("parallel"),
        name="merge_out_ln",
    )(y_ssm, y_rwkv, g_rwkv, u_gate, x, p_ssm, p_rwkv, w_out, ln_g, ln_b)


def _swiglu_kernel(x_ref, wg_ref, wu_ref, o_ref, wgb_scr, wub_scr):
    @pl.when(pl.program_id(1) == 0)
    def _():
        wgb_scr[...] = wg_ref[...].astype(BF16)
        wub_scr[...] = wu_ref[...].astype(BF16)

    x = x_ref[...]
    hg = jnp.dot(x, wgb_scr[...], preferred_element_type=F32)
    hu = jnp.dot(x, wub_scr[...], preferred_element_type=F32)
    o_ref[...] = (_silu(hg) * hu).astype(o_ref.dtype)


def _ffn_in(x, w_ffn_in_all, layer):
    m, k = x.shape
    dff = w_ffn_in_all.shape[2] // 2
    tm = _pick_tile(m, 1024)
    tn = _pick_tile(dff, 1408)
    nj = dff // tn
    return pl.pallas_call(
        _swiglu_kernel,
        grid=(nj, m // tm),
        in_specs=[pl.BlockSpec((tm, k), lambda j, i: (i, 0)),
                  pl.BlockSpec((None, k, tn), lambda j, i: (layer, 0, j)),
                  pl.BlockSpec((None, k, tn), lambda j, i: (layer, 0, j + nj))],
        out_specs=pl.BlockSpec((tm, tn), lambda j, i: (i, j)),
        out_shape=jax.ShapeDtypeStruct((m, dff), BF16),
        scratch_shapes=[pltpu.VMEM((k, tn), BF16)] * 2,
        compiler_params=_cparams("parallel", "arbitrary"),
        name="ffn_in_swiglu",
    )(x, w_ffn_in_all, w_ffn_in_all)


def _ffn_out_kernel(alpha, h_ref, w_ref, x_ref, lg_ref, lb_ref, o_ref, ob_ref):
    y = alpha * x_ref[...] + jnp.dot(h_ref[...], w_ref[...], preferred_element_type=F32)
    out = _layer_norm(y, lg_ref[...], lb_ref[...])
    o_ref[...] = out
    ob_ref[...] = out.astype(BF16)


def _ffn_out_ln(alpha, h, w_ffn_out, x, ln_g, ln_b):
    m, d = x.shape
    k = h.shape[1]
    tm = _pick_tile(m, 512)
    row = lambda c: pl.BlockSpec((tm, c), lambda i: (i, 0))
    full = lambda a: pl.BlockSpec(a.shape, lambda i: (0, 0))
    return pl.pallas_call(
        functools.partial(_ffn_out_kernel, alpha),
        grid=(m // tm,),
        in_specs=[row(k), full(w_ffn_out), row(d), full(ln_g), full(ln_b)],
        out_specs=[row(d), row(d)],
        out_shape=[jax.ShapeDtypeStruct((m, d), F32), jax.ShapeDtypeStruct((m, d), BF16)],
        compiler_params=_cparams("parallel"),
        name="ffn_out_ln",
    )(h, w_ffn_out, x, ln_g, ln_b)


def _ssd_chunk_kernel(l_real, nchunks,
                      uzx_ref, udt_ref, h0_ref, cpre_ref, convw_ref, convb_ref, dtb_ref, alog_ref,
                      dskip_ref, normw_ref,
                      y_ref, hout_ref, ctail_ref,
                      ht_scr, cbuf_scr, xbc_scr, y_scr):
    q = SSD_CHUNK
    d_inner = y_ref.shape[1]
    gw = d_inner // SSM_GROUPS
    c = pl.program_id(1)

    @pl.when(c == 0)
    def _():
        for kb in range(d_inner // LANES):
            ht_scr[:, kb * LANES:(kb + 1) * LANES] = h0_ref[0, kb * LANES:(kb + 1) * LANES, :].T
        for ct in range(cbuf_scr.shape[0]):
            cbuf_scr[ct, pl.ds(5 * CONV_PITCH, CONV_K - 1, stride=CONV_PITCH), :] = \
                cpre_ref[0, :, ct * LANES:(ct + 1) * LANES]

    rows_at = lambda r0, n: pl.ds(r0 * CONV_PITCH, n, stride=CONV_PITCH)
    for ct in range(cbuf_scr.shape[0]):
        cols = slice(ct * LANES, (ct + 1) * LANES)
        cbuf_scr[ct, rows_at(8, q), :] = uzx_ref[:, d_inner + ct * LANES:d_inner + (ct + 1) * LANES]
        acc = convb_ref[:, cols] + cbuf_scr[ct, rows_at(5, q), :] * convw_ref[0:1, cols]
        for k in range(1, CONV_K):
            acc = acc + cbuf_scr[ct, rows_at(5 + k, q), :] * convw_ref[k:k + 1, cols]
        xbc_scr[:, cols] = _silu(acc)
        tail = cbuf_scr[ct, rows_at(l_real + 5, CONV_K - 1), :]
        cbuf_scr[ct, rows_at(5, CONV_K - 1), :] = tail

    rows = lax.broadcasted_iota(jnp.int32, (q, LANES), 0)
    dt = _softplus(udt_ref[...] + dtb_ref[...])
    if l_real < q:
        dt = jnp.where(rows < l_real, dt, 0.0)
    da = dt * (-jnp.exp(alog_ref[...]))
    acum = da
    s = 1
    while s < q:
        acum = acum + jnp.where(rows >= s, pltpu.roll(acum, s, axis=0), 0.0)
        s *= 2
    acum = acum * LOG2E
    acum_t = acum.T
    dt_t = dt.T
    a_last = acum[q - 1:q, :]
    st_t = dt_t * jnp.exp2(acum_t[:, q - 1:q] - acum_t)
    ii = lax.broadcasted_iota(jnp.int32, (q, q), 0)
    jj = lax.broadcasted_iota(jnp.int32, (q, q), 1)
    causal = ii >= jj
    low = lax.broadcasted_iota(jnp.int32, (q, LANES), 1) < HEAD_DIM

    for g in range(SSM_GROUPS):
        bm = xbc_scr[:, d_inner + g * SSM_STATE:d_inner + (g + 1) * SSM_STATE]
        cm = xbc_scr[:, d_inner + (SSM_GROUPS + g) * SSM_STATE:d_inner + (SSM_GROUPS + g + 1) * SSM_STATE]
        cmb = cm.astype(BF16)
        cb = lax.dot_general(cmb, bm.astype(BF16), (((1,), (1,)), ((), ())), preferred_element_type=F32)
        bm_t = bm.T
        for pr in range(gw // LANES):
            lanes = slice(g * gw + pr * LANES, g * gw + (pr + 1) * LANES)
            hd0 = (g * gw + pr * LANES) // HEAD_DIM
            x_pair = xbc_scr[:, lanes]
            xb = x_pair.astype(BF16)
            yo = jnp.dot(cmb, ht_scr[:, lanes].astype(BF16), preferred_element_type=F32)
            yd, st, acol, alast = [], [], [], []
            for hd in (hd0, hd0 + 1):
                a_col = jnp.broadcast_to(acum[:, hd:hd + 1], (q, q))
                lmat = jnp.exp2(jnp.where(causal, a_col - acum_t[hd:hd + 1, :], -jnp.inf))
                wd = (cb * lmat * dt_t[hd:hd + 1, :]).astype(BF16)
                yd.append(jnp.dot(wd, xb, preferred_element_type=F32))
                acol.append(a_col)
                st.append(jnp.dot((bm_t * st_t[hd:hd + 1, :]).astype(BF16), xb, preferred_element_type=F32))
                alast.append(jnp.broadcast_to(a_last[:, hd:hd + 1], (SSM_STATE, LANES)))
            ea = jnp.exp2(jnp.where(low, acol[0], acol[1]))
            y_scr[:, lanes] = jnp.where(low, yd[0], yd[1]) + yo * ea + x_pair * dskip_ref[:, lanes]
            cd = jnp.exp2(jnp.where(low, alast[0], alast[1]))
            ht_scr[:, lanes] = ht_scr[:, lanes] * cd + jnp.where(low, st[0], st[1])

    for g in range(SSM_GROUPS):
        cols = slice(g * gw, (g + 1) * gw)
        yg = y_scr[:, cols] * _silu(uzx_ref[:, cols])
        ms = jnp.mean(yg * yg, axis=-1, keepdims=True)
        y_ref[:, cols] = (yg * lax.rsqrt(ms + RMS_EPS) * normw_ref[:, cols]).astype(y_ref.dtype)

    @pl.when(c == nchunks - 1)
    def _():
        for kb in range(d_inner // LANES):
            hout_ref[0, kb * LANES:(kb + 1) * LANES, :] = ht_scr[:, kb * LANES:(kb + 1) * LANES].T
        for ct in range(cbuf_scr.shape[0]):
            ctail_ref[0, :, ct * LANES:(ct + 1) * LANES] = \
                cbuf_scr[ct, pl.ds(5 * CONV_PITCH, CONV_K - 1, stride=CONV_PITCH), :]


def _ssd_chunked(u_zxd, h0, conv_pre, lw, bsz, l_real):
    m = u_zxd.shape[0]
    conv_dim = conv_pre.shape[2]
    d_inner = lw['d_inner']
    q = SSD_CHUNK
    nchunks = m // bsz // q
    row = lambda cdim, cblk=0: pl.BlockSpec((q, cdim), lambda b, c: (b * nchunks + c, cblk))
    full = lambda a: pl.BlockSpec(a.shape, lambda b, c: (0, 0))
    per_b = lambda a: pl.BlockSpec((1,) + a.shape[1:], lambda b, c: (b, 0, 0))
    hshape = jax.ShapeDtypeStruct(h0.shape, F32)
    cshape = jax.ShapeDtypeStruct(conv_pre.shape, F32)
    return pl.pallas_call(
        functools.partial(_ssd_chunk_kernel, min(l_real, q), nchunks),
        grid=(bsz, nchunks),
        in_specs=[row(d_inner + conv_dim), row(LANES, (d_inner + conv_dim) // LANES), per_b(h0), per_b(conv_pre),
                  full(lw['conv_w']), full(lw['conv_b']), full(lw['dt_bias']), full(lw['a_log']),
                  full(lw['d_skip']), full(lw['ssm_norm_w'])],
        out_specs=[row(d_inner), per_b(h0), per_b(conv_pre)],
        out_shape=[jax.ShapeDtypeStruct((m, d_inner), BF16), hshape, cshape],
        scratch_shapes=[pltpu.VMEM((SSM_STATE, d_inner), F32),
                        pltpu.VMEM((conv_dim // LANES, (q + 8) * CONV_PITCH, LANES), F32),
                        pltpu.VMEM((q, conv_dim), F32),
                        pltpu.VMEM((q, d_inner), F32)],
        compiler_params=_cparams("parallel", "arbitrary"),
        name="ssd_chunk",
    )(u_zxd, u_zxd, h0, conv_pre, lw['conv_w'], lw['conv_b'], lw['dt_bias'], lw['a_log'],
      lw['d_skip'], lw['ssm_norm_w'])


def _row_to_col(row):
    r_i = lax.broadcasted_iota(jnp.int32, (LANES, LANES), 0)
    c_i = lax.broadcasted_iota(jnp.int32, (LANES, LANES), 1)
    return jnp.sum(jnp.where(r_i == c_i, jnp.broadcast_to(row, (LANES, LANES)), 0.0), axis=1, keepdims=True)


def _ssd_step_kernel(aliased, layer, uz_ref, uxbc_ref, udtx_ref, h0_ref, cpre_ref, convw_ref, convb_ref, dtbx_ref,
                     alogx_ref, dskip_ref, normw_ref, *rest):
    y_ref, hout_ref, ctail_ref, ht_scr = rest[1:] if aliased else rest
    nseq = uz_ref.shape[0]
    d_inner = uz_ref.shape[2]
    gw = d_inner // SSM_GROUPS
    slot = 0 if aliased else layer
    r_i = lax.broadcasted_iota(jnp.int32, (LANES, 3 * LANES), 0)
    c_i = lax.broadcasted_iota(jnp.int32, (LANES, 3 * LANES), 1)
    eye3 = jnp.where((c_i == r_i) | (c_i == r_i + LANES) | (c_i == r_i + 2 * LANES), 1.0, 0.0).astype(BF16)
    for s in range(nseq):
        u = uxbc_ref[s]
        pre = cpre_ref[0, s]
        acc = convb_ref[...] + u * convw_ref[CONV_K - 1:CONV_K, :]
        for k in range(CONV_K - 1):
            acc = acc + pre[k:k + 1, :] * convw_ref[k:k + 1, :]
        xbc = _silu(acc)
        ctail_ref[s, 0:CONV_K - 2, :] = pre[1:, :]
        ctail_ref[s, CONV_K - 2:CONV_K - 1, :] = u
        dt = _softplus(udtx_ref[s] + dtbx_ref[...])
        dec = jnp.exp(dt * (-jnp.exp(alogx_ref[...])))
        x_row = xbc[:, :d_inner]
        xdt = x_row * dt
        for kb in range(d_inner // LANES):
            ht_scr[s, :, kb * LANES:(kb + 1) * LANES] = h0_ref[0, s, kb * LANES:(kb + 1) * LANES, :].T
        y_parts = []
        for g in range(SSM_GROUPS):
            cols = slice(g * gw, (g + 1) * gw)
            bm = xbc[:, d_inner + g * SSM_STATE:d_inner + (g + 1) * SSM_STATE]
            cm = xbc[:, d_inner + (SSM_GROUPS + g) * SSM_STATE:d_inner + (SSM_GROUPS + g + 1) * SSM_STATE]
            hn = ht_scr[s, :, cols] * dec[:, cols] + _row_to_col(bm) * xdt[:, cols]
            ht_scr[s, :, cols] = hn
            y_parts.append(jnp.sum(hn * _row_to_col(cm), axis=0, keepdims=True))
        y_row = jnp.concatenate(y_parts, axis=1) + x_row * dskip_ref[...]
        y_row = y_row * _silu(uz_ref[s])
        outs = []
        for g in range(SSM_GROUPS):
            yg = y_row[:, g * gw:(g + 1) * gw]
            ms = jnp.mean(yg * yg, axis=-1, keepdims=True)
            outs.append(yg * lax.rsqrt(ms + RMS_EPS))
        y_ref[s] = (jnp.concatenate(outs, axis=1) * normw_ref[...]).astype(y_ref.dtype)
        for kb in range(d_inner // LANES):
            a = ht_scr[s, :, kb * LANES:(kb + 1) * LANES]
            hi = a.astype(BF16)
            rest1 = a - hi.astype(F32)
            mid = rest1.astype(BF16)
            lo = (rest1 - mid.astype(F32)).astype(BF16)
            pieces = jnp.concatenate([hi, mid, lo], axis=1)
            hout_ref[slot, s, kb * LANES:(kb + 1) * LANES, :] = lax.dot_general(
                eye3, pieces, (((1,), (1,)), ((), ())), preferred_element_type=F32)
        if not aliased:
            for other in range(hout_ref.shape[0]):
                if other != layer:
                    hout_ref[other, s] = jnp.zeros(hout_ref.shape[2:], F32)


def _ssd_step(u_z, u_xbc, u_dtx, ssm_all, conv_all, layer, lw, h_acc):
    bsz, d_inner = u_z.shape
    ns = SSD_STEP_SEQS if bsz % SSD_STEP_SEQS == 0 else 1
    r3 = lambda a: a.reshape(bsz, 1, a.shape[1])
    per_b = lambda a: pl.BlockSpec((ns,) + a.shape[1:], lambda b: (b, 0, 0))
    per_lb = lambda a: pl.BlockSpec((1, ns) + a.shape[2:], lambda b: (layer, b, 0, 0))
    full = lambda a: pl.BlockSpec(a.shape, lambda b: (0, 0))
    uz3, ux3, ud3 = r3(u_z), r3(u_xbc), r3(u_dtx)
    cshape = conv_all.shape[1:]
    args = [uz3, ux3, ud3, ssm_all, conv_all, lw['conv_w'], lw['conv_b'], lw['dt_bias_x'], lw['a_log_x'],
            lw['d_skip'], lw['ssm_norm_w']]
    in_specs = [per_b(uz3), per_b(ux3), per_b(ud3), per_lb(ssm_all), per_lb(conv_all)]
    in_specs += [full(a) for a in args[5:]]
    aliases = {}
    h_spec = pl.BlockSpec((ssm_all.shape[0], ns) + ssm_all.shape[2:], lambda b: (0, b, 0, 0))
    if h_acc is not None:
        args.append(h_acc)
        in_specs.append(pl.BlockSpec(memory_space=pl.ANY))
        aliases = {len(args) - 1: 1}
        h_spec = per_lb(ssm_all)
    y, h, ct = pl.pallas_call(
        functools.partial(_ssd_step_kernel, h_acc is not None, layer),
        grid=(bsz // ns,),
        in_specs=in_specs,
        out_specs=[per_b(uz3), h_spec, pl.BlockSpec((ns,) + cshape[1:], lambda b: (b, 0, 0))],
        out_shape=[jax.ShapeDtypeStruct((bsz, 1, d_inner), BF16), jax.ShapeDtypeStruct(ssm_all.shape, F32),
                   jax.ShapeDtypeStruct(cshape, F32)],
        scratch_shapes=[pltpu.VMEM((ns, SSM_STATE, d_inner), F32)],
        input_output_aliases=aliases,
        compiler_params=_cparams("arbitrary"),
        name="ssd_step",
    )(*args)
    return y.reshape(bsz, d_inner), h, ct


def _rwkv_prep_kernel(seq_mode, l_real, ur_ref, prev_ref, mu_ref, w0_ref, a0_ref, lora_ref, gup_ref,
                      r_ref, w_ref, k_ref, v_ref, a_ref, g_ref, last_ref, carry_scr):
    t, cols = ur_ref.shape
    d = r_ref.shape[1]
    u = ur_ref[...]
    if seq_mode:
        c = pl.program_id(1)

        @pl.when(c == 0)
        def _():
            carry_scr[...] = jnp.broadcast_to(prev_ref[0], carry_scr.shape)

        rows = lax.broadcasted_iota(jnp.int32, (t, cols), 0)
        prev = jnp.where(rows == 0, jnp.broadcast_to(carry_scr[0:1, :], (t, cols)), pltpu.roll(u, 1, axis=0))
        lrow = (l_real - 1) % t
        carry_scr[...] = jnp.broadcast_to(u[lrow:lrow + 1, :], carry_scr.shape)
        last_ref[0] = u[lrow:lrow + 1, :]
    else:
        prev = prev_ref[...]
        last_ref[...] = u
    sh = u + (prev - u) * mu_ref[...]
    r_ref[...] = sh[:, 0:d]
    k_ref[...] = sh[:, d:2 * d]
    v_ref[...] = sh[:, 2 * d:3 * d]
    la = sh[:, 3 * d:3 * d + LANES]
    lane = lax.broadcasted_iota(jnp.int32, la.shape, 1)
    la = jnp.where(lane < LANES // 2, jnp.tanh(la), la).astype(BF16)
    pre = jnp.dot(la, lora_ref[...], preferred_element_type=F32)
    w_log = -_softplus(-(w0_ref[...] + pre[:, :d])) - 0.5
    w_ref[...] = jnp.exp(-jnp.exp(w_log))
    a_ref[...] = jax.nn.sigmoid(a0_ref[...] + pre[:, d:])
    gl = jax.nn.sigmoid(sh[:, 3 * d + LANES:]).astype(BF16)
    g_ref[...] = jnp.dot(gl, gup_ref[...], preferred_element_type=F32)


def _rwkv_prep(u_r, shift_prev, lw, bsz, seq_mode, l_real=None):
    m, cols = u_r.shape
    d = lw['w0'].shape[1]
    full = lambda a: pl.BlockSpec(a.shape, lambda *_: (0,) * a.ndim)
    outs = [jax.ShapeDtypeStruct((m, d), F32)] * 6
    args = (lw['shift_mu'], lw['w0'], lw['a0'], lw['lora_up'], lw['g_lora_up'])
    if seq_mode:
        seq = m // bsz
        t = _pick_tile(seq, 256)
        nblk = seq // t
        l_real = seq if l_real is None else l_real
        row = lambda cdim: pl.BlockSpec((t, cdim), lambda b, c: (b * nblk + c, 0))
        prev3 = shift_prev.reshape(bsz, 1, cols)
        per_b = pl.BlockSpec((1, 1, cols), lambda b, c: (b, 0, 0))
        res = pl.pallas_call(
            functools.partial(_rwkv_prep_kernel, True, l_real),
            grid=(bsz, nblk),
            in_specs=[row(cols), per_b] + [full(a) for a in args],
            out_specs=[row(d)] * 6 + [per_b],
            out_shape=outs + [jax.ShapeDtypeStruct((bsz, 1, cols), F32)],
            scratch_shapes=[pltpu.VMEM((8, cols), F32)],
            compiler_params=_cparams("parallel", "arbitrary"),
            name="rwkv_prep_seq",
        )(u_r, prev3, *args)
        return res[:6], res[6].reshape(bsz, cols)
    t = _pick_tile(m, 256)
    row = lambda cdim: pl.BlockSpec((t, cdim), lambda i: (i, 0))
    res = pl.pallas_call(
        functools.partial(_rwkv_prep_kernel, False, None),
        grid=(m // t,),
        in_specs=[row(cols), row(cols)] + [full(a) for a in args],
        out_specs=[row(d)] * 6 + [row(cols)],
        out_shape=outs + [jax.ShapeDtypeStruct((m, cols), F32)],
        scratch_shapes=[pltpu.VMEM((8, cols), F32)],
        compiler_params=_cparams("parallel"),
        name="rwkv_prep_step",
    )(u_r, shift_prev, *args)
    return res[:6], res[6]


def _rwkv_scan_kernel(tb, r_ref, w_ref, k_ref, v_ref, a_ref, s0_ref, kk_ref, ka_ref, rk_ref, lg_ref, lb_ref,
                      o_ref, sout_ref, s_scr, g_scr, sa_scr, nkk_scr, b_scr, km_scr, rt_scr, bon_scr):
    hd = HEAD_DIM
    blk = pl.program_id(1)
    pad_rows = o_ref.shape[1] - hd

    @pl.when(blk == 0)
    def _():
        s_scr[...] = s0_ref[...]

    def prep(t, slot):
        k = k_ref[t, 0:hd, :]
        a = a_ref[t, 0:hd, :]
        r = r_ref[t, 0:hd, :]
        kk = k * kk_ref[...]
        kk = kk * lax.rsqrt(jnp.maximum(jnp.sum(kk * kk, axis=0, keepdims=True), 1e-24))
        g_prev = g_scr[...]
        g = g_prev * w_ref[t, 0:hd, :]
        g_scr[...] = g
        ginv = 1.0 / g
        nkk_scr[slot] = -(kk * g_prev)
        b_scr[slot] = kk * a * ginv
        km = k * (1.0 + (a - 1.0) * ka_ref[...])
        km_scr[slot] = km * ginv
        rt_scr[slot] = r * g
        bon_scr[slot] = jnp.broadcast_to(jnp.sum(r * km * rk_ref[...], axis=0, keepdims=True), (8, LANES))

    def step(t, slot, has_next):
        nslot = 1 - slot
        if has_next:
            prep(t + 1, nslot)
        v = v_ref[t, 0:hd, :]
        sa = sa_scr[...]
        zero = jnp.zeros((hd, LANES), F32)

        def jbody(j, carry):
            o, sa_next = carry
            sj = s_scr[j] + sa * b_scr[slot, pl.ds(j, 1), :] + v * km_scr[slot, pl.ds(j, 1), :]
            s_scr[j] = sj
            o = o + sj * rt_scr[slot, pl.ds(j, 1), :]
            if has_next:
                sa_next = sa_next + sj * nkk_scr[nslot, pl.ds(j, 1), :]
            return o, sa_next

        o, sa_next = lax.fori_loop(0, hd, jbody, (zero, zero), unroll=SCAN_UNROLL)
        if has_next:
            sa_scr[...] = sa_next
        mu = jnp.mean(o, axis=0, keepdims=True)
        dlt = o - mu
        var = jnp.mean(dlt * dlt, axis=0, keepdims=True)
        on = dlt * lax.rsqrt(var + GN_EPS) * lg_ref[...] + lb_ref[...]
        o_ref[t, 0:hd, :] = on + bon_scr[slot, 0:1, :] * v
        if pad_rows:
            o_ref[t, hd:hd + pad_rows, :] = jnp.zeros((pad_rows, LANES), F32)

    g_scr[...] = jnp.ones((hd, LANES), F32)
    prep(0, 0)
    sa0 = s_scr[0] * nkk_scr[0, 0:1, :]
    for j in range(1, hd):
        sa0 = sa0 + s_scr[j] * nkk_scr[0, j:j + 1, :]
    sa_scr[...] = sa0

    if tb >= 2:
        def pair(p, carry):
            step(2 * p, 0, True)
            step(2 * p + 1, 1, True)
            return carry

        lax.fori_loop(0, tb // 2 - 1, pair, 0)
        step(tb - 2, 0, True)
        step(tb - 1, 1, False)
    else:
        step(0, 0, False)

    for j in range(hd):
        s_scr[j] = s_scr[j] * g_scr[j:j + 1, :]

    @pl.when(blk == pl.num_programs(1) - 1)
    def _():
        sout_ref[...] = s_scr[...]


def _rwkv_scan(r, w, k, v, a, s0, lw_c):
    t_total, rows, c = r.shape
    hd = HEAD_DIM
    tb = 32 if t_total % 32 == 0 else t_total
    assert tb == 1 or tb % 2 == 0
    nb = t_total // tb
    ncl = c // LANES
    seq = pl.BlockSpec((tb, rows, LANES), lambda cl, i: (i, 0, cl))
    st = pl.BlockSpec((hd, hd, LANES), lambda cl, i: (0, 0, cl))
    par = pl.BlockSpec((hd, LANES), lambda cl, i: (0, cl))
    return pl.pallas_call(
        functools.partial(_rwkv_scan_kernel, tb),
        grid=(ncl, nb),
        in_specs=[seq] * 5 + [st] + [par] * 5,
        out_specs=[seq, st],
        out_shape=[jax.ShapeDtypeStruct((t_total, rows, c), F32), jax.ShapeDtypeStruct((hd, hd, c), F32)],
        scratch_shapes=[pltpu.VMEM((hd, hd, LANES), F32)] + [pltpu.VMEM((hd, LANES), F32)] * 2
        + [pltpu.VMEM((2, hd, LANES), F32)] * 4 + [pltpu.VMEM((2, 8, LANES), F32)],
        compiler_params=_cparams("parallel", "arbitrary"),
        name="rwkv_scan",
    )(r, w, k, v, a, s0, lw_c['k_k'], lw_c['k_a'], lw_c['r_k'], lw_c['lnx_g'], lw_c['lnx_b'])


def _to_chain(x, bsz, heads):
    t = x.shape[0] // bsz
    return x.reshape(bsz, t, heads, HEAD_DIM).transpose(1, 3, 0, 2).reshape(t, HEAD_DIM, bsz * heads)


def _from_chain(x, bsz, heads):
    t = x.shape[0]
    return x.reshape(t, HEAD_DIM, bsz, heads).transpose(2, 0, 3, 1).reshape(bsz * t, heads * HEAD_DIM)


CHAIN_PITCH = HEAD_DIM + 8


def _chain_from_u_kernel(kind, u_ref, prev_ref, mu_ref, *rest):
    if kind == 'lerp':
        o_ref, y_scr, carry_scr = rest
    else:
        p0_ref, lora_ref, o_ref, y_scr, carry_scr = rest
    nb, t, wdt = u_ref.shape

    @pl.when(pl.program_id(0) == 0)
    def _():
        carry_scr[...] = prev_ref[:, 0, :]

    rows = lax.broadcasted_iota(jnp.int32, (t, wdt), 0)
    for b in range(nb):
        u = u_ref[b]
        prev = jnp.where(rows == 0, jnp.broadcast_to(carry_scr[b:b + 1, :], (t, wdt)), pltpu.roll(u, 1, axis=0))
        carry_scr[b:b + 1, :] = u[t - 1:t, :]
        sh = u + (prev - u) * mu_ref[...]
        if kind == 'lerp':
            x = sh
        else:
            lane = lax.broadcasted_iota(jnp.int32, sh.shape, 1)
            la = jnp.where(lane < LANES // 2, jnp.tanh(sh), sh).astype(BF16)
            pre = p0_ref[...] + jnp.dot(la, lora_ref[...], preferred_element_type=F32)
            if kind == 'decay':
                x = jnp.exp(-jnp.exp(-_softplus(-pre) - 0.5))
            else:
                x = jax.nn.sigmoid(pre)
        nk = x.shape[1] // LANES
        for k in range(nk):
            tt = x[:, k * LANES:(k + 1) * LANES].T
            m0 = b * 2 * nk + 2 * k
            y_scr[m0 * CHAIN_PITCH:m0 * CHAIN_PITCH + HEAD_DIM, :] = tt[:HEAD_DIM, :]
            y_scr[(m0 + 1) * CHAIN_PITCH:(m0 + 1) * CHAIN_PITCH + HEAD_DIM, :] = tt[HEAD_DIM:, :]
    for j in range(HEAD_DIM):
        g = y_scr[pl.ds(j, LANES, stride=CHAIN_PITCH), :]
        o_ref[pl.ds(j, LANES, stride=CHAIN_PITCH), :] = g.T
    for j in range(HEAD_DIM, CHAIN_PITCH):
        o_ref[pl.ds(j, LANES, stride=CHAIN_PITCH), :] = jnp.zeros((LANES, LANES), F32)


def _chain_from_u(kind, u_r, shift_prev, lw, bsz):
    m, cols = u_r.shape
    t_total = m // bsz
    d = lw['w0'].shape[1]
    if kind in ('r', 'k', 'v'):
        width, cb, kern_kind, extra = d, 'rkv'.index(kind), 'lerp', []
    else:
        width, cb, kern_kind = LANES, 3 * d // LANES, kind
        p0 = lw['w0'] if kind == 'decay' else lw['a0']
        half = 0 if kind == 'decay' else 1
        extra = [(p0, pl.BlockSpec((1, d), lambda i: (0, 0))),
                 (lw['lora_up'], pl.BlockSpec((LANES, d), lambda i: (0, half)))]
    args = [u_r.reshape(bsz, t_total, cols), shift_prev.reshape(bsz, 1, cols), lw['shift_mu']] + [a for a, _ in extra]
    in_specs = [pl.BlockSpec((bsz, LANES, width), lambda i: (0, i, cb)),
                pl.BlockSpec((bsz, 1, width), lambda i: (0, 0, cb)),
                pl.BlockSpec((1, width), lambda i: (0, cb))] + [s for _, s in extra]
    out = pl.pallas_call(
        functools.partial(_chain_from_u_kernel, kern_kind),
        grid=(t_total // LANES,),
        in_specs=in_specs,
        out_specs=pl.BlockSpec((LANES * CHAIN_PITCH, LANES), lambda i: (i, 0)),
        out_shape=jax.ShapeDtypeStruct((t_total * CHAIN_PITCH, LANES), F32),
        scratch_shapes=[pltpu.VMEM((LANES * CHAIN_PITCH, LANES), F32), pltpu.VMEM((bsz, width), F32)],
        compiler_params=_cparams("arbitrary"),
        name="chain_" + kern_kind,
    )(*args)
    return out.reshape(t_total, CHAIN_PITCH, LANES)


def _gate_kernel(u_ref, prev_ref, mu_ref, gup_ref, g_ref, carry_scr):
    t, wdt = u_ref.shape

    @pl.when(pl.program_id(1) == 0)
    def _():
        carry_scr[...] = jnp.broadcast_to(prev_ref[0], carry_scr.shape)

    u = u_ref[...]
    rows = lax.broadcasted_iota(jnp.int32, (t, wdt), 0)
    prev = jnp.where(rows == 0, jnp.broadcast_to(carry_scr[0:1, :], (t, wdt)), pltpu.roll(u, 1, axis=0))
    carry_scr[...] = jnp.broadcast_to(u[t - 1:t, :], carry_scr.shape)
    sh = u + (prev - u) * mu_ref[...]
    gl = jax.nn.sigmoid(sh[:, wdt - LANES:]).astype(BF16)
    g_ref[...] = jnp.dot(gl, gup_ref[...], preferred_element_type=F32)


def _rwkv_gate(u_r, shift_prev, lw, bsz):
    m, cols = u_r.shape
    d = lw['w0'].shape[1]
    seq = m // bsz
    t = _pick_tile(seq, 512)
    nblk = seq // t
    wdt = 2 * LANES
    cb = cols // wdt - 1
    assert cols % wdt == 0
    return pl.pallas_call(
        _gate_kernel,
        grid=(bsz, nblk),
        in_specs=[pl.BlockSpec((t, wdt), lambda b, c: (b * nblk + c, cb)),
                  pl.BlockSpec((1, 1, wdt), lambda b, c: (b, 0, cb)),
                  pl.BlockSpec((1, wdt), lambda b, c: (0, cb)),
                  pl.BlockSpec(lw['g_lora_up'].shape, lambda b, c: (0, 0))],
        out_specs=pl.BlockSpec((t, d), lambda b, c: (b * nblk + c, 0)),
        out_shape=jax.ShapeDtypeStruct((m, d), F32),
        scratch_shapes=[pltpu.VMEM((8, wdt), F32)],
        compiler_params=_cparams("parallel", "arbitrary"),
        name="rwkv_gate",
    )(u_r, shift_prev.reshape(bsz, 1, cols), lw['shift_mu'], lw['g_lora_up'])


def _from_chain_kernel(x_ref, o_ref, y_scr):
    nb = o_ref.shape[0]
    nk = o_ref.shape[2] // LANES
    for i in range(HEAD_DIM):
        g = x_ref[pl.ds(i, LANES, stride=CHAIN_PITCH), :]
        y_scr[pl.ds(i, LANES, stride=CHAIN_PITCH), :] = g.T
    for b in range(nb):
        for k in range(nk):
            m0 = b * 2 * nk + 2 * k
            tt = jnp.concatenate([y_scr[m0 * CHAIN_PITCH:m0 * CHAIN_PITCH + HEAD_DIM, :],
                                  y_scr[(m0 + 1) * CHAIN_PITCH:(m0 + 1) * CHAIN_PITCH + HEAD_DIM, :]], axis=0)
            o_ref[b, :, k * LANES:(k + 1) * LANES] = tt.T


def _from_chain_pallas(x, bsz, d):
    t_total = x.shape[0]
    out = pl.pallas_call(
        _from_chain_kernel,
        grid=(t_total // LANES,),
        in_specs=[pl.BlockSpec((LANES * CHAIN_PITCH, LANES), lambda i: (i, 0))],
        out_specs=pl.BlockSpec((bsz, LANES, d), lambda i: (0, i, 0)),
        out_shape=jax.ShapeDtypeStruct((bsz, t_total, d), F32),
        scratch_shapes=[pltpu.VMEM((LANES * CHAIN_PITCH, LANES), F32)],
        compiler_params=_cparams("parallel"),
        name="from_chain",
    )(x.reshape(t_total * CHAIN_PITCH, LANES))
    return out.reshape(bsz * t_total, d)


def _chain_param(p, bsz, heads):
    return jnp.tile(p.reshape(heads, HEAD_DIM).T, (1, bsz))


def _pad_lanes(x, c_pad):
    c = x.shape[-1]
    if c == c_pad:
        return x
    return jnp.pad(x, [(0, 0)] * (x.ndim - 1) + [(0, c_pad - c)])


def _rwkv_core(rwkva, s0, lw, bsz, chain_ready=False):
    d = lw['w0'].shape[1]
    heads = d // HEAD_DIM
    c = bsz * heads
    c_pad = -(-c // LANES) * LANES
    if chain_ready:
        chain = rwkva
    else:
        chain = [_pad_lanes(_to_chain(x, bsz, heads), c_pad) for x in rwkva]
    s0c = _pad_lanes(s0.transpose(3, 2, 0, 1).reshape(HEAD_DIM, HEAD_DIM, c), c_pad)
    lw_c = {n: _pad_lanes(_chain_param(lw[n], bsz, heads), c_pad) for n in ('k_k', 'k_a', 'r_k', 'lnx_g', 'lnx_b')}
    o, s = _rwkv_scan(*chain, s0c, lw_c)
    if chain_ready:
        o = _from_chain_pallas(o, bsz, d)
    else:
        o = _from_chain(o[..., :c], bsz, heads)
    s = s[..., :c].reshape(HEAD_DIM, HEAD_DIM, bsz, heads).transpose(2, 3, 1, 0)
    return o, s


def _prep_layer_weights(w, l):
    d = w['w_in'].shape[1]
    d_inner = w['p_ssm'].shape[1]
    heads_ssm = w['dt_bias'].shape[1]
    conv_dim = w['conv_w'].shape[2]
    rcols = w['shift_mu'].shape[1]
    dl = w['w_lora_up'].shape[1]
    s0, s1, s2 = d_inner, d_inner + conv_dim, d_inner + conv_dim + heads_ssm
    s3 = s2 + rcols
    w_in = w['w_in'][l]
    pad_h = lambda v, fill: jnp.pad(v, (0, LANES - heads_ssm), constant_values=fill).reshape(1, LANES)
    row = lambda v: v.reshape(1, -1)
    lora = jnp.zeros((LANES, 2 * d), F32)
    lora = lora.at[:dl, :d].set(w['w_lora_up'][l]).at[dl:dl + w['a_lora_up'].shape[1], d:].set(w['a_lora_up'][l])
    return {
        'w_zxd': jnp.pad(w_in[:, :s2], ((0, 0), (0, -s2 % MM_TN_CAP))).astype(BF16),
        'layer': l, 'd_inner': d_inner, 'w_ffn_in_all': w['w_ffn_in'],
        'w_dt_x': jnp.repeat(w_in[:, s1:s2], HEAD_DIM, axis=1).astype(BF16),
        'dt_bias_x': row(jnp.repeat(w['dt_bias'][l], HEAD_DIM)),
        'a_log_x': row(jnp.repeat(w['a_log'][l], HEAD_DIM)),
        'w_r': w_in[:, s2:s3].astype(BF16), 'w_gate': w_in[:, s3:].astype(BF16),
        'conv_w': w['conv_w'][l], 'conv_b': row(w['conv_b'][l]),
        'dt_bias': pad_h(w['dt_bias'][l], 0.0), 'a_log': pad_h(w['a_log'][l], 0.0),
        'd_skip': row(jnp.repeat(w['d_skip'][l], HEAD_DIM)),
        'ssm_norm_w': row(w['ssm_norm_w'][l]),
        'p_ssm': w['p_ssm'][l].astype(BF16),
        'shift_mu': row(w['shift_mu'][l]), 'w0': row(w['w0'][l]), 'a0': row(w['a0'][l]),
        'lora_up': lora.astype(BF16), 'g_lora_up': w['g_lora_up'][l].astype(BF16),
        'k_k': w['k_k'][l], 'k_a': w['k_a'][l], 'r_k': w['r_k'][l].reshape(-1),
        'lnx_g': w['lnx_g'][l], 'lnx_b': w['lnx_b'][l],
        'p_rwkv': w['p_rwkv'][l].astype(BF16), 'w_out': w['w_out'][l].astype(BF16),
        'ln1_g': row(w['ln1_g'][l]), 'ln1_b': row(w['ln1_b'][l]),
        'w_ffn_out': w['w_ffn_out'][l].astype(BF16),
        'ln2_g': row(w['ln2_g'][l]), 'ln2_b': row(w['ln2_b'][l]),
    }


def _layer_tail(alpha, x, xb, y_ssm, o_rwkv, g_rwkv, u_gate, lw):
    del xb
    x1, x1b = _merge_out_ln(alpha, y_ssm, o_rwkv, g_rwkv, u_gate, x, lw['p_ssm'], lw['p_rwkv'], lw['w_out'],
                            lw['ln1_g'], lw['ln1_b'])
    hmid = _ffn_in(x1b, lw['w_ffn_in_all'], lw['layer'])
    return _ffn_out_ln(alpha, hmid, lw['w_ffn_out'], x1, lw['ln2_g'], lw['ln2_b'])


def _in_proj(xb, lw, dt_expanded=False):
    dtx = _matmul(xb, lw['w_dt_x']) if dt_expanded else None
    return _matmul(xb, lw['w_zxd']), dtx, _matmul(xb, lw['w_r']), _matmul(xb, lw['w_gate'])


def _seq_layer(alpha, x, xb, ssm0, conv0, wkv0, shift0, lw, bsz, l_real):
    l_pad = x.shape[0] // bsz
    u_zxd, _, u_r, u_gate = _in_proj(xb, lw)
    h0 = ssm0.reshape(bsz, -1, SSM_STATE)
    y_ssm, h_new, conv_new = _ssd_chunked(u_zxd, h0, conv0, lw, bsz, l_real)
    heads_r = lw['w0'].shape[1] // HEAD_DIM
    if bsz * heads_r == LANES and l_real == l_pad:
        chain = [_chain_from_u(kind, u_r, shift0, lw, bsz) for kind in ('r', 'decay', 'k', 'v', 'rate')]
        g = _rwkv_gate(u_r, shift0, lw, bsz)
        shift_new = u_r.reshape(bsz, l_pad, -1)[:, -1]
        o, s_new = _rwkv_core(chain, wkv0, lw, bsz, chain_ready=True)
        x2, x2b = _layer_tail(alpha, x, xb, y_ssm, o, g, u_gate, lw)
        return x2, x2b, h_new.reshape(ssm0.shape), conv_new, s_new, shift_new
    (r, w, k, v, a, g), shift_new = _rwkv_prep(u_r, shift0, lw, bsz, True, l_real)
    if l_real < l_pad:
        cut = lambda t: t.reshape(bsz, l_pad, -1)[:, :l_real].reshape(bsz * l_real, -1)
        o, s_new = _rwkv_core([cut(t) for t in (r, w, k, v, a)], wkv0, lw, bsz)
        o = jnp.pad(o.reshape(bsz, l_real, -1), ((0, 0), (0, l_pad - l_real), (0, 0))).reshape(bsz * l_pad, -1)
    else:
        o, s_new = _rwkv_core([r, w, k, v, a], wkv0, lw, bsz)
    x2, x2b = _layer_tail(alpha, x, xb, y_ssm, o, g, u_gate, lw)
    return x2, x2b, h_new.reshape(ssm0.shape), conv_new, s_new, shift_new


def _step_layer(alpha, x, xb, ssm_all, conv_all, layer, h_acc, wkv0, shift0, lw):
    bsz = x.shape[0]
    u_zxd, u_dtx, u_r, u_gate = _in_proj(xb, lw, dt_expanded=True)
    u_z, u_xbc = u_zxd[:, :lw['d_inner']], u_zxd[:, lw['d_inner']:lw['d_inner'] + conv_all.shape[3]]
    y_ssm, h_acc, conv_new = _ssd_step(u_z, u_xbc, u_dtx, ssm_all, conv_all, layer, lw, h_acc)
    (r, w, k, v, a, g), shift_new = _rwkv_prep(u_r, shift0, lw, bsz, False)
    o, s_new = _rwkv_core([r, w, k, v, a], wkv0, lw, bsz)
    x2, x2b = _layer_tail(alpha, x, xb, y_ssm, o, g, u_gate, lw)
    return x2, x2b, h_acc, conv_new, s_new, shift_new


def kernel(x_prompt, x_sample, state_ssm, state_conv, state_wkv, state_shift, meta_tokens, w_in, conv_w, conv_b,
           dt_bias, a_log, d_skip, ssm_norm_w, p_ssm, shift_mu, w0, w_lora_up, a0, a_lora_up, g_lora_up, k_k,
           k_a, r_k, lnx_g, lnx_b, p_rwkv, w_out, ln1_g, ln1_b, w_ffn_in, w_ffn_out, ln2_g, ln2_b):
    weights = {
        'w_in': w_in, 'conv_w': conv_w, 'conv_b': conv_b, 'dt_bias': dt_bias, 'a_log': a_log,
        'd_skip': d_skip, 'ssm_norm_w': ssm_norm_w, 'p_ssm': p_ssm, 'shift_mu': shift_mu,
        'w0': w0, 'w_lora_up': w_lora_up, 'a0': a0, 'a_lora_up': a_lora_up, 'g_lora_up': g_lora_up,
        'k_k': k_k, 'k_a': k_a, 'r_k': r_k, 'lnx_g': lnx_g, 'lnx_b': lnx_b, 'p_rwkv': p_rwkv,
        'w_out': w_out, 'ln1_g': ln1_g, 'ln1_b': ln1_b, 'w_ffn_in': w_ffn_in,
        'w_ffn_out': w_ffn_out, 'ln2_g': ln2_g, 'ln2_b': ln2_b,
    }
    depth = w_in.shape[0]
    alpha = (2 * depth) ** 0.25
    bsz, seq, d = x_prompt.shape
    n_meta = meta_tokens.shape[0]
    layers = [_prep_layer_weights(weights, l) for l in range(depth)]

    xm = jnp.pad(meta_tokens.astype(F32), ((0, SSD_CHUNK - n_meta), (0, 0)))
    xp = x_prompt.reshape(bsz * seq, d)
    xs = x_sample.reshape(x_sample.shape[0], d)
    xmb, xpb, xsb = xm.astype(BF16), _to_bf16(xp), xs.astype(BF16)
    bcast = lambda t: jnp.broadcast_to(t, (bsz,) + t.shape[1:])
    ssm_p, conv_p, wkv_p, shift_p, conv_s, wkv_s, shift_s = [], [], [], [], [], [], []
    ssm_all = state_ssm.reshape(state_ssm.shape[:2] + (-1, SSM_STATE))
    ssm_s = None
    for l in range(depth):
        lw = layers[l]
        z = lambda a: jnp.zeros((1,) + a.shape[2:], F32)
        xm, xmb, hm, cm, sm, shm = _seq_layer(alpha, xm, xmb, z(state_ssm), z(state_conv), z(state_wkv),
                                              z(state_shift), lw, 1, n_meta)
        xp, xpb, h, c, s, sh = _seq_layer(alpha, xp, xpb, bcast(hm), bcast(cm), bcast(sm), bcast(shm), lw,
                                          bsz, seq)
        ssm_p.append(h), conv_p.append(c), wkv_p.append(s), shift_p.append(sh)
        xs, xsb, ssm_s, c, s, sh = _step_layer(alpha, xs, xsb, ssm_all, state_conv, l, ssm_s, state_wkv[l],
                                               state_shift[l], lw)
        conv_s.append(c), wkv_s.append(s), shift_s.append(sh)
    st = jnp.stack
    return (xp.reshape(bsz, seq, d), xs.reshape(x_sample.shape), st(ssm_p), st(conv_p), st(wkv_p), st(shift_p),
            ssm_s.reshape(state_ssm.shape), st(conv_s), st(wkv_s), st(shift_s))
```

```python
import functools
import math

import jax
import jax.numpy as jnp
from jax import lax
from jax.experimental import pallas as pl
from jax.experimental.pallas import tpu as pltpu

F32 = jnp.float32
BF16 = jnp.bfloat16

LANES = 128
N_META = 16
HEAD_DIM = 64
SSM_STATE = 128
SSM_GROUPS = 4
CONV_K = 4
SSD_CHUNK = 128
CONV_PITCH = 3
SCAN_BLOCK = 64
SCAN_UNROLL = 32
SSD_STEP_SEQS = 4
MM_TN_CAP = 14 * LANES
LOG2E = 1.4426950408889634
LN_EPS = 1e-5
RMS_EPS = 1e-5
GN_EPS = 64e-5
VMEM_LIMIT_BYTES = 56 * 1024 * 1024


def _cparams(*sem):
    return pltpu.CompilerParams(dimension_semantics=sem, vmem_limit_bytes=VMEM_LIMIT_BYTES)


def _pick_tile(n, cap):
    if n <= cap:
        return n
    best = LANES
    for t in range(LANES, cap + 1, LANES):
        if n % t == 0:
            best = t
    return best


def _silu(x):
    return x * jax.nn.sigmoid(x)


def _softplus(x):
    return jnp.maximum(x, 0.0) + jnp.log(1.0 + jnp.exp(-jnp.abs(x)))


def _layer_norm(y, g, b):
    mu = jnp.mean(y, axis=-1, keepdims=True)
    d = y - mu
    var = jnp.mean(d * d, axis=-1, keepdims=True)
    return d * lax.rsqrt(var + LN_EPS) * g + b


def _cast_kernel(x_ref, o_ref):
    o_ref[...] = x_ref[...].astype(o_ref.dtype)


def _to_bf16(x):
    m, d = x.shape
    tm = _pick_tile(m, 1024)
    return pl.pallas_call(
        _cast_kernel,
        grid=(m // tm,),
        in_specs=[pl.BlockSpec((tm, d), lambda i: (i, 0))],
        out_specs=pl.BlockSpec((tm, d), lambda i: (i, 0)),
        out_shape=jax.ShapeDtypeStruct((m, d), BF16),
        compiler_params=_cparams("parallel"),
        name="to_bf16",
    )(x)


def _mm_kernel(x_ref, w_ref, o_ref):
    o_ref[...] = jnp.dot(x_ref[...], w_ref[...], preferred_element_type=F32).astype(o_ref.dtype)


def _matmul(x, w, out_dtype=F32):
    m, k = x.shape
    n = w.shape[1]
    tm = _pick_tile(m, 1024)
    tn = _pick_tile(n, MM_TN_CAP)
    return pl.pallas_call(
        _mm_kernel,
        grid=(n // tn, m // tm),
        in_specs=[pl.BlockSpec((tm, k), lambda j, i: (i, 0)),
                  pl.BlockSpec((k, tn), lambda j, i: (0, j))],
        out_specs=pl.BlockSpec((tm, tn), lambda j, i: (i, j)),
        out_shape=jax.ShapeDtypeStruct((m, n), out_dtype),
        compiler_params=_cparams("parallel", "parallel"),
        name="matmul",
    )(x, w)


def _merge_kernel(alpha, ys_ref, yr_ref, g_ref, xb_ref, x_ref, wg_ref, ps_ref, pr_ref, wo_ref, lg_ref, lb_ref,
                  o_ref, ob_ref):
    d = o_ref.shape[1]
    a = jnp.dot(ys_ref[...], ps_ref[...], preferred_element_type=F32)
    yr = (yr_ref[...] * g_ref[...]).astype(BF16)
    b = jnp.dot(yr, pr_ref[...], preferred_element_type=F32)
    gates = jax.nn.sigmoid(jnp.dot(xb_ref[...], wg_ref[...], preferred_element_type=F32))
    merged = gates[:, :d] * a + gates[:, d:] * b
    y = alpha * x_ref[...] + jnp.dot(merged.astype(BF16), wo_ref[...], preferred_element_type=F32)
    out = _layer_norm(y, lg_ref[...], lb_ref[...])
    o_ref[...] = out
    ob_ref[...] = out.astype(BF16)


def _merge_out_ln(alpha, y_ssm, y_rwkv, g_rwkv, xb, x, w_gate, p_ssm, p_rwkv, w_out, ln_g, ln_b):
    m, d = x.shape
    tm = _pick_tile(m, 512)
    row = lambda c: pl.BlockSpec((tm, c), lambda i: (i, 0))
    full = lambda a: pl.BlockSpec(a.shape, lambda i: (0, 0), pipeline_mode=pl.Buffered(1))
    return pl.pallas_call(
        functools.partial(_merge_kernel, alpha),
        grid=(m // tm,),
        in_specs=[row(y_ssm.shape[1]), row(d), row(d), row(d), row(d),
                  full(w_gate), full(p_ssm), full(p_rwkv), full(w_out), full(ln_g), full(ln_b)],
        out_specs=[row(d), row(d)],
        out_shape=[jax.ShapeDtypeStruct((m, d), F32), jax.ShapeDtypeStruct((m, d), BF16)],
        compiler_params=_cparams("parallel"),
        name="merge_out_ln",
    )(y_ssm, y_rwkv, g_rwkv, xb, x, w_gate, p_ssm, p_rwkv, w_out, ln_g, ln_b)


def _swiglu_kernel(x_ref, wg_ref, wu_ref, o_ref, wgb_scr, wub_scr):
    @pl.when(pl.program_id(1) == 0)
    def _():
        wgb_scr[...] = wg_ref[...].astype(BF16)
        wub_scr[...] = wu_ref[...].astype(BF16)

    x = x_ref[...]
    hg = jnp.dot(x, wgb_scr[...], preferred_element_type=F32)
    hu = jnp.dot(x, wub_scr[...], preferred_element_type=F32)
    o_ref[...] = (_silu(hg) * hu).astype(o_ref.dtype)


def _ffn_in(x, w_ffn_in_all, layer):
    m, k = x.shape
    dff = w_ffn_in_all.shape[2] // 2
    tm = _pick_tile(m, 1024)
    tn = _pick_tile(dff, 1408)
    nj = dff // tn
    return pl.pallas_call(
        _swiglu_kernel,
        grid=(nj, m // tm),
        in_specs=[pl.BlockSpec((tm, k), lambda j, i: (i, 0)),
                  pl.BlockSpec((None, k, tn), lambda j, i: (layer, 0, j)),
                  pl.BlockSpec((None, k, tn), lambda j, i: (layer, 0, j + nj))],
        out_specs=pl.BlockSpec((tm, tn), lambda j, i: (i, j)),
        out_shape=jax.ShapeDtypeStruct((m, dff), BF16),
        scratch_shapes=[pltpu.VMEM((k, tn), BF16)] * 2,
        compiler_params=_cparams("parallel", "arbitrary"),
        name="ffn_in_swiglu",
    )(x, w_ffn_in_all, w_ffn_in_all)


def _ffn_out_kernel(alpha, h_ref, w_ref, x_ref, lg_ref, lb_ref, o_ref, ob_ref):
    y = alpha * x_ref[...] + jnp.dot(h_ref[...], w_ref[...], preferred_element_type=F32)
    out = _layer_norm(y, lg_ref[...], lb_ref[...])
    o_ref[...] = out
    ob_ref[...] = out.astype(BF16)


def _ffn_out_ln(alpha, h, w_ffn_out, x, ln_g, ln_b):
    m, d = x.shape
    k = h.shape[1]
    tm = _pick_tile(m, 1024)
    row = lambda c: pl.BlockSpec((tm, c), lambda i: (i, 0))
    full = lambda a: pl.BlockSpec(a.shape, lambda i: (0, 0), pipeline_mode=pl.Buffered(1))
    return pl.pallas_call(
        functools.partial(_ffn_out_kernel, alpha),
        grid=(m // tm,),
        in_specs=[row(k), full(w_ffn_out), row(d), full(ln_g), full(ln_b)],
        out_specs=[row(d), row(d)],
        out_shape=[jax.ShapeDtypeStruct((m, d), F32), jax.ShapeDtypeStruct((m, d), BF16)],
        compiler_params=_cparams("parallel"),
        name="ffn_out_ln",
    )(h, w_ffn_out, x, ln_g, ln_b)


def _ssd_chunk_kernel(l_real, nchunks,
                      uxd_ref, udt_ref, xb_ref, wz_ref, h0_ref, cpre_ref, convw_ref, convb_ref, dtb_ref, alog_ref,
                      dskip_ref, normw_ref,
                      y_ref, hout_ref, ctail_ref,
                      ht_scr, cbuf_scr, xbc_scr, y_scr):
    q = SSD_CHUNK
    d_inner = y_ref.shape[1]
    gw = d_inner // SSM_GROUPS
    c = pl.program_id(1)

    @pl.when(c == 0)
    def _():
        for kb in range(d_inner // LANES):
            ht_scr[:, kb * LANES:(kb + 1) * LANES] = h0_ref[0, kb * LANES:(kb + 1) * LANES, :].T
        for ct in range(cbuf_scr.shape[0]):
            cbuf_scr[ct, pl.ds(5 * CONV_PITCH, CONV_K - 1, stride=CONV_PITCH), :] = \
                cpre_ref[0, :, ct * LANES:(ct + 1) * LANES]

    rows_at = lambda r0, n: pl.ds(r0 * CONV_PITCH, n, stride=CONV_PITCH)
    for ct in range(cbuf_scr.shape[0]):
        cols = slice(ct * LANES, (ct + 1) * LANES)
        cbuf_scr[ct, rows_at(8, q), :] = uxd_ref[:, cols]
        acc = convb_ref[:, cols] + cbuf_scr[ct, rows_at(5, q), :] * convw_ref[0:1, cols]
        for k in range(1, CONV_K):
            acc = acc + cbuf_scr[ct, rows_at(5 + k, q), :] * convw_ref[k:k + 1, cols]
        xbc_scr[:, cols] = _silu(acc)
        tail = cbuf_scr[ct, rows_at(l_real + 5, CONV_K - 1), :]
        cbuf_scr[ct, rows_at(5, CONV_K - 1), :] = tail

    rows = lax.broadcasted_iota(jnp.int32, (q, LANES), 0)
    dt = _softplus(udt_ref[...] + dtb_ref[...])
    if l_real < q:
        dt = jnp.where(rows < l_real, dt, 0.0)
    da = dt * (-jnp.exp(alog_ref[...]))
    acum = da
    s = 1
    while s < q:
        acum = acum + jnp.where(rows >= s, pltpu.roll(acum, s, axis=0), 0.0)
        s *= 2
    acum = acum * LOG2E
    acum_t = acum.T
    dt_t = dt.T
    a_last = acum[q - 1:q, :]
    st_t = dt_t * jnp.exp2(acum_t[:, q - 1:q] - acum_t)
    ii = lax.broadcasted_iota(jnp.int32, (q, q), 0)
    jj = lax.broadcasted_iota(jnp.int32, (q, q), 1)
    causal = ii >= jj
    low = lax.broadcasted_iota(jnp.int32, (q, LANES), 1) < HEAD_DIM

    for g in range(SSM_GROUPS):
        bm = xbc_scr[:, d_inner + g * SSM_STATE:d_inner + (g + 1) * SSM_STATE]
        cm = xbc_scr[:, d_inner + (SSM_GROUPS + g) * SSM_STATE:d_inner + (SSM_GROUPS + g + 1) * SSM_STATE]
        cmb = cm.astype(BF16)
        cb = lax.dot_general(cmb, bm.astype(BF16), (((1,), (1,)), ((), ())), preferred_element_type=F32)
        bm_t = bm.T
        for pr in range(gw // LANES):
            lanes = slice(g * gw + pr * LANES, g * gw + (pr + 1) * LANES)
            hd0 = (g * gw + pr * LANES) // HEAD_DIM
            x_pair = xbc_scr[:, lanes]
            xb = x_pair.astype(BF16)
            yo = jnp.dot(cmb, ht_scr[:, lanes].astype(BF16), preferred_element_type=F32)
            yd, st, acol, alast = [], [], [], []
            for hd in (hd0, hd0 + 1):
                a_col = jnp.broadcast_to(acum[:, hd:hd + 1], (q, q))
                lmat = jnp.exp2(jnp.where(causal, a_col - acum_t[hd:hd + 1, :], -jnp.inf))
                wd = (cb * lmat * dt_t[hd:hd + 1, :]).astype(BF16)
                yd.append(jnp.dot(wd, xb, preferred_element_type=F32))
                acol.append(a_col)
                st.append(jnp.dot((bm_t * st_t[hd:hd + 1, :]).astype(BF16), xb, preferred_element_type=F32))
                alast.append(jnp.broadcast_to(a_last[:, hd:hd + 1], (SSM_STATE, LANES)))
            ea = jnp.exp2(jnp.where(low, acol[0], acol[1]))
            y_scr[:, lanes] = jnp.where(low, yd[0], yd[1]) + yo * ea + x_pair * dskip_ref[:, lanes]
            cd = jnp.exp2(jnp.where(low, alast[0], alast[1]))
            ht_scr[:, lanes] = ht_scr[:, lanes] * cd + jnp.where(low, st[0], st[1])

    for g in range(SSM_GROUPS):
        cols = slice(g * gw, (g + 1) * gw)
        uz = jnp.dot(xb_ref[...], wz_ref[:, cols], preferred_element_type=F32)
        yg = y_scr[:, cols] * _silu(uz)
        ms = jnp.mean(yg * yg, axis=-1, keepdims=True)
        y_ref[:, cols] = (yg * lax.rsqrt(ms + RMS_EPS) * normw_ref[:, cols]).astype(y_ref.dtype)

    @pl.when(c == nchunks - 1)
    def _():
        for kb in range(d_inner // LANES):
            hout_ref[0, kb * LANES:(kb + 1) * LANES, :] = ht_scr[:, kb * LANES:(kb + 1) * LANES].T
        for ct in range(cbuf_scr.shape[0]):
            ctail_ref[0, :, ct * LANES:(ct + 1) * LANES] = \
                cbuf_scr[ct, pl.ds(5 * CONV_PITCH, CONV_K - 1, stride=CONV_PITCH), :]


def _ssd_chunked(u_xd, xb, h0, conv_pre, lw, bsz, l_real):
    m = u_xd.shape[0]
    conv_dim = conv_pre.shape[2]
    d_inner = lw['d_inner']
    q = SSD_CHUNK
    nchunks = m // bsz // q
    row = lambda cdim, cblk=0: pl.BlockSpec((q, cdim), lambda b, c: (b * nchunks + c, cblk))
    full = lambda a: pl.BlockSpec(a.shape, lambda b, c: (0, 0))
    per_b = lambda a: pl.BlockSpec((1,) + a.shape[1:], lambda b, c: (b, 0, 0))
    hshape = jax.ShapeDtypeStruct(h0.shape, F32)
    cshape = jax.ShapeDtypeStruct(conv_pre.shape, F32)
    return pl.pallas_call(
        functools.partial(_ssd_chunk_kernel, min(l_real, q), nchunks),
        grid=(bsz, nchunks),
        in_specs=[row(conv_dim), row(LANES, conv_dim // LANES), row(xb.shape[1]),
                  pl.BlockSpec(lw['w_z'].shape, lambda b, c: (0, 0), pipeline_mode=pl.Buffered(1)),
                  per_b(h0), per_b(conv_pre), full(lw['conv_w']), full(lw['conv_b']), full(lw['dt_bias']), full(lw['a_log']),
                  full(lw['d_skip']), full(lw['ssm_norm_w'])],
        out_specs=[row(d_inner), per_b(h0), per_b(conv_pre)],
        out_shape=[jax.ShapeDtypeStruct((m, d_inner), BF16), hshape, cshape],
        scratch_shapes=[pltpu.VMEM((SSM_STATE, d_inner), F32),
                        pltpu.VMEM((conv_dim // LANES, (q + 8) * CONV_PITCH, LANES), F32),
                        pltpu.VMEM((q, conv_dim), F32),
                        pltpu.VMEM((q, d_inner), F32)],
        compiler_params=_cparams("parallel", "arbitrary"),
        name="ssd_chunk",
    )(u_xd, u_xd, xb, lw['w_z'], h0, conv_pre, lw['conv_w'], lw['conv_b'], lw['dt_bias'], lw['a_log'],
      lw['d_skip'], lw['ssm_norm_w'])


def _row_to_col(row):
    r_i = lax.broadcasted_iota(jnp.int32, (LANES, LANES), 0)
    c_i = lax.broadcasted_iota(jnp.int32, (LANES, LANES), 1)
    return jnp.sum(jnp.where(r_i == c_i, jnp.broadcast_to(row, (LANES, LANES)), 0.0), axis=1, keepdims=True)


def _ssd_step_kernel(aliased, layer, uz_ref, uxbc_ref, udtx_ref, h0_ref, cpre_ref, convw_ref, convb_ref, dtbx_ref,
                     alogx_ref, dskip_ref, normw_ref, *rest):
    y_ref, hout_ref, ctail_ref, ht_scr = rest[1:] if aliased else rest
    nseq = uz_ref.shape[0]
    d_inner = uz_ref.shape[2]
    gw = d_inner // SSM_GROUPS
    slot = 0 if aliased else layer
    r_i = lax.broadcasted_iota(jnp.int32, (LANES, 3 * LANES), 0)
    c_i = lax.broadcasted_iota(jnp.int32, (LANES, 3 * LANES), 1)
    eye3 = jnp.where((c_i == r_i) | (c_i == r_i + LANES) | (c_i == r_i + 2 * LANES), 1.0, 0.0).astype(BF16)
    for s in range(nseq):
        u = uxbc_ref[s]
        pre = cpre_ref[0, s]
        acc = convb_ref[...] + u * convw_ref[CONV_K - 1:CONV_K, :]
        for k in range(CONV_K - 1):
            acc = acc + pre[k:k + 1, :] * convw_ref[k:k + 1, :]
        xbc = _silu(acc)
        ctail_ref[s, 0:CONV_K - 2, :] = pre[1:, :]
        ctail_ref[s, CONV_K - 2:CONV_K - 1, :] = u
        dt = _softplus(udtx_ref[s] + dtbx_ref[...])
        dec = jnp.exp(dt * (-jnp.exp(alogx_ref[...])))
        x_row = xbc[:, :d_inner]
        xdt = x_row * dt
        for kb in range(d_inner // LANES):
            ht_scr[s, :, kb * LANES:(kb + 1) * LANES] = h0_ref[0, s, kb * LANES:(kb + 1) * LANES, :].T
        y_parts = []
        for g in range(SSM_GROUPS):
            cols = slice(g * gw, (g + 1) * gw)
            bm = xbc[:, d_inner + g * SSM_STATE:d_inner + (g + 1) * SSM_STATE]
            cm = xbc[:, d_inner + (SSM_GROUPS + g) * SSM_STATE:d_inner + (SSM_GROUPS + g + 1) * SSM_STATE]
            hn = ht_scr[s, :, cols] * dec[:, cols] + _row_to_col(bm) * xdt[:, cols]
            ht_scr[s, :, cols] = hn
            y_parts.append(jnp.sum(hn * _row_to_col(cm), axis=0, keepdims=True))
        y_row = jnp.concatenate(y_parts, axis=1) + x_row * dskip_ref[...]
        y_row = y_row * _silu(uz_ref[s])
        outs = []
        for g in range(SSM_GROUPS):
            yg = y_row[:, g * gw:(g + 1) * gw]
            ms = jnp.mean(yg * yg, axis=-1, keepdims=True)
            outs.append(yg * lax.rsqrt(ms + RMS_EPS))
        y_ref[s] = (jnp.concatenate(outs, axis=1) * normw_ref[...]).astype(y_ref.dtype)
        for kb in range(d_inner // LANES):
            a = ht_scr[s, :, kb * LANES:(kb + 1) * LANES]
            hi = a.astype(BF16)
            rest1 = a - hi.astype(F32)
            mid = rest1.astype(BF16)
            lo = (rest1 - mid.astype(F32)).astype(BF16)
            pieces = jnp.concatenate([hi, mid, lo], axis=1)
            hout_ref[slot, s, kb * LANES:(kb + 1) * LANES, :] = lax.dot_general(
                eye3, pieces, (((1,), (1,)), ((), ())), preferred_element_type=F32)
        if not aliased:
            for other in range(hout_ref.shape[0]):
                if other != layer:
                    hout_ref[other, s] = jnp.zeros(hout_ref.shape[2:], F32)


def _ssd_step(u_z, u_xbc, u_dtx, ssm_all, conv_all, layer, lw, h_acc):
    bsz, d_inner = u_z.shape
    ns = SSD_STEP_SEQS if bsz % SSD_STEP_SEQS == 0 else 1
    r3 = lambda a: a.reshape(bsz, 1, a.shape[1])
    per_b = lambda a: pl.BlockSpec((ns,) + a.shape[1:], lambda b: (b, 0, 0))
    per_lb = lambda a: pl.BlockSpec((1, ns) + a.shape[2:], lambda b: (layer, b, 0, 0))
    full = lambda a: pl.BlockSpec(a.shape, lambda b: (0, 0))
    uz3, ux3, ud3 = r3(u_z), r3(u_xbc), r3(u_dtx)
    cshape = conv_all.shape[1:]
    args = [uz3, ux3, ud3, ssm_all, conv_all, lw['conv_w'], lw['conv_b'], lw['dt_bias_x'], lw['a_log_x'],
            lw['d_skip'], lw['ssm_norm_w']]
    in_specs = [per_b(uz3), per_b(ux3), per_b(ud3), per_lb(ssm_all), per_lb(conv_all)]
    in_specs += [full(a) for a in args[5:]]
    aliases = {}
    h_spec = pl.BlockSpec((ssm_all.shape[0], ns) + ssm_all.shape[2:], lambda b: (0, b, 0, 0))
    if h_acc is not None:
        args.append(h_acc)
        in_specs.append(pl.BlockSpec(memory_space=pl.ANY))
        aliases = {len(args) - 1: 1}
        h_spec = per_lb(ssm_all)
    y, h, ct = pl.pallas_call(
        functools.partial(_ssd_step_kernel, h_acc is not None, layer),
        grid=(bsz // ns,),
        in_specs=in_specs,
        out_specs=[per_b(uz3), h_spec, pl.BlockSpec((ns,) + cshape[1:], lambda b: (b, 0, 0))],
        out_shape=[jax.ShapeDtypeStruct((bsz, 1, d_inner), BF16), jax.ShapeDtypeStruct(ssm_all.shape, F32),
                   jax.ShapeDtypeStruct(cshape, F32)],
        scratch_shapes=[pltpu.VMEM((ns, SSM_STATE, d_inner), F32)],
        input_output_aliases=aliases,
        compiler_params=_cparams("arbitrary"),
        name="ssd_step",
    )(*args)
    return y.reshape(bsz, d_inner), h, ct


def _rwkv_prep_kernel(seq_mode, l_real, ur_ref, prev_ref, mu_ref, w0_ref, a0_ref, lora_ref, gup_ref,
                      r_ref, w_ref, k_ref, v_ref, a_ref, g_ref, last_ref, carry_scr):
    t, cols = ur_ref.shape
    d = r_ref.shape[1]
    u = ur_ref[...]
    if seq_mode:
        c = pl.program_id(1)

        @pl.when(c == 0)
        def _():
            carry_scr[...] = jnp.broadcast_to(prev_ref[0], carry_scr.shape)

        rows = lax.broadcasted_iota(jnp.int32, (t, cols), 0)
        prev = jnp.where(rows == 0, jnp.broadcast_to(carry_scr[0:1, :], (t, cols)), pltpu.roll(u, 1, axis=0))
        lrow = (l_real - 1) % t
        carry_scr[...] = jnp.broadcast_to(u[lrow:lrow + 1, :], carry_scr.shape)
        last_ref[0] = u[lrow:lrow + 1, :]
    else:
        prev = prev_ref[...]
        last_ref[...] = u
    sh = u + (prev - u) * mu_ref[...]
    r_ref[...] = sh[:, 0:d]
    k_ref[...] = sh[:, d:2 * d]
    v_ref[...] = sh[:, 2 * d:3 * d]
    la = sh[:, 3 * d:3 * d + LANES]
    lane = lax.broadcasted_iota(jnp.int32, la.shape, 1)
    la = jnp.where(lane < LANES // 2, jnp.tanh(la), la).astype(BF16)
    pre = jnp.dot(la, lora_ref[...], preferred_element_type=F32)
    w_log = -_softplus(-(w0_ref[...] + pre[:, :d])) - 0.5
    w_ref[...] = jnp.exp(-jnp.exp(w_log))
    a_ref[...] = jax.nn.sigmoid(a0_ref[...] + pre[:, d:])
    gl = jax.nn.sigmoid(sh[:, 3 * d + LANES:]).astype(BF16)
    g_ref[...] = jnp.dot(gl, gup_ref[...], preferred_element_type=F32)


def _rwkv_prep(u_r, shift_prev, lw, bsz, seq_mode, l_real=None):
    m, cols = u_r.shape
    d = lw['w0'].shape[1]
    full = lambda a: pl.BlockSpec(a.shape, lambda *_: (0,) * a.ndim)
    outs = [jax.ShapeDtypeStruct((m, d), F32)] * 6
    args = (lw['shift_mu'], lw['w0'], lw['a0'], lw['lora_up'], lw['g_lora_up'])
    if seq_mode:
        seq = m // bsz
        t = _pick_tile(seq, 256)
        nblk = seq // t
        l_real = seq if l_real is None else l_real
        row = lambda cdim: pl.BlockSpec((t, cdim), lambda b, c: (b * nblk + c, 0))
        prev3 = shift_prev.reshape(bsz, 1, cols)
        per_b = pl.BlockSpec((1, 1, cols), lambda b, c: (b, 0, 0))
        res = pl.pallas_call(
            functools.partial(_rwkv_prep_kernel, True, l_real),
            grid=(bsz, nblk),
            in_specs=[row(cols), per_b] + [full(a) for a in args],
            out_specs=[row(d)] * 6 + [per_b],
            out_shape=outs + [jax.ShapeDtypeStruct((bsz, 1, cols), F32)],
            scratch_shapes=[pltpu.VMEM((8, cols), F32)],
            compiler_params=_cparams("parallel", "arbitrary"),
            name="rwkv_prep_seq",
        )(u_r, prev3, *args)
        return res[:6], res[6].reshape(bsz, cols)
    t = _pick_tile(m, 256)
    row = lambda cdim: pl.BlockSpec((t, cdim), lambda i: (i, 0))
    res = pl.pallas_call(
        functools.partial(_rwkv_prep_kernel, False, None),
        grid=(m // t,),
        in_specs=[row(cols), row(cols)] + [full(a) for a in args],
        out_specs=[row(d)] * 6 + [row(cols)],
        out_shape=outs + [jax.ShapeDtypeStruct((m, cols), F32)],
        scratch_shapes=[pltpu.VMEM((8, cols), F32)],
        compiler_params=_cparams("parallel"),
        name="rwkv_prep_step",
    )(u_r, shift_prev, *args)
    return res[:6], res[6]


def _rwkv_scan_kernel(tb, r_ref, w_ref, k_ref, v_ref, a_ref, s0_ref, kk_ref, ka_ref, rk_ref, lg_ref, lb_ref,
                      o_ref, sout_ref, s_scr, g_scr, sa_scr, nkk_scr, b_scr, km_scr, rt_scr, bon_scr):
    hd = HEAD_DIM
    blk = pl.program_id(1)
    pad_rows = o_ref.shape[1] - hd

    @pl.when(blk == 0)
    def _():
        s_scr[...] = s0_ref[...]

    def prep(t, slot):
        k = k_ref[t, 0:hd, :]
        a = a_ref[t, 0:hd, :]
        r = r_ref[t, 0:hd, :]
        kk = k * kk_ref[...]
        kk = kk * lax.rsqrt(jnp.maximum(jnp.sum(kk * kk, axis=0, keepdims=True), 1e-24))
        g_prev = g_scr[...]
        g = g_prev * w_ref[t, 0:hd, :]
        g_scr[...] = g
        ginv = 1.0 / g
        nkk_scr[slot] = -(kk * g_prev)
        b_scr[slot] = kk * a * ginv
        km = k * (1.0 + (a - 1.0) * ka_ref[...])
        km_scr[slot] = km * ginv
        rt_scr[slot] = r * g
        bon_scr[slot] = jnp.broadcast_to(jnp.sum(r * km * rk_ref[...], axis=0, keepdims=True), (8, LANES))

    def step(t, slot, has_next):
        nslot = 1 - slot
        if has_next:
            prep(t + 1, nslot)
        v = v_ref[t, 0:hd, :]
        sa = sa_scr[...]
        zero = jnp.zeros((hd, LANES), F32)

        def jbody(j, carry):
            o, sa_next = carry
            sj = s_scr[j] + sa * b_scr[slot, pl.ds(j, 1), :] + v * km_scr[slot, pl.ds(j, 1), :]
            s_scr[j] = sj
            o = o + sj * rt_scr[slot, pl.ds(j, 1), :]
            if has_next:
                sa_next = sa_next + sj * nkk_scr[nslot, pl.ds(j, 1), :]
            return o, sa_next

        o, sa_next = lax.fori_loop(0, hd, jbody, (zero, zero), unroll=SCAN_UNROLL)
        if has_next:
            sa_scr[...] = sa_next
        mu = jnp.mean(o, axis=0, keepdims=True)
        dlt = o - mu
        var = jnp.mean(dlt * dlt, axis=0, keepdims=True)
        on = dlt * lax.rsqrt(var + GN_EPS) * lg_ref[...] + lb_ref[...]
        o_ref[t, 0:hd, :] = on + bon_scr[slot, 0:1, :] * v
        if pad_rows:
            o_ref[t, hd:hd + pad_rows, :] = jnp.zeros((pad_rows, LANES), F32)

    g_scr[...] = jnp.ones((hd, LANES), F32)
    prep(0, 0)
    sa0 = s_scr[0] * nkk_scr[0, 0:1, :]
    for j in range(1, hd):
        sa0 = sa0 + s_scr[j] * nkk_scr[0, j:j + 1, :]
    sa_scr[...] = sa0

    if tb >= 2:
        def pair(p, carry):
            step(2 * p, 0, True)
            step(2 * p + 1, 1, True)
            return carry

        lax.fori_loop(0, tb // 2 - 1, pair, 0)
        step(tb - 2, 0, True)
        step(tb - 1, 1, False)
    else:
        step(0, 0, False)

    for j in range(hd):
        s_scr[j] = s_scr[j] * g_scr[j:j + 1, :]

    @pl.when(blk == pl.num_programs(1) - 1)
    def _():
        sout_ref[...] = s_scr[...]


def _rwkv_scan(r, w, k, v, a, s0, lw_c):
    t_total, rows, c = r.shape
    hd = HEAD_DIM
    tb = SCAN_BLOCK if t_total % SCAN_BLOCK == 0 else t_total
    assert tb == 1 or tb % 2 == 0
    nb = t_total // tb
    ncl = c // LANES
    seq = pl.BlockSpec((tb, rows, LANES), lambda cl, i: (i, 0, cl))
    st = pl.BlockSpec((hd, hd, LANES), lambda cl, i: (0, 0, cl))
    par = pl.BlockSpec((hd, LANES), lambda cl, i: (0, cl))
    return pl.pallas_call(
        functools.partial(_rwkv_scan_kernel, tb),
        grid=(ncl, nb),
        in_specs=[seq] * 5 + [st] + [par] * 5,
        out_specs=[seq, st],
        out_shape=[jax.ShapeDtypeStruct((t_total, rows, c), F32), jax.ShapeDtypeStruct((hd, hd, c), F32)],
        scratch_shapes=[pltpu.VMEM((hd, hd, LANES), F32)] + [pltpu.VMEM((hd, LANES), F32)] * 2
        + [pltpu.VMEM((2, hd, LANES), F32)] * 4 + [pltpu.VMEM((2, 8, LANES), F32)],
        compiler_params=_cparams("parallel", "arbitrary"),
        name="rwkv_scan",
    )(r, w, k, v, a, s0, lw_c['k_k'], lw_c['k_a'], lw_c['r_k'], lw_c['lnx_g'], lw_c['lnx_b'])


def _to_chain(x, bsz, heads):
    t = x.shape[0] // bsz
    return x.reshape(bsz, t, heads, HEAD_DIM).transpose(1, 3, 0, 2).reshape(t, HEAD_DIM, bsz * heads)


def _from_chain(x, bsz, heads):
    t = x.shape[0]
    return x.reshape(t, HEAD_DIM, bsz, heads).transpose(2, 0, 3, 1).reshape(bsz * t, heads * HEAD_DIM)


CHAIN_PITCH = HEAD_DIM + 8


def _chain_from_u_kernel(kind, u_ref, prev_ref, mu_ref, *rest):
    if kind == 'lerp':
        o_ref, y_scr, carry_scr = rest
    else:
        p0_ref, lora_ref, o_ref, y_scr, carry_scr = rest
    nb, t, wdt = u_ref.shape

    @pl.when(pl.program_id(0) == 0)
    def _():
        carry_scr[...] = prev_ref[:, 0, :]

    rows = lax.broadcasted_iota(jnp.int32, (t, wdt), 0)
    for b in range(nb):
        u = u_ref[b]
        prev = jnp.where(rows == 0, jnp.broadcast_to(carry_scr[b:b + 1, :], (t, wdt)), pltpu.roll(u, 1, axis=0))
        carry_scr[b:b + 1, :] = u[t - 1:t, :]
        sh = u + (prev - u) * mu_ref[...]
        if kind == 'lerp':
            x = sh
        else:
            lane = lax.broadcasted_iota(jnp.int32, sh.shape, 1)
            la = jnp.where(lane < LANES // 2, jnp.tanh(sh), sh).astype(BF16)
            pre = p0_ref[...] + jnp.dot(la, lora_ref[...], preferred_element_type=F32)
            if kind == 'decay':
                x = jnp.exp(-jnp.exp(-_softplus(-pre) - 0.5))
            else:
                x = jax.nn.sigmoid(pre)
        nk = x.shape[1] // LANES
        for k in range(nk):
            tt = x[:, k * LANES:(k + 1) * LANES].T
            m0 = b * 2 * nk + 2 * k
            y_scr[m0 * CHAIN_PITCH:m0 * CHAIN_PITCH + HEAD_DIM, :] = tt[:HEAD_DIM, :]
            y_scr[(m0 + 1) * CHAIN_PITCH:(m0 + 1) * CHAIN_PITCH + HEAD_DIM, :] = tt[HEAD_DIM:, :]
    for j in range(HEAD_DIM):
        g = y_scr[pl.ds(j, LANES, stride=CHAIN_PITCH), :]
        o_ref[pl.ds(j, LANES, stride=CHAIN_PITCH), :] = g.T
    for j in range(HEAD_DIM, CHAIN_PITCH):
        o_ref[pl.ds(j, LANES, stride=CHAIN_PITCH), :] = jnp.zeros((LANES, LANES), F32)


def _chain_from_u(kind, u_r, shift_prev, lw, bsz):
    m, cols = u_r.shape
    t_total = m // bsz
    d = lw['w0'].shape[1]
    if kind in ('r', 'k', 'v'):
        width, cb, kern_kind, extra = d, 'rkv'.index(kind), 'lerp', []
    else:
        width, cb, kern_kind = LANES, 3 * d // LANES, kind
        p0 = lw['w0'] if kind == 'decay' else lw['a0']
        half = 0 if kind == 'decay' else 1
        extra = [(p0, pl.BlockSpec((1, d), lambda i: (0, 0))),
                 (lw['lora_up'], pl.BlockSpec((LANES, d), lambda i: (0, half)))]
    args = [u_r.reshape(bsz, t_total, cols), shift_prev.reshape(bsz, 1, cols), lw['shift_mu']] + [a for a, _ in extra]
    in_specs = [pl.BlockSpec((bsz, LANES, width), lambda i: (0, i, cb)),
                pl.BlockSpec((bsz, 1, width), lambda i: (0, 0, cb)),
                pl.BlockSpec((1, width), lambda i: (0, cb))] + [s for _, s in extra]
    out = pl.pallas_call(
        functools.partial(_chain_from_u_kernel, kern_kind),
        grid=(t_total // LANES,),
        in_specs=in_specs,
        out_specs=pl.BlockSpec((LANES * CHAIN_PITCH, LANES), lambda i: (i, 0)),
        out_shape=jax.ShapeDtypeStruct((t_total * CHAIN_PITCH, LANES), F32),
        scratch_shapes=[pltpu.VMEM((LANES * CHAIN_PITCH, LANES), F32), pltpu.VMEM((bsz, width), F32)],
        compiler_params=_cparams("arbitrary"),
        name="chain_" + kern_kind,
    )(*args)
    return out.reshape(t_total, CHAIN_PITCH, LANES)


def _gate_kernel(u_ref, prev_ref, mu_ref, gup_ref, g_ref, carry_scr):
    t, wdt = u_ref.shape

    @pl.when(pl.program_id(1) == 0)
    def _():
        carry_scr[...] = jnp.broadcast_to(prev_ref[0], carry_scr.shape)

    u = u_ref[...]
    rows = lax.broadcasted_iota(jnp.int32, (t, wdt), 0)
    prev = jnp.where(rows == 0, jnp.broadcast_to(carry_scr[0:1, :], (t, wdt)), pltpu.roll(u, 1, axis=0))
    carry_scr[...] = jnp.broadcast_to(u[t - 1:t, :], carry_scr.shape)
    sh = u + (prev - u) * mu_ref[...]
    gl = jax.nn.sigmoid(sh[:, wdt - LANES:]).astype(BF16)
    g_ref[...] = jnp.dot(gl, gup_ref[...], preferred_element_type=F32)


def _rwkv_gate(u_r, shift_prev, lw, bsz):
    m, cols = u_r.shape
    d = lw['w0'].shape[1]
    seq = m // bsz
    t = _pick_tile(seq, 512)
    nblk = seq // t
    wdt = 2 * LANES
    cb = cols // wdt - 1
    assert cols % wdt == 0
    return pl.pallas_call(
        _gate_kernel,
        grid=(bsz, nblk),
        in_specs=[pl.BlockSpec((t, wdt), lambda b, c: (b * nblk + c, cb)),
                  pl.BlockSpec((1, 1, wdt), lambda b, c: (b, 0, cb)),
                  pl.BlockSpec((1, wdt), lambda b, c: (0, cb)),
                  pl.BlockSpec(lw['g_lora_up'].shape, lambda b, c: (0, 0))],
        out_specs=pl.BlockSpec((t, d), lambda b, c: (b * nblk + c, 0)),
        out_shape=jax.ShapeDtypeStruct((m, d), F32),
        scratch_shapes=[pltpu.VMEM((8, wdt), F32)],
        compiler_params=_cparams("parallel", "arbitrary"),
        name="rwkv_gate",
    )(u_r, shift_prev.reshape(bsz, 1, cols), lw['shift_mu'], lw['g_lora_up'])


def _from_chain_kernel(x_ref, o_ref, y_scr):
    nb = o_ref.shape[0]
    nk = o_ref.shape[2] // LANES
    for i in range(HEAD_DIM):
        g = x_ref[pl.ds(i, LANES, stride=CHAIN_PITCH), :]
        y_scr[pl.ds(i, LANES, stride=CHAIN_PITCH), :] = g.T
    for b in range(nb):
        for k in range(nk):
            m0 = b * 2 * nk + 2 * k
            tt = jnp.concatenate([y_scr[m0 * CHAIN_PITCH:m0 * CHAIN_PITCH + HEAD_DIM, :],
                                  y_scr[(m0 + 1) * CHAIN_PITCH:(m0 + 1) * CHAIN_PITCH + HEAD_DIM, :]], axis=0)
            o_ref[b, :, k * LANES:(k + 1) * LANES] = tt.T


def _from_chain_pallas(x, bsz, d):
    t_total = x.shape[0]
    out = pl.pallas_call(
        _from_chain_kernel,
        grid=(t_total // LANES,),
        in_specs=[pl.BlockSpec((LANES * CHAIN_PITCH, LANES), lambda i: (i, 0))],
        out_specs=pl.BlockSpec((bsz, LANES, d), lambda i: (0, i, 0)),
        out_shape=jax.ShapeDtypeStruct((bsz, t_total, d), F32),
        scratch_shapes=[pltpu.VMEM((LANES * CHAIN_PITCH, LANES), F32)],
        compiler_params=_cparams("parallel"),
        name="from_chain",
    )(x.reshape(t_total * CHAIN_PITCH, LANES))
    return out.reshape(bsz * t_total, d)


def _chain_param(p, bsz, heads):
    return jnp.tile(p.reshape(heads, HEAD_DIM).T, (1, bsz))


def _pad_lanes(x, c_pad):
    c = x.shape[-1]
    if c == c_pad:
        return x
    return jnp.pad(x, [(0, 0)] * (x.ndim - 1) + [(0, c_pad - c)])


def _rwkv_core(rwkva, s0, lw, bsz, chain_ready=False):
    d = lw['w0'].shape[1]
    heads = d // HEAD_DIM
    c = bsz * heads
    c_pad = -(-c // LANES) * LANES
    if chain_ready:
        chain = rwkva
    else:
        chain = [_pad_lanes(_to_chain(x, bsz, heads), c_pad) for x in rwkva]
    s0c = _pad_lanes(s0.transpose(3, 2, 0, 1).reshape(HEAD_DIM, HEAD_DIM, c), c_pad)
    lw_c = {n: _pad_lanes(_chain_param(lw[n], bsz, heads), c_pad) for n in ('k_k', 'k_a', 'r_k', 'lnx_g', 'lnx_b')}
    o, s = _rwkv_scan(*chain, s0c, lw_c)
    if chain_ready:
        o = _from_chain_pallas(o, bsz, d)
    else:
        o = _from_chain(o[..., :c], bsz, heads)
    s = s[..., :c].reshape(HEAD_DIM, HEAD_DIM, bsz, heads).transpose(2, 3, 1, 0)
    return o, s


def _prep_layer_weights(w, l):
    d = w['w_in'].shape[1]
    d_inner = w['p_ssm'].shape[1]
    heads_ssm = w['dt_bias'].shape[1]
    conv_dim = w['conv_w'].shape[2]
    rcols = w['shift_mu'].shape[1]
    dl = w['w_lora_up'].shape[1]
    s0, s1, s2 = d_inner, d_inner + conv_dim, d_inner + conv_dim + heads_ssm
    s3 = s2 + rcols
    w_in = w['w_in'][l]
    pad_h = lambda v, fill: jnp.pad(v, (0, LANES - heads_ssm), constant_values=fill).reshape(1, LANES)
    row = lambda v: v.reshape(1, -1)
    lora = jnp.zeros((LANES, 2 * d), F32)
    lora = lora.at[:dl, :d].set(w['w_lora_up'][l]).at[dl:dl + w['a_lora_up'].shape[1], d:].set(w['a_lora_up'][l])
    return {
        'w_z': w_in[:, :s0].astype(BF16),
        'w_xd': jnp.pad(w_in[:, s0:s2], ((0, 0), (0, -(s2 - s0) % MM_TN_CAP))).astype(BF16),
        'layer': l, 'd_inner': d_inner, 'w_ffn_in_all': w['w_ffn_in'],
        'w_dt_x': jnp.repeat(w_in[:, s1:s2], HEAD_DIM, axis=1).astype(BF16),
        'dt_bias_x': row(jnp.repeat(w['dt_bias'][l], HEAD_DIM)),
        'a_log_x': row(jnp.repeat(w['a_log'][l], HEAD_DIM)),
        'w_r': w_in[:, s2:s3].astype(BF16), 'w_gate': w_in[:, s3:].astype(BF16),
        'conv_w': w['conv_w'][l], 'conv_b': row(w['conv_b'][l]),
        'dt_bias': pad_h(w['dt_bias'][l], 0.0), 'a_log': pad_h(w['a_log'][l], 0.0),
        'd_skip': row(jnp.repeat(w['d_skip'][l], HEAD_DIM)),
        'ssm_norm_w': row(w['ssm_norm_w'][l]),
        'p_ssm': w['p_ssm'][l].astype(BF16),
        'shift_mu': row(w['shift_mu'][l]), 'w0': row(w['w0'][l]), 'a0': row(w['a0'][l]),
        'lora_up': lora.astype(BF16), 'g_lora_up': w['g_lora_up'][l].astype(BF16),
        'k_k': w['k_k'][l], 'k_a': w['k_a'][l], 'r_k': w['r_k'][l].reshape(-1),
        'lnx_g': w['lnx_g'][l], 'lnx_b': w['lnx_b'][l],
        'p_rwkv': w['p_rwkv'][l].astype(BF16), 'w_out': w['w_out'][l].astype(BF16),
        'ln1_g': row(w['ln1_g'][l]), 'ln1_b': row(w['ln1_b'][l]),
        'w_ffn_out': w['w_ffn_out'][l].astype(BF16),
        'ln2_g': row(w['ln2_g'][l]), 'ln2_b': row(w['ln2_b'][l]),
    }


def _layer_tail(alpha, x, xb, y_ssm, o_rwkv, g_rwkv, lw):
    x1, x1b = _merge_out_ln(alpha, y_ssm, o_rwkv, g_rwkv, xb, x, lw['w_gate'], lw['p_ssm'], lw['p_rwkv'],
                            lw['w_out'], lw['ln1_g'], lw['ln1_b'])
    hmid = _ffn_in(x1b, lw['w_ffn_in_all'], lw['layer'])
    return _ffn_out_ln(alpha, hmid, lw['w_ffn_out'], x1, lw['ln2_g'], lw['ln2_b'])


def _in_proj(xb, lw, dt_expanded=False):
    dtx = _matmul(xb, lw['w_dt_x']) if dt_expanded else None
    return _matmul(xb, lw['w_xd']), dtx, _matmul(xb, lw['w_r'])


def _seq_layer(alpha, x, xb, ssm0, conv0, wkv0, shift0, lw, bsz, l_real):
    l_pad = x.shape[0] // bsz
    u_zxd, _, u_r = _in_proj(xb, lw)
    h0 = ssm0.reshape(bsz, -1, SSM_STATE)
    y_ssm, h_new, conv_new = _ssd_chunked(u_zxd, xb, h0, conv0, lw, bsz, l_real)
    heads_r = lw['w0'].shape[1] // HEAD_DIM
    if bsz * heads_r == LANES and l_real == l_pad:
        chain = [_chain_from_u(kind, u_r, shift0, lw, bsz) for kind in ('r', 'decay', 'k', 'v', 'rate')]
        g = _rwkv_gate(u_r, shift0, lw, bsz)
        shift_new = u_r.reshape(bsz, l_pad, -1)[:, -1]
        o, s_new = _rwkv_core(chain, wkv0, lw, bsz, chain_ready=True)
        x2, x2b = _layer_tail(alpha, x, xb, y_ssm, o, g, lw)
        return x2, x2b, h_new.reshape(ssm0.shape), conv_new, s_new, shift_new
    (r, w, k, v, a, g), shift_new = _rwkv_prep(u_r, shift0, lw, bsz, True, l_real)
    if l_real < l_pad:
        cut = lambda t: t.reshape(bsz, l_pad, -1)[:, :l_real].reshape(bsz * l_real, -1)
        o, s_new = _rwkv_core([cut(t) for t in (r, w, k, v, a)], wkv0, lw, bsz)
        o = jnp.pad(o.reshape(bsz, l_real, -1), ((0, 0), (0, l_pad - l_real), (0, 0))).reshape(bsz * l_pad, -1)
    else:
        o, s_new = _rwkv_core([r, w, k, v, a], wkv0, lw, bsz)
    x2, x2b = _layer_tail(alpha, x, xb, y_ssm, o, g, lw)
    return x2, x2b, h_new.reshape(ssm0.shape), conv_new, s_new, shift_new


def _step_layer(alpha, x, xb, ssm_all, conv_all, layer, h_acc, wkv0, shift0, lw):
    bsz = x.shape[0]
    u_zxd, u_dtx, u_r = _in_proj(xb, lw, dt_expanded=True)
    u_z, u_xbc = _matmul(xb, lw['w_z']), u_zxd[:, :conv_all.shape[3]]
    y_ssm, h_acc, conv_new = _ssd_step(u_z, u_xbc, u_dtx, ssm_all, conv_all, layer, lw, h_acc)
    (r, w, k, v, a, g), shift_new = _rwkv_prep(u_r, shift0, lw, bsz, False)
    o, s_new = _rwkv_core([r, w, k, v, a], wkv0, lw, bsz)
    x2, x2b = _layer_tail(alpha, x, xb, y_ssm, o, g, lw)
    return x2, x2b, h_acc, conv_new, s_new, shift_new


def kernel(x_prompt, x_sample, state_ssm, state_conv, state_wkv, state_shift, meta_tokens, w_in, conv_w, conv_b,
           dt_bias, a_log, d_skip, ssm_norm_w, p_ssm, shift_mu, w0, w_lora_up, a0, a_lora_up, g_lora_up, k_k,
           k_a, r_k, lnx_g, lnx_b, p_rwkv, w_out, ln1_g, ln1_b, w_ffn_in, w_ffn_out, ln2_g, ln2_b):
    weights = {
        'w_in': w_in, 'conv_w': conv_w, 'conv_b': conv_b, 'dt_bias': dt_bias, 'a_log': a_log,
        'd_skip': d_skip, 'ssm_norm_w': ssm_norm_w, 'p_ssm': p_ssm, 'shift_mu': shift_mu,
        'w0': w0, 'w_lora_up': w_lora_up, 'a0': a0, 'a_lora_up': a_lora_up, 'g_lora_up': g_lora_up,
        'k_k': k_k, 'k_a': k_a, 'r_k': r_k, 'lnx_g': lnx_g, 'lnx_b': lnx_b, 'p_rwkv': p_rwkv,
        'w_out': w_out, 'ln1_g': ln1_g, 'ln1_b': ln1_b, 'w_ffn_in': w_ffn_in,
        'w_ffn_out': w_ffn_out, 'ln2_g': ln2_g, 'ln2_b': ln2_b,
    }
    depth = w_in.shape[0]
    alpha = (2 * depth) ** 0.25
    bsz, seq, d = x_prompt.shape
    n_meta = meta_tokens.shape[0]
    layers = [_prep_layer_weights(weights, l) for l in range(depth)]

    xm = jnp.pad(meta_tokens.astype(F32), ((0, SSD_CHUNK - n_meta), (0, 0)))
    xp = x_prompt.reshape(bsz * seq, d)
    xs = x_sample.reshape(x_sample.shape[0], d)
    xmb, xpb, xsb = xm.astype(BF16), _to_bf16(xp), xs.astype(BF16)
    bcast = lambda t: jnp.broadcast_to(t, (bsz,) + t.shape[1:])
    ssm_p, conv_p, wkv_p, shift_p, conv_s, wkv_s, shift_s = [], [], [], [], [], [], []
    ssm_all = state_ssm.reshape(state_ssm.shape[:2] + (-1, SSM_STATE))
    ssm_s = None
    for l in range(depth):
        lw = layers[l]
        z = lambda a: jnp.zeros((1,) + a.shape[2:], F32)
        xm, xmb, hm, cm, sm, shm = _seq_layer(alpha, xm, xmb, z(state_ssm), z(state_conv), z(state_wkv),
                                              z(state_shift), lw, 1, n_meta)
        xp, xpb, h, c, s, sh = _seq_layer(alpha, xp, xpb, bcast(hm), bcast(cm), bcast(sm), bcast(shm), lw,
                                          bsz, seq)
        ssm_p.append(h), conv_p.append(c), wkv_p.append(s), shift_p.append(sh)
        xs, xsb, ssm_s, c, s, sh = _step_layer(alpha, xs, xsb, ssm_all, state_conv, l, ssm_s, state_wkv[l],
                                               state_shift[l], lw)
        conv_s.append(c), wkv_s.append(s), shift_s.append(sh)
    st = jnp.stack
    return (xp.reshape(bsz, seq, d), xs.reshape(x_sample.shape), st(ssm_p), st(conv_p), st(wkv_p), st(shift_p),
            ssm_s.reshape(state_ssm.shape), st(conv_s), st(wkv_s), st(shift_s))
```
